```python
import math
import jax, jax.numpy as jnp
from jax import lax
import numpy as np

D_MODEL = 2048
BATCH = 4
SEQ = 2048
DEPTH = 1
DEC_BATCH = 128
DEC_SEQ = 4
PAST_LEN = 2048
PAGE_SIZE = 128

MIX_WIDTH = D_MODEL
NSA_WIDTH = MIX_WIDTH // 2
CONV_DIM = MIX_WIDTH - NSA_WIDTH
HEAD_DIM = 128
N_HEADS = NSA_WIDTH // HEAD_DIM
N_KV_HEADS = 2
CMP_BLOCK = 32
CMP_STRIDE = 16
SEL_BLOCK = 64
N_SELECT = 16
WINDOW = 512
CONV_WIDTH = 31
N_CACHE_SLOTS = 4
N_WIN_SLOTS = 2
N_BRANCH = 3
Q_COLS = N_HEADS * HEAD_DIM
KV_COLS = N_CACHE_SLOTS * N_KV_HEADS * HEAD_DIM
WIN_COLS = N_WIN_SLOTS * N_KV_HEADS * HEAD_DIM
GATE_COLS = N_BRANCH * N_HEADS
IN_COLS = Q_COLS + KV_COLS + WIN_COLS + GATE_COLS + NSA_WIDTH + 2 * CONV_DIM + CONV_DIM
SEL_QROWS = 128
WIN_QBLK = 128
NORM_EPS = 1e-6
NEG_INF = -1e30
FORCE_BONUS = 1e6

kernel_name = 'nsa_conformer_hybrid_step'


def _rmsnorm(x, g):
    x32 = x.astype(jnp.float32)
    y = x32 * lax.rsqrt(jnp.mean(x32 * x32, -1, keepdims=True) + NORM_EPS)
    return (y * g.astype(jnp.float32)).astype(x.dtype)


def _layernorm(x, g, b):
    x32 = x.astype(jnp.float32)
    mu = jnp.mean(x32, -1, keepdims=True)
    var = jnp.mean(jnp.square(x32 - mu), -1, keepdims=True)
    return ((x32 - mu) * lax.rsqrt(var + NORM_EPS) * g.astype(jnp.float32) + b.astype(jnp.float32)).astype(x.dtype)


def _masked_softmax(s, mask):
    s = jnp.where(mask, s, NEG_INF)
    e = jnp.where(mask, jnp.exp(s - jnp.max(s, -1, keepdims=True)), 0.0)
    return e / jnp.maximum(jnp.sum(e, -1, keepdims=True), 1e-30)


def _block_size(n, target):
    b = max(1, min(n, target))
    while n % b:
        b -= 1
    return b


def _compress(kv_cmp, pe, w1, b1, w2, b2):
    L = kv_cmp.shape[1]
    n_cmp = (L - CMP_BLOCK) // CMP_STRIDE + 1
    idx = np.arange(n_cmp)[:, None] * CMP_STRIDE + np.arange(CMP_BLOCK)[None, :]
    blocks = kv_cmp[:, idx]
    h = jnp.einsum('bnlsgd,slde->bnsge', blocks, w1) + (jnp.einsum('sld,slde->se', pe, w1) + b1)[:, None, :]
    h = jax.nn.silu(h)
    return jnp.einsum('bnsge,sef->bnsgf', h, w2) + b2[:, None, :]


def _cmp_attend(q, kc, vc, qpos):
    B, T = q.shape[:2]
    qg = q.reshape(B, T, N_KV_HEADS, N_HEADS // N_KV_HEADS, HEAD_DIM)
    ends = jnp.arange(kc.shape[1]) * CMP_STRIDE + CMP_BLOCK - 1
    s = jnp.einsum('btgrd,bngd->bgrtn', qg, kc, preferred_element_type=jnp.float32) * HEAD_DIM ** -0.5
    p = _masked_softmax(s, ends[None, :] <= qpos[:, None])
    o = jnp.einsum('bgrtn,bngd->btgrd', p.astype(vc.dtype), vc)
    return o.reshape(B, T, N_HEADS, HEAD_DIM), p


def _select_blocks(p_cmp, qpos, n_sel):
    n_cmp = p_cmp.shape[-1]
    ci = np.arange(n_cmp)[:, None] * CMP_STRIDE
    sj = np.arange(n_sel)[None, :] * SEL_BLOCK
    overlap = jnp.asarray(((ci < sj + SEL_BLOCK) & (ci + CMP_BLOCK > sj)).astype(np.float32))
    imp = jnp.einsum('bgtn,ns->bgts', jnp.sum(p_cmp, axis=2), overlap)
    j = jnp.arange(n_sel)[None, :]
    cur = (qpos // SEL_BLOCK)[:, None]
    allowed = j * SEL_BLOCK <= qpos[:, None]
    forced = (j == 0) | (j == cur) | (j == cur - 1)
    score = jnp.where(allowed, imp + jnp.where(forced, FORCE_BONUS, 0.0), NEG_INF)
    vals, idx = lax.top_k(score, min(N_SELECT, n_sel))
    return idx, vals > 0.5 * NEG_INF


def _select_attend(q, ks, vs, sel_idx, sel_valid, qpos):
    B, T = q.shape[:2]
    L = ks.shape[1]
    n_sel = -(-L // SEL_BLOCK)
    pad = ((0, 0), (0, n_sel * SEL_BLOCK - L), (0, 0), (0, 0))
    kb = jnp.pad(ks, pad).reshape(B, n_sel, SEL_BLOCK, N_KV_HEADS, HEAD_DIM).transpose(0, 3, 1, 2, 4)
    vb = jnp.pad(vs, pad).reshape(B, n_sel, SEL_BLOCK, N_KV_HEADS, HEAD_DIM).transpose(0, 3, 1, 2, 4)
    r = N_HEADS // N_KV_HEADS
    qg = q.reshape(B, T, N_KV_HEADS, r, HEAD_DIM)
    K = sel_idx.shape[-1]
    qb = _block_size(T, SEL_QROWS // B)
    bi = jnp.arange(B)[:, None, None, None]
    gi = jnp.arange(N_KV_HEADS)[None, :, None, None]
    offs = jnp.arange(SEL_BLOCK)

    def one_block(i):
        s0 = i * qb
        qi = lax.dynamic_slice_in_dim(qg, s0, qb, axis=1)
        ii = lax.dynamic_slice_in_dim(sel_idx, s0, qb, axis=2)
        vi = lax.dynamic_slice_in_dim(sel_valid, s0, qb, axis=2)
        pi = lax.dynamic_slice_in_dim(qpos, s0, qb)
        kg = kb[bi, gi, ii].reshape(B, N_KV_HEADS, qb, K * SEL_BLOCK, HEAD_DIM)
        vg = vb[bi, gi, ii].reshape(B, N_KV_HEADS, qb, K * SEL_BLOCK, HEAD_DIM)
        kpos = ii[..., None] * SEL_BLOCK + offs
        mask = (vi[..., None] & (kpos <= pi[None, None, :, None, None])).reshape(B, N_KV_HEADS, qb, K * SEL_BLOCK)
        s = jnp.einsum('bqgrd,bgqnd->bgrqn', qi, kg, preferred_element_type=jnp.float32) * HEAD_DIM ** -0.5
        p = _masked_softmax(s, mask[:, :, None])
        o = jnp.einsum('bgrqn,bgqnd->bqgrd', p.astype(vg.dtype), vg)
        return o.reshape(B, qb, N_HEADS, HEAD_DIM)

    out = lax.map(one_block, jnp.arange(T // qb))
    return jnp.moveaxis(out, 0, 1).reshape(B, T, N_HEADS, HEAD_DIM)


def _window_attend(q, kw, vw, kpos, qpos):
    B, T = q.shape[:2]
    r = N_HEADS // N_KV_HEADS
    qg = q.reshape(B, T, N_KV_HEADS, r, HEAD_DIM)
    qb = _block_size(T, WIN_QBLK)

    def one_block(i):
        s0 = i * qb
        qi = lax.dynamic_slice_in_dim(qg, s0, qb, axis=1)
        ki = lax.dynamic_slice_in_dim(kw, s0, WINDOW + qb, axis=1)
        vi = lax.dynamic_slice_in_dim(vw, s0, WINDOW + qb, axis=1)
        kp = lax.dynamic_slice_in_dim(kpos, s0, WINDOW + qb)
        qp = lax.dynamic_slice_in_dim(qpos, s0, qb)
        dist = qp[:, None] - kp[None, :]
        mask = (dist >= 0) & (dist <= WINDOW) & (kp >= 0)[None, :]
        s = jnp.einsum('bqgrd,bkgd->bgrqk', qi, ki, preferred_element_type=jnp.float32) * HEAD_DIM ** -0.5
        p = _masked_softmax(s, mask)
        o = jnp.einsum('bgrqk,bkgd->bqgrd', p.astype(vi.dtype), vi)
        return o.reshape(B, qb, N_HEADS, HEAD_DIM)

    out = lax.map(one_block, jnp.arange(T // qb))
    return jnp.moveaxis(out, 0, 1).reshape(B, T, N_HEADS, HEAD_DIM)


def _causal_depthwise_conv(u_all, w, b):
    y = lax.conv_general_dilated(u_all, w[:, None, :], window_strides=(1,), padding='VALID',
                                 dimension_numbers=('NWC', 'WIO', 'NWC'), feature_group_count=CONV_DIM)
    return y + b


def _layer(x, c, kv_past, win_past, conv_past, w_ada, b_ada, norm_pre, norm_post, w_in,
           cmp_pe, cmp_w1, cmp_b1, cmp_w2, cmp_b2, conv_dw, conv_db, conv_ln_g, conv_ln_b, w_out):
    B, T, _ = x.shape
    pos0 = kv_past.shape[1]
    pw = win_past.shape[1]
    shift, scale, gate = jnp.split(jax.nn.silu(c) @ w_ada + b_ada, 3, axis=-1)
    h = _rmsnorm(x, norm_pre) * (1.0 + scale[:, None, :]) + shift[:, None, :]
    proj = h @ w_in
    cuts = np.cumsum([Q_COLS, KV_COLS, WIN_COLS, GATE_COLS, NSA_WIDTH, 2 * CONV_DIM]).tolist()
    q, kv_new, win_new, g_raw, z_nsa, glu_in, z_conv = jnp.split(proj, cuts, axis=-1)
    q = q.reshape(B, T, N_HEADS, HEAD_DIM)
    kv_new = kv_new.reshape(B, T, N_CACHE_SLOTS, N_KV_HEADS, HEAD_DIM)
    win_new = win_new.reshape(B, T, N_WIN_SLOTS, N_KV_HEADS, HEAD_DIM)
    qpos = pos0 + jnp.arange(T)

    kv_full = jnp.concatenate([kv_past, kv_new], axis=1)
    kc = _compress(kv_full[:, :, 0:2], cmp_pe, cmp_w1, cmp_b1, cmp_w2, cmp_b2)
    o_cmp, p_cmp = _cmp_attend(q, kc[:, :, 0], kc[:, :, 1], qpos)
    n_sel = -(-kv_full.shape[1] // SEL_BLOCK)
    sel_idx, sel_valid = _select_blocks(p_cmp, qpos, n_sel)
    o_slc = _select_attend(q, kv_full[:, :, 2], kv_full[:, :, 3], sel_idx, sel_valid, qpos)
    win_all = jnp.concatenate([win_past, win_new], axis=1)
    win_pad = jnp.pad(win_all, ((0, 0), (WINDOW - pw, 0), (0, 0), (0, 0), (0, 0)))
    kpos = pos0 - WINDOW + jnp.arange(WINDOW + T)
    o_win = _window_attend(q, win_pad[:, :, 0], win_pad[:, :, 1], kpos, qpos)
    g = jax.nn.sigmoid(g_raw.reshape(B, T, N_BRANCH, N_HEADS))[..., None]
    o = g[:, :, 0] * o_cmp + g[:, :, 1] * o_slc + g[:, :, 2] * o_win
    o_nsa = o.reshape(B, T, NSA_WIDTH) * jax.nn.silu(z_nsa)

    a, gl = jnp.split(glu_in, 2, axis=-1)
    u = a * jax.nn.sigmoid(gl)
    u_all = jnp.concatenate([conv_past, u], axis=1)
    cv = _causal_depthwise_conv(u_all, conv_dw, conv_db)
    cv = jax.nn.silu(_layernorm(cv, conv_ln_g, conv_ln_b)) * jax.nn.silu(z_conv)

    mix = jnp.concatenate([o_nsa, cv], axis=-1) @ w_out
    y = x + gate[:, None, :] * _rmsnorm(mix, norm_post)
    keep = min(WINDOW, pw + T)
    new_win = win_all[:, pw + T - keep:]
    new_conv = u_all[:, u_all.shape[1] - (CONV_WIDTH - 1):]
    return y, kv_new, new_win, new_conv


def setup_inputs(seed: int = 0) -> dict:
    key = jax.random.key(seed)
    ks = jax.random.split(key, 24)
    f32 = jnp.float32
    n_pages = PAST_LEN // PAGE_SIZE
    n_phys = (5 * DEC_BATCH * n_pages) // 4
    w_len = min(WINDOW, PAST_LEN)

    def nrm(k, shape, s=1.0):
        return jax.random.normal(k, shape, f32) * s

    page_table = jax.random.permutation(ks[5], n_phys)[:DEC_BATCH * n_pages].reshape(DEC_BATCH, n_pages).astype(jnp.int32)
    return {
        'x_prompt': nrm(ks[0], (BATCH, SEQ, D_MODEL)),
        'x_sample': nrm(ks[1], (DEC_BATCH, DEC_SEQ, D_MODEL)),
        'c_prompt': nrm(ks[2], (BATCH, D_MODEL)),
        'c_sample': nrm(ks[3], (DEC_BATCH, D_MODEL)),
        'cache_kv': nrm(ks[4], (DEPTH, n_phys, PAGE_SIZE, N_CACHE_SLOTS, N_KV_HEADS, HEAD_DIM)),
        'page_table': page_table,
        'state_win_kv': nrm(ks[6], (DEPTH, DEC_BATCH, w_len, N_WIN_SLOTS, N_KV_HEADS, HEAD_DIM)),
        'state_conv': nrm(ks[7], (DEPTH, DEC_BATCH, CONV_WIDTH - 1, CONV_DIM), 0.5),
        'w_ada': nrm(ks[8], (DEPTH, D_MODEL, 3 * D_MODEL), 0.5 * D_MODEL ** -0.5),
        'b_ada': nrm(ks[9], (DEPTH, 3 * D_MODEL), 0.01),
        'norm_pre': 1.0 + nrm(ks[10], (DEPTH, D_MODEL), 0.01),
        'norm_post': 1.0 + nrm(ks[11], (DEPTH, D_MODEL), 0.01),
        'w_in': nrm(ks[12], (DEPTH, D_MODEL, IN_COLS), D_MODEL ** -0.5),
        'cmp_pe': nrm(ks[13], (DEPTH, 2, CMP_BLOCK, HEAD_DIM), 0.1),
        'cmp_w1': nrm(ks[14], (DEPTH, 2, CMP_BLOCK, HEAD_DIM, HEAD_DIM), (CMP_BLOCK * HEAD_DIM) ** -0.5),
        'cmp_b1': nrm(ks[15], (DEPTH, 2, HEAD_DIM), 0.01),
        'cmp_w2': nrm(ks[16], (DEPTH, 2, HEAD_DIM, HEAD_DIM), HEAD_DIM ** -0.5),
        'cmp_b2': nrm(ks[17], (DEPTH, 2, HEAD_DIM), 0.01),
        'conv_dw': nrm(ks[18], (DEPTH, CONV_WIDTH, CONV_DIM), CONV_WIDTH ** -0.5),
        'conv_db': nrm(ks[19], (DEPTH, CONV_DIM), 0.01),
        'conv_ln_g': 1.0 + nrm(ks[20], (DEPTH, CONV_DIM), 0.01),
        'conv_ln_b': nrm(ks[21], (DEPTH, CONV_DIM), 0.01),
        'w_out': nrm(ks[22], (DEPTH, MIX_WIDTH, D_MODEL), MIX_WIDTH ** -0.5),
    }


def reference(x_prompt, x_sample, c_prompt, c_sample, cache_kv, page_table, state_win_kv, state_conv,
              w_ada, b_ada, norm_pre, norm_post, w_in, cmp_pe, cmp_w1, cmp_b1, cmp_w2, cmp_b2,
              conv_dw, conv_db, conv_ln_g, conv_ln_b, w_out):
    bp = x_prompt.shape[0]
    bs, n_pages = page_table.shape
    past_len = n_pages * cache_kv.shape[2]
    hp, hs = x_prompt, x_sample
    kv_p, kv_s, win_p, win_s, conv_p, conv_s = [], [], [], [], [], []
    for l in range(DEPTH):
        wl = (w_ada[l], b_ada[l], norm_pre[l], norm_post[l], w_in[l], cmp_pe[l], cmp_w1[l], cmp_b1[l],
              cmp_w2[l], cmp_b2[l], conv_dw[l], conv_db[l], conv_ln_g[l], conv_ln_b[l], w_out[l])
        hp, a1, a2, a3 = _layer(
            hp, c_prompt,
            jnp.zeros((bp, 0, N_CACHE_SLOTS, N_KV_HEADS, HEAD_DIM), hp.dtype),
            jnp.zeros((bp, 0, N_WIN_SLOTS, N_KV_HEADS, HEAD_DIM), hp.dtype),
            jnp.zeros((bp, CONV_WIDTH - 1, CONV_DIM), hp.dtype), *wl)
        kv_past = cache_kv[l][page_table].reshape(bs, past_len, N_CACHE_SLOTS, N_KV_HEADS, HEAD_DIM)
        hs, b1_, b2_, b3_ = _layer(hs, c_sample, kv_past, state_win_kv[l], state_conv[l], *wl)
        kv_p.append(a1); win_p.append(a2); conv_p.append(a3)
        kv_s.append(b1_); win_s.append(b2_); conv_s.append(b3_)
    kv_rows_prompt = jnp.stack(kv_p)
    kv_rows_sample = jnp.stack(kv_s)
    win_prompt = jnp.stack(win_p)
    win_sample = jnp.stack(win_s)
    conv_prompt = jnp.stack(conv_p)
    conv_sample = jnp.stack(conv_s)
    return (hp, hs, kv_rows_prompt, kv_rows_sample, win_prompt, win_sample, conv_prompt, conv_sample)
```

```python
import functools

import numpy as np
import jax
import jax.numpy as jnp
from jax import lax
from jax.experimental import pallas as pl
from jax.experimental.pallas import tpu as pltpu

D_MODEL = 2048
HEAD_DIM = 128
N_HEADS = 8
N_KV_HEADS = 2
HEADS_PER_GROUP = N_HEADS // N_KV_HEADS
NSA_WIDTH = N_HEADS * HEAD_DIM
CONV_DIM = D_MODEL - NSA_WIDTH
CMP_BLOCK = 32
CMP_STRIDE = 16
SEL_BLOCK = 64
N_SELECT = 16
WINDOW = 512
CONV_WIDTH = 31
N_CACHE_SLOTS = 4
N_WIN_SLOTS = 2
N_BRANCH = 3
Q_COLS = N_HEADS * HEAD_DIM
KV_COLS = N_CACHE_SLOTS * N_KV_HEADS * HEAD_DIM
WIN_COLS = N_WIN_SLOTS * N_KV_HEADS * HEAD_DIM
GATE_COLS = N_BRANCH * N_HEADS
_KV_STREAMS = N_CACHE_SLOTS * N_KV_HEADS
NORM_EPS = 1e-6
NEG_INF = -1e30
FORCE_BONUS = 1e6
SM_SCALE = HEAD_DIM ** -0.5

LANES = 128
SUBLANES = 8
CONV_HALO = 32
SAMPLE_ROWS = 8
VMEM_LIMIT = 56 * 1024 * 1024

F32 = jnp.float32
BF16 = jnp.bfloat16


def _sigmoid(x):
    return 1.0 / (1.0 + jnp.exp(-x))


def _silu(x):
    return x * _sigmoid(x)


def _dot(a, b):
    return jnp.dot(a, b, preferred_element_type=F32)


def _dot_nt(a, b):
    return lax.dot_general(a, b, (((1,), (1,)), ((), ())), preferred_element_type=F32)


def _masked_softmax(s, mask):
    s = jnp.where(mask, s, NEG_INF)
    e = jnp.where(mask, jnp.exp(s - jnp.max(s, axis=-1, keepdims=True)), 0.0)
    return e / jnp.maximum(jnp.sum(e, axis=-1, keepdims=True), 1e-30)


def _params(*sem):
    return pltpu.CompilerParams(dimension_semantics=sem, vmem_limit_bytes=VMEM_LIMIT)


def _ada_kernel(c_ref, w_ref, b_ref, o_ref):
    o_ref[...] = _dot(_silu(c_ref[...]).astype(BF16), w_ref[...].astype(BF16)) + b_ref[...]


def _ada(c_all, w_ada, b_ada, tn=512):
    rows, d = c_all.shape
    n = w_ada.shape[1]
    return pl.pallas_call(
        _ada_kernel,
        grid=(n // tn,),
        in_specs=[pl.BlockSpec((rows, d), lambda j: (0, 0)),
                  pl.BlockSpec((d, tn), lambda j: (0, j)),
                  pl.BlockSpec((1, tn), lambda j: (0, j))],
        out_specs=pl.BlockSpec((rows, tn), lambda j: (0, j)),
        out_shape=jax.ShapeDtypeStruct((rows, n), F32),
        compiler_params=_params("parallel"),
        name="ada",
    )(c_all, w_ada, b_ada.reshape(1, n))


def _cmpconst_kernel(pe_ref, w1_ref, b1_ref, o_ref):
    y = _dot(pe_ref[...].astype(BF16), w1_ref[...])
    o_ref[...] = y[0:1, :HEAD_DIM] + y[1:2, HEAD_DIM:] + b1_ref[...]


def _cmpconst(pe_rows, w1cat, b1):
    return pl.pallas_call(
        _cmpconst_kernel,
        grid=(2,),
        in_specs=[pl.BlockSpec((None, SUBLANES, 16 * HEAD_DIM), lambda s: (s, 0, 0)),
                  pl.BlockSpec((None, 16 * HEAD_DIM, 2 * HEAD_DIM), lambda s: (s, 0, 0)),
                  pl.BlockSpec((None, 1, HEAD_DIM), lambda s: (s, 0, 0))],
        out_specs=pl.BlockSpec((None, 1, HEAD_DIM), lambda s: (s, 0, 0)),
        out_shape=jax.ShapeDtypeStruct((2, 1, HEAD_DIM), F32),
        compiler_params=_params("parallel"),
        name="cmpconst",
    )(pe_rows, w1cat, b1)


_TN = 512
_J_KV = Q_COLS // _TN
_J_REST = (Q_COLS + KV_COLS) // _TN
_REST_COLS = NSA_WIDTH + 3 * CONV_DIM + WIN_COLS
_N_MAIN = Q_COLS + KV_COLS + _REST_COLS
_J_WIN = (_N_MAIN - WIN_COLS) // _TN
_NJ = _N_MAIN // _TN
_REST_Z = 0
_REST_A = 1
_REST_GL = 2
_REST_ZC = 3
_REST_WIN = (NSA_WIDTH + 3 * CONV_DIM) // WIN_COLS


def _inproj_kernel(x_ref, g_ref, sc_ref, sh_ref, w_ref, wg_ref,
                   q_ref, kv_ref, kvb_ref, rest_ref, winb_ref, gate_ref, h_ref):
    j = pl.program_id(1)

    @pl.when(j == 0)
    def _():
        x = x_ref[...]
        y = x * lax.rsqrt(jnp.mean(x * x, axis=-1, keepdims=True) + NORM_EPS) * g_ref[...]
        h = (y * (1.0 + sc_ref[...]) + sh_ref[...]).astype(BF16)
        h_ref[...] = h
        gate_ref[...] = _dot(h, wg_ref[...])

    acc = _dot(h_ref[...], w_ref[...])

    @pl.when(j < _J_KV)
    def _():
        q_ref[...] = acc.astype(BF16)

    @pl.when((j >= _J_KV) & (j < _J_REST))
    def _():
        kv_ref[...] = acc
        kvb_ref[...] = acc.astype(BF16)

    @pl.when(j >= _J_REST)
    def _():
        rest_ref[...] = acc

    @pl.when(j == _J_WIN)
    def _():
        winb_ref[...] = acc.astype(BF16)


def _inproj(x, g, scale3, shift3, w_main, w_gate, tm, rows_per_mod):
    m, d = x.shape
    mod_rows = scale3.shape[1]
    tiles_per_mod = rows_per_mod // tm

    def mod_map(i, j):
        return (i // tiles_per_mod, 0, 0)

    clampj = lambda j, lo, n: jnp.clip(j - lo, 0, n - 1)
    return pl.pallas_call(
        _inproj_kernel,
        grid=(m // tm, _NJ),
        in_specs=[pl.BlockSpec((tm, d), lambda i, j: (i, 0)),
                  pl.BlockSpec((1, d), lambda i, j: (0, 0)),
                  pl.BlockSpec((None, mod_rows, d), mod_map),
                  pl.BlockSpec((None, mod_rows, d), mod_map),
                  pl.BlockSpec((d, _TN), lambda i, j: (0, j)),
                  pl.BlockSpec((d, 2 * LANES), lambda i, j: (0, 0))],
        out_specs=[pl.BlockSpec((tm, _TN), lambda i, j: (i, clampj(j, 0, _J_KV))),
                   pl.BlockSpec((tm, _TN), lambda i, j: (i, clampj(j, _J_KV, _J_REST - _J_KV))),
                   pl.BlockSpec((tm, _TN), lambda i, j: (i, clampj(j, _J_KV, _J_REST - _J_KV))),
                   pl.BlockSpec((tm, _TN), lambda i, j: (i, clampj(j, _J_REST, _NJ - _J_REST))),
                   pl.BlockSpec((tm, _TN), lambda i, j: (i, 0)),
                   pl.BlockSpec((tm, 2 * LANES), lambda i, j: (i, 0))],
        out_shape=[jax.ShapeDtypeStruct((m, Q_COLS), BF16),
                   jax.ShapeDtypeStruct((m, KV_COLS), F32),
                   jax.ShapeDtypeStruct((m, KV_COLS), BF16),
                   jax.ShapeDtypeStruct((m, _REST_COLS), F32),
                   jax.ShapeDtypeStruct((m, WIN_COLS), BF16),
                   jax.ShapeDtypeStruct((m, 2 * LANES), F32)],
        scratch_shapes=[pltpu.VMEM((tm, d), BF16)],
        compiler_params=_params("parallel", "arbitrary"),
        name="inproj",
    )(x, g, scale3, shift3, w_main, w_gate)


def _compress_rows(load_rows, nch, w1, cconst, w2, b2):
    lhs = jnp.concatenate([load_rows(l) for l in range(CMP_STRIDE)], axis=1).astype(BF16)
    y = _dot(lhs, w1)
    second = pltpu.roll(y[:, HEAD_DIM:], shift=nch - 1, axis=0)
    h = _silu(y[:, :HEAD_DIM] + second + cconst)
    out = _dot(h.astype(BF16), w2) + b2
    row = lax.broadcasted_iota(jnp.int32, out.shape, 0)
    return jnp.where(row < nch - 1, out, 0.0)


def _compress_kernel(x_ref, w1_ref, cc_ref, w2_ref, b2_ref, o_ref, *, nch, ncp):
    for c in range(2 * N_KV_HEADS):
        s = c // N_KV_HEADS

        def load_rows(l, c=c):
            return x_ref[pl.ds(l * _KV_STREAMS + c, nch, stride=CMP_STRIDE * _KV_STREAMS), :]

        out = _compress_rows(load_rows, nch, w1_ref[s], cc_ref[s], w2_ref[s], b2_ref[s])
        if ncp > nch:
            out = jnp.concatenate([out, jnp.zeros((ncp - nch, HEAD_DIM), F32)], axis=0)
        o_ref[c] = out.astype(BF16)


def _compress(kv_streams, w1cat, cconst, w2b, b2, ncp):
    b, rows, _ = kv_streams.shape
    t = rows // _KV_STREAMS
    nch = t // CMP_STRIDE
    return pl.pallas_call(
        functools.partial(_compress_kernel, nch=nch, ncp=ncp),
        grid=(b,),
        in_specs=[pl.BlockSpec((None, rows, HEAD_DIM), lambda i: (i, 0, 0)),
                  pl.BlockSpec((2, 16 * HEAD_DIM, 2 * HEAD_DIM), lambda i: (0, 0, 0)),
                  pl.BlockSpec((2, 1, HEAD_DIM), lambda i: (0, 0, 0)),
                  pl.BlockSpec((2, HEAD_DIM, HEAD_DIM), lambda i: (0, 0, 0)),
                  pl.BlockSpec((2, 1, HEAD_DIM), lambda i: (0, 0, 0))],
        out_specs=pl.BlockSpec((None, 4, ncp, HEAD_DIM), lambda i: (i, 0, 0, 0)),
        out_shape=jax.ShapeDtypeStruct((b, 4, ncp, HEAD_DIM), BF16),
        compiler_params=_params("parallel"),
        name="compress",
    )(kv_streams, w1cat, cconst, w2b, b2)


def _cmp_branch(q_heads, kc, vc, qpos, n_cmp):
    rows = q_heads[0].shape[0]
    nio = lax.broadcasted_iota(jnp.int32, (rows, kc.shape[0]), 1)
    mask = (nio < n_cmp) & (nio * CMP_STRIDE + (CMP_BLOCK - 1) <= qpos)
    outs, probs = [], []
    for q in q_heads:
        p = _masked_softmax(_dot_nt(q, kc) * SM_SCALE, mask)
        outs.append(_dot(p.astype(BF16), vc))
        probs.append(p)
    return outs, probs


def _select_mask(psum, ov, qpos, n_sel):
    hi = psum.astype(BF16)
    lo = (psum - hi.astype(F32)).astype(BF16)
    imp = _dot(hi, ov) + _dot(lo, ov)
    rows = imp.shape[0]
    jio = lax.broadcasted_iota(jnp.int32, (rows, LANES), 1)
    allowed = (jio * SEL_BLOCK <= qpos) & (jio < n_sel)
    cur = qpos // SEL_BLOCK
    forced = (jio == 0) | (jio == cur) | (jio == cur - 1)
    score = jnp.where(allowed, imp + jnp.where(forced, FORCE_BONUS, 0.0), NEG_INF)
    rank = jnp.zeros((rows, LANES), F32)
    for i in range(n_sel):
        si = jnp.broadcast_to(score[:, i:i + 1], (rows, LANES))
        tie = jnp.where(jio > i, 1.0, 0.0)
        rank = rank + jnp.where(si > score, 1.0, jnp.where(si == score, tie, 0.0))
    return (rank < float(min(N_SELECT, n_sel))) & allowed


def _attn_p_kernel(q_ref, kc_ref, vc_ref, ks_ref, vs_ref, kw_ref, vw_ref, gate_ref, z_ref, ov_ref, e_ref,
                   o_ref, m_s, l_s, acc_s, *, tq, tk, n_cmp, n_sel):
    qi = pl.program_id(2)
    q0 = qi * tq
    qpos = q0 + lax.broadcasted_iota(jnp.int32, (tq, 1), 0)
    q_heads = [q_ref[:, r * HEAD_DIM:(r + 1) * HEAD_DIM] for r in range(HEADS_PER_GROUP)]

    o_cmp, probs = _cmp_branch(q_heads, kc_ref[...], vc_ref[...], qpos, n_cmp)
    psum = probs[0] + probs[1] + probs[2] + probs[3]
    sel = _select_mask(psum, ov_ref[...], qpos, n_sel)
    selb = jnp.where(sel, 1.0, 0.0).astype(BF16)

    m_s[...] = jnp.full(m_s.shape, NEG_INF, F32)
    l_s[...] = jnp.zeros(l_s.shape, F32)
    acc_s[...] = jnp.zeros(acc_s.shape, F32)
    n_kt = (q0 + tq - 1) // tk + 1

    def sweep(kt, carry):
        k0 = pl.multiple_of(kt * tk, tk)
        k_t = ks_ref[pl.ds(k0, tk), :]
        v_t = vs_ref[pl.ds(k0, tk), :]
        kpos = k0 + lax.broadcasted_iota(jnp.int32, (tq, tk), 1)
        mask = (_dot(selb, e_ref[kt]) > 0.5) & (kpos <= qpos)
        for r in range(HEADS_PER_GROUP):
            s = jnp.where(mask, _dot_nt(q_heads[r], k_t) * SM_SCALE, NEG_INF)
            m_prev = m_s[r]
            m_new = jnp.maximum(m_prev, jnp.max(s, axis=-1, keepdims=True))
            p = jnp.where(mask, jnp.exp(s - jnp.concatenate([m_new] * (tk // LANES), axis=1)), 0.0)
            alpha = jnp.exp(m_prev - m_new)
            l_s[r] = alpha * l_s[r] + jnp.sum(p, axis=-1, keepdims=True)
            acc_s[r] = alpha * acc_s[r] + _dot(p.astype(BF16), v_t)
            m_s[r] = m_new
        return carry

    lax.fori_loop(0, n_kt, sweep, 0)

    w0 = pl.multiple_of(jnp.maximum(q0 - WINDOW, 0), LANES)
    k_w = kw_ref[pl.ds(w0, WINDOW + tq), :]
    v_w = vw_ref[pl.ds(w0, WINDOW + tq), :]
    dist = qpos - (w0 + lax.broadcasted_iota(jnp.int32, (tq, WINDOW + tq), 1))
    wmask = (dist >= 0) & (dist <= WINDOW)

    gates = _sigmoid(gate_ref[...])
    for r in range(HEADS_PER_GROUP):
        o_slc = acc_s[r] / jnp.maximum(l_s[r], 1e-30)
        p = _masked_softmax(_dot_nt(q_heads[r], k_w) * SM_SCALE, wmask)
        o_win = _dot(p.astype(BF16), v_w)
        g_of = lambda br: jnp.broadcast_to(
            gates[:, br * HEADS_PER_GROUP + r:br * HEADS_PER_GROUP + r + 1], (tq, HEAD_DIM))
        o = g_of(0) * o_cmp[r] + g_of(1) * o_slc + g_of(2) * o_win
        o_ref[:, r * HEAD_DIM:(r + 1) * HEAD_DIM] = (
            o * _silu(z_ref[:, r * HEAD_DIM:(r + 1) * HEAD_DIM])).astype(BF16)


def _attn_p(q_bf, kcv, kv_bf3, win_bf3, gate, rest, ov, e_p, b, t, n_cmp, n_sel, tq=128, tk=256):
    assert t >= WINDOW + tq and t % tk == 0
    nq = t // tq
    ncp = kcv.shape[2]
    gw = HEADS_PER_GROUP * HEAD_DIM
    row = lambda bi, g, qi: bi * nq + qi
    return pl.pallas_call(
        functools.partial(_attn_p_kernel, tq=tq, tk=tk, n_cmp=n_cmp, n_sel=n_sel),
        grid=(b, N_KV_HEADS, nq),
        in_specs=[pl.BlockSpec((tq, gw), lambda bi, g, qi: (row(bi, g, qi), g)),
                  pl.BlockSpec((None, None, ncp, HEAD_DIM), lambda bi, g, qi: (bi, g, 0, 0)),
                  pl.BlockSpec((None, None, ncp, HEAD_DIM), lambda bi, g, qi: (bi, N_KV_HEADS + g, 0, 0)),
                  pl.BlockSpec((None, t, HEAD_DIM), lambda bi, g, qi: (bi, 0, 2 * N_KV_HEADS + g)),
                  pl.BlockSpec((None, t, HEAD_DIM), lambda bi, g, qi: (bi, 0, 3 * N_KV_HEADS + g)),
                  pl.BlockSpec((None, t, HEAD_DIM), lambda bi, g, qi: (bi, 0, g)),
                  pl.BlockSpec((None, t, HEAD_DIM), lambda bi, g, qi: (bi, 0, N_KV_HEADS + g)),
                  pl.BlockSpec((tq, LANES), lambda bi, g, qi: (row(bi, g, qi), g)),
                  pl.BlockSpec((tq, gw), lambda bi, g, qi: (row(bi, g, qi), g)),
                  pl.BlockSpec((ncp, LANES), lambda bi, g, qi: (0, 0)),
                  pl.BlockSpec((t // tk, LANES, tk), lambda bi, g, qi: (0, 0, 0))],
        out_specs=pl.BlockSpec((tq, gw), lambda bi, g, qi: (row(bi, g, qi), g)),
        out_shape=jax.ShapeDtypeStruct((b * t, NSA_WIDTH), BF16),
        scratch_shapes=[pltpu.VMEM((HEADS_PER_GROUP, tq, LANES), F32),
                        pltpu.VMEM((HEADS_PER_GROUP, tq, LANES), F32),
                        pltpu.VMEM((HEADS_PER_GROUP, tq, HEAD_DIM), F32)],
        compiler_params=_params("parallel", "parallel", "arbitrary"),
        name="attn_p",
    )(q_bf, kcv, kcv, kv_bf3, kv_bf3, win_bf3, win_bf3, gate, rest, ov, e_p)


def _attn_s_kernel(pt_ref, cache_ref, q_ref, kvn_ref, wst_ref, winn_ref, gate_ref, z_ref,
                   w1_ref, cc_ref, w2_ref, b2_ref, ov_ref, e_ref, o_ref, kvbuf, sem,
                   *, n_pages, page, n_cmp, n_sel, pos0):
    b = pl.program_id(0)
    nb = pl.num_programs(0)
    past = n_pages * page
    nch = past // CMP_STRIDE
    rows = HEADS_PER_GROUP * SAMPLE_ROWS

    def page_copy(seq, p, slot):
        return pltpu.make_async_copy(cache_ref.at[pt_ref[seq, p]],
                                     kvbuf.at[slot, pl.ds(p * page * _KV_STREAMS, page * _KV_STREAMS)],
                                     sem.at[slot])

    def fetch(seq, slot):
        for p in range(n_pages):
            page_copy(seq, p, slot).start()

    @pl.when(b == 0)
    def _():
        fetch(0, 0)

    @pl.when(b + 1 < nb)
    def _():
        fetch(b + 1, (b + 1) % 2)

    slot = b % 2
    for p in range(n_pages):
        page_copy(b, p, slot).wait()
    xbuf = kvbuf.at[slot]

    qpos = pos0 + lax.broadcasted_iota(jnp.int32, (rows, 1), 0) % SAMPLE_ROWS
    qf = q_ref[...].astype(F32)
    gates = _sigmoid(gate_ref[...])
    zero_tail = jnp.zeros((LANES - 2 * SUBLANES, HEAD_DIM), BF16)
    pad8 = jnp.zeros((SUBLANES, HEAD_DIM), F32)

    def with_new_rows(old_bf, new_f32):
        new16 = jnp.concatenate([new_f32, pad8], axis=0).astype(BF16)
        return jnp.concatenate([old_bf, new16, zero_tail], axis=0)

    def stack_heads(x8):
        return jnp.concatenate([x8] * HEADS_PER_GROUP, axis=0)

    for g in range(N_KV_HEADS):
        heads = [qf[:, (g * HEADS_PER_GROUP + r) * HEAD_DIM:(g * HEADS_PER_GROUP + r + 1) * HEAD_DIM]
                 for r in range(HEADS_PER_GROUP)]
        qg = jnp.concatenate(heads, axis=0).astype(BF16)

        def cmp_stream(c):
            s = c // N_KV_HEADS
            load = lambda l: xbuf[pl.ds(l * _KV_STREAMS + c, nch, stride=CMP_STRIDE * _KV_STREAMS), :]
            return _compress_rows(load, nch, w1_ref[s], cc_ref[s], w2_ref[s], b2_ref[s]).astype(BF16)

        kc = cmp_stream(g)
        vc = cmp_stream(N_KV_HEADS + g)
        (o_cmp,), (p_cmp,) = _cmp_branch([qg], kc, vc, qpos, n_cmp)
        psum = (p_cmp[0:8] + p_cmp[8:16]) + (p_cmp[16:24] + p_cmp[24:32])
        sel = _select_mask(psum, ov_ref[...], qpos[0:SAMPLE_ROWS], n_sel)
        selb = jnp.where(sel, 1.0, 0.0).astype(BF16)

        c_k = (2 * N_KV_HEADS + g) * HEAD_DIM
        c_v = (3 * N_KV_HEADS + g) * HEAD_DIM
        stream = lambda c: xbuf[pl.ds(c, past, stride=_KV_STREAMS), :].astype(BF16)
        k_all = with_new_rows(stream(2 * N_KV_HEADS + g), kvn_ref[:, c_k:c_k + HEAD_DIM])
        v_all = with_new_rows(stream(3 * N_KV_HEADS + g), kvn_ref[:, c_v:c_v + HEAD_DIM])
        nk = past + LANES
        kpos = lax.broadcasted_iota(jnp.int32, (rows, nk), 1)
        smask = (stack_heads(_dot(selb, e_ref[...])) > 0.5) & (kpos <= qpos)
        p = _masked_softmax(_dot_nt(qg, k_all) * SM_SCALE, smask)
        o_slc = _dot(p.astype(BF16), v_all)

        wk = g * HEAD_DIM
        wv = (N_KV_HEADS + g) * HEAD_DIM
        kw_all = with_new_rows(wst_ref[:, wk:wk + HEAD_DIM].astype(BF16), winn_ref[:, wk:wk + HEAD_DIM])
        vw_all = with_new_rows(wst_ref[:, wv:wv + HEAD_DIM].astype(BF16), winn_ref[:, wv:wv + HEAD_DIM])
        wpos = (pos0 - WINDOW) + lax.broadcasted_iota(jnp.int32, (rows, WINDOW + LANES), 1)
        dist = qpos - wpos
        wmask = (dist >= 0) & (dist <= WINDOW) & (wpos >= 0)
        p = _masked_softmax(_dot_nt(qg, kw_all) * SM_SCALE, wmask)
        o_win = _dot(p.astype(BF16), vw_all)

        for r in range(HEADS_PER_GROUP):
            h = g * HEADS_PER_GROUP + r
            rs = slice(r * SAMPLE_ROWS, (r + 1) * SAMPLE_ROWS)
            g_of = lambda br: jnp.broadcast_to(
                gates[:, g * LANES + br * HEADS_PER_GROUP + r:g * LANES + br * HEADS_PER_GROUP + r + 1],
                (SAMPLE_ROWS, HEAD_DIM))
            o = g_of(0) * o_cmp[rs] + g_of(1) * o_slc[rs] + g_of(2) * o_win[rs]
            o_ref[:, h * HEAD_DIM:(h + 1) * HEAD_DIM] = (
                o * _silu(z_ref[:, h * HEAD_DIM:(h + 1) * HEAD_DIM])).astype(BF16)


def _attn_s(page_table, cache3, q3, kvn3, wst3, rest3, gate3, w1cat, cconst, w2b, b2, ov, e_s,
            n_cmp, n_sel, pos0):
    nb, n_pages = page_table.shape
    page = cache3.shape[1] // _KV_STREAMS
    past = n_pages * page
    assert wst3.shape[1] == WINDOW
    ncp = ov.shape[0]
    grid_spec = pltpu.PrefetchScalarGridSpec(
        num_scalar_prefetch=1,
        grid=(nb,),
        in_specs=[pl.BlockSpec(memory_space=pl.ANY),
                  pl.BlockSpec((None, SAMPLE_ROWS, Q_COLS), lambda i, pt: (i, 0, 0)),
                  pl.BlockSpec((None, SAMPLE_ROWS, KV_COLS), lambda i, pt: (i, 0, 0)),
                  pl.BlockSpec((None, WINDOW, WIN_COLS), lambda i, pt: (i, 0, 0)),
                  pl.BlockSpec((None, SAMPLE_ROWS, WIN_COLS), lambda i, pt: (i, 0, _REST_WIN)),
                  pl.BlockSpec((None, SAMPLE_ROWS, 2 * LANES), lambda i, pt: (i, 0, 0)),
                  pl.BlockSpec((None, SAMPLE_ROWS, NSA_WIDTH), lambda i, pt: (i, 0, _REST_Z)),
                  pl.BlockSpec((2, 16 * HEAD_DIM, 2 * HEAD_DIM), lambda i, pt: (0, 0, 0)),
                  pl.BlockSpec((2, 1, HEAD_DIM), lambda i, pt: (0, 0, 0)),
                  pl.BlockSpec((2, HEAD_DIM, HEAD_DIM), lambda i, pt: (0, 0, 0)),
                  pl.BlockSpec((2, 1, HEAD_DIM), lambda i, pt: (0, 0, 0)),
                  pl.BlockSpec((ncp, LANES), lambda i, pt: (0, 0)),
                  pl.BlockSpec((LANES, past + LANES), lambda i, pt: (0, 0))],
        out_specs=pl.BlockSpec((None, SAMPLE_ROWS, NSA_WIDTH), lambda i, pt: (i, 0, 0)),
        scratch_shapes=[pltpu.VMEM((2, past * _KV_STREAMS, HEAD_DIM), F32),
                        pltpu.SemaphoreType.DMA((2,))],
    )
    return pl.pallas_call(
        functools.partial(_attn_s_kernel, n_pages=n_pages, page=page, n_cmp=n_cmp, n_sel=n_sel, pos0=pos0),
        grid_spec=grid_spec,
        out_shape=jax.ShapeDtypeStruct((nb, SAMPLE_ROWS, NSA_WIDTH), BF16),
        compiler_params=_params("arbitrary"),
        name="attn_s",
    )(page_table, cache3, q3, kvn3, wst3, rest3, gate3, rest3, w1cat, cconst, w2b, b2, ov, e_s)


_CONV_RC = 64
_CONV_LC = 256


def _conv_kernel(a_ref, gl_ref, ah_ref, glh_ref, init_ref, zc_ref, w_ref, db_ref, lg_ref, lb_ref,
                 cv_ref, st_ref, uext, ybuf, *, tt, t_valid):
    ti = pl.program_id(1)
    nt = pl.num_programs(1)
    halo = ah_ref[...] * _sigmoid(glh_ref[...])
    uext[0:CONV_HALO] = jnp.where(ti == 0, init_ref[...], halo)
    uext[CONV_HALO:CONV_HALO + tt] = a_ref[...] * _sigmoid(gl_ref[...])

    off = CONV_HALO - (CONV_WIDTH - 1)
    rc = min(_CONV_RC, tt)
    for r0 in range(0, tt, rc):
        for c0 in range(0, CONV_DIM, _CONV_LC):
            acc = jnp.broadcast_to(db_ref[:, c0:c0 + _CONV_LC], (rc, _CONV_LC))
            for k in range(CONV_WIDTH):
                acc = acc + w_ref[k:k + 1, c0:c0 + _CONV_LC] * uext[pl.ds(r0 + off + k, rc), c0:c0 + _CONV_LC]
            ybuf[r0:r0 + rc, c0:c0 + _CONV_LC] = acc

    y = ybuf[...]
    mu = jnp.mean(y, axis=-1, keepdims=True)
    var = jnp.mean(jnp.square(y - mu), axis=-1, keepdims=True)
    ln = (y - mu) * lax.rsqrt(var + NORM_EPS) * lg_ref[...] + lb_ref[...]
    cv_ref[...] = (_silu(ln) * _silu(zc_ref[...])).astype(BF16)

    @pl.when(ti == nt - 1)
    def _():
        st_ref[...] = uext[pl.ds(t_valid, CONV_HALO), :]


def _conv(rest, u_init, w_pad, db, lg, lb, b, t, tt, t_valid):
    nt = t // tt
    hb = tt // CONV_HALO if tt >= CONV_HALO else None
    row = lambda bi, ti: bi * nt + ti
    if hb is None:
        assert nt == 1
        halo_map = lambda col: (lambda bi, ti: (0, col))
        halo_rows = tt
    else:
        halo_map = lambda col: (lambda bi, ti: (jnp.maximum(row(bi, ti) * hb - 1, 0), col))
        halo_rows = CONV_HALO
    vec = lambda: pl.BlockSpec((1, CONV_DIM), lambda bi, ti: (0, 0))
    kern = functools.partial(_conv_kernel, tt=tt, t_valid=t_valid)
    if hb is None:
        kern = functools.partial(_conv_kernel_nohalo, tt=tt, t_valid=t_valid)
    return pl.pallas_call(
        kern,
        grid=(b, nt),
        in_specs=[pl.BlockSpec((tt, CONV_DIM), lambda bi, ti: (row(bi, ti), _REST_A)),
                  pl.BlockSpec((tt, CONV_DIM), lambda bi, ti: (row(bi, ti), _REST_GL)),
                  pl.BlockSpec((halo_rows, CONV_DIM), halo_map(_REST_A)),
                  pl.BlockSpec((halo_rows, CONV_DIM), halo_map(_REST_GL)),
                  pl.BlockSpec((None, CONV_HALO, CONV_DIM), lambda bi, ti: (bi, 0, 0)),
                  pl.BlockSpec((tt, CONV_DIM), lambda bi, ti: (row(bi, ti), _REST_ZC)),
                  pl.BlockSpec((CONV_HALO, CONV_DIM), lambda bi, ti: (0, 0)),
                  vec(), vec(), vec()],
        out_specs=[pl.BlockSpec((tt, CONV_DIM), lambda bi, ti: (row(bi, ti), 0)),
                   pl.BlockSpec((None, CONV_HALO, CONV_DIM), lambda bi, ti: (bi, 0, 0))],
        out_shape=[jax.ShapeDtypeStruct((b * t, CONV_DIM), BF16),
                   jax.ShapeDtypeStruct((b, CONV_HALO, CONV_DIM), F32)],
        scratch_shapes=[pltpu.VMEM((CONV_HALO + tt, CONV_DIM), F32),
                        pltpu.VMEM((tt, CONV_DIM), F32)],
        compiler_params=_params("parallel", "arbitrary"),
        name="conv",
    )(rest, rest, rest, rest, u_init, rest, w_pad, db, lg, lb)


def _conv_kernel_nohalo(a_ref, gl_ref, ah_ref, glh_ref, init_ref, zc_ref, w_ref, db_ref, lg_ref, lb_ref,
                        cv_ref, st_ref, uext, ybuf, *, tt, t_valid):
    del ah_ref, glh_ref
    uext[0:CONV_HALO] = init_ref[...]
    uext[CONV_HALO:CONV_HALO + tt] = a_ref[...] * _sigmoid(gl_ref[...])
    off = CONV_HALO - (CONV_WIDTH - 1)
    for c0 in range(0, CONV_DIM, _CONV_LC):
        acc = jnp.broadcast_to(db_ref[:, c0:c0 + _CONV_LC], (tt, _CONV_LC))
        for k in range(CONV_WIDTH):
            acc = acc + w_ref[k:k + 1, c0:c0 + _CONV_LC] * uext[pl.ds(off + k, tt), c0:c0 + _CONV_LC]
        ybuf[:, c0:c0 + _CONV_LC] = acc
    y = ybuf[...]
    mu = jnp.mean(y, axis=-1, keepdims=True)
    var = jnp.mean(jnp.square(y - mu), axis=-1, keepdims=True)
    ln = (y - mu) * lax.rsqrt(var + NORM_EPS) * lg_ref[...] + lb_ref[...]
    cv_ref[...] = (_silu(ln) * _silu(zc_ref[...])).astype(BF16)
    st_ref[...] = uext[pl.ds(t_valid, CONV_HALO), :]


def _outproj_kernel(on_ref, cv_ref, w_ref, x_ref, gate_ref, g_ref, y_ref):
    mix = _dot(on_ref[...], w_ref[0:NSA_WIDTH, :]) + _dot(cv_ref[...], w_ref[NSA_WIDTH:, :])
    nrm = mix * lax.rsqrt(jnp.mean(mix * mix, axis=-1, keepdims=True) + NORM_EPS) * g_ref[...]
    y_ref[...] = x_ref[...] + gate_ref[...] * nrm


def _outproj(o_nsa, cv, w_out_b, x, gate3, g_post, tm, rows_per_mod):
    m, d = x.shape
    mod_rows = gate3.shape[1]
    tiles_per_mod = rows_per_mod // tm
    return pl.pallas_call(
        _outproj_kernel,
        grid=(m // tm,),
        in_specs=[pl.BlockSpec((tm, NSA_WIDTH), lambda i: (i, 0)),
                  pl.BlockSpec((tm, CONV_DIM), lambda i: (i, 0)),
                  pl.BlockSpec((d, d), lambda i: (0, 0)),
                  pl.BlockSpec((tm, d), lambda i: (i, 0)),
                  pl.BlockSpec((None, mod_rows, d), lambda i: (i // tiles_per_mod, 0, 0)),
                  pl.BlockSpec((1, d), lambda i: (0, 0))],
        out_specs=pl.BlockSpec((tm, d), lambda i: (i, 0)),
        out_shape=jax.ShapeDtypeStruct((m, d), F32),
        compiler_params=_params("parallel"),
        name="outproj",
    )(o_nsa, cv, w_out_b, x, gate3, g_post)


def _overlap_matrix(n_cmp, n_sel, ncp):
    ci = np.arange(ncp)[:, None] * CMP_STRIDE
    sj = np.arange(LANES)[None, :] * SEL_BLOCK
    ov = (ci < sj + SEL_BLOCK) & (ci + CMP_BLOCK > sj)
    ov &= (np.arange(ncp)[:, None] < n_cmp) & (np.arange(LANES)[None, :] < n_sel)
    return jnp.asarray(ov.astype(np.float32), dtype=BF16)


def _expand_matrix(n_keys):
    e = (np.arange(n_keys)[None, :] // SEL_BLOCK) == np.arange(LANES)[:, None]
    return jnp.asarray(e.astype(np.float32), dtype=BF16)


def _round_up(x, m):
    return -(-x // m) * m


def kernel(x_prompt, x_sample, c_prompt, c_sample, cache_kv, page_table, state_win_kv, state_conv, w_ada, b_ada, norm_pre, norm_post, w_in, cmp_pe, cmp_w1, cmp_b1, cmp_w2, cmp_b2, conv_dw, conv_db, conv_ln_g, conv_ln_b, w_out):
    assert w_ada.shape[0] == 1, "single-layer trunk"
    bp, tp, d = x_prompt.shape
    bs, ts, _ = x_sample.shape
    n_phys, page = cache_kv.shape[1], cache_kv.shape[2]
    n_pages = page_table.shape[1]
    past = n_pages * page
    assert ts <= SAMPLE_ROWS and d == D_MODEL

    w = w_in[0]
    c_q, c_kv, c_win, c_gate = 0, Q_COLS, Q_COLS + KV_COLS, Q_COLS + KV_COLS + WIN_COLS
    c_rest = c_gate + GATE_COLS
    w_main = jnp.concatenate([w[:, c_q:c_win], w[:, c_rest:], w[:, c_win:c_gate]], axis=1).astype(BF16)
    wg = w[:, c_gate:c_rest].reshape(d, N_BRANCH, N_KV_HEADS, HEADS_PER_GROUP).transpose(0, 2, 1, 3)
    wg = wg.reshape(d, N_KV_HEADS, N_BRANCH * HEADS_PER_GROUP)
    w_gate = jnp.pad(wg, ((0, 0), (0, 0), (0, LANES - N_BRANCH * HEADS_PER_GROUP))).reshape(d, 2 * LANES).astype(BF16)
    w_out_b = w_out[0].astype(BF16)
    w1 = cmp_w1[0]
    w1cat = jnp.concatenate([w1[:, :CMP_STRIDE].reshape(2, CMP_STRIDE * HEAD_DIM, HEAD_DIM),
                             w1[:, CMP_STRIDE:].reshape(2, CMP_STRIDE * HEAD_DIM, HEAD_DIM)], axis=2).astype(BF16)
    w2b = cmp_w2[0].astype(BF16)
    pe_rows = jnp.pad(cmp_pe[0].reshape(2, 2, CMP_STRIDE * HEAD_DIM), ((0, 0), (0, SUBLANES - 2), (0, 0)))
    b1 = cmp_b1[0].reshape(2, 1, HEAD_DIM)
    b2 = cmp_b2[0].reshape(2, 1, HEAD_DIM)
    conv_w_pad = jnp.pad(conv_dw[0], ((0, CONV_HALO - CONV_WIDTH), (0, 0)))
    db = conv_db[0].reshape(1, CONV_DIM)
    lg = conv_ln_g[0].reshape(1, CONV_DIM)
    lb = conv_ln_b[0].reshape(1, CONV_DIM)
    g_pre = norm_pre[0].reshape(1, d)
    g_post = norm_post[0].reshape(1, d)

    n_c = bp + bs
    c_all = jnp.pad(jnp.concatenate([c_prompt, c_sample], axis=0), ((0, _round_up(n_c, SUBLANES) - n_c), (0, 0)))
    mod = _ada(c_all, w_ada[0], b_ada[0])
    shift, scale, gate = mod[:, :d], mod[:, d:2 * d], mod[:, 2 * d:]
    cconst = _cmpconst(pe_rows, w1cat, b1)

    tm_p = 512
    xp = x_prompt.reshape(bp * tp, d)
    q_p, kv_p, kvb_p, rest_p, winb_p, gate_p = _inproj(
        xp, g_pre, scale[:bp].reshape(bp, 1, d), shift[:bp].reshape(bp, 1, d), w_main, w_gate, tm_p, tp)
    n_cmp_p = (tp - CMP_BLOCK) // CMP_STRIDE + 1
    n_sel_p = -(-tp // SEL_BLOCK)
    ncp_p = _round_up(tp // CMP_STRIDE, LANES)
    kcv_p = _compress(kv_p.reshape(bp, tp * _KV_STREAMS, HEAD_DIM), w1cat, cconst, w2b, b2, ncp_p)
    tk = 256
    e_p = _expand_matrix(tp).reshape(LANES, tp // tk, tk).transpose(1, 0, 2)
    o_nsa_p = _attn_p(q_p, kcv_p, kvb_p.reshape(bp, tp, KV_COLS), winb_p.reshape(bp, tp, WIN_COLS), gate_p, rest_p,
                      _overlap_matrix(n_cmp_p, n_sel_p, ncp_p), e_p, bp, tp, n_cmp_p, n_sel_p, tk=tk)
    cv_p, st_p = _conv(rest_p, jnp.zeros((bp, CONV_HALO, CONV_DIM), F32), conv_w_pad, db, lg, lb, bp, tp, 128, 128)
    y_p = _outproj(o_nsa_p, cv_p, w_out_b, xp, gate[:bp].reshape(bp, 1, d), g_post, 256, tp)

    r = SAMPLE_ROWS
    xs = jnp.pad(x_sample, ((0, 0), (0, r - ts), (0, 0))).reshape(bs * r, d)
    per_row = lambda v: jnp.repeat(v[bp:bp + bs], r, axis=0)
    tm_s = min(512, bs * r)
    n_mod = (bs * r) // tm_s
    q_s, kv_s, _, rest_s, _, gate_s = _inproj(
        xs, g_pre, per_row(scale).reshape(n_mod, tm_s, d), per_row(shift).reshape(n_mod, tm_s, d),
        w_main, w_gate, tm_s, tm_s)
    total = past + ts
    n_cmp_s = (total - CMP_BLOCK) // CMP_STRIDE + 1
    n_sel_s = -(-total // SEL_BLOCK)
    assert n_cmp_s <= past // CMP_STRIDE - 1 + 1 and (n_cmp_s - 1) * CMP_STRIDE + CMP_BLOCK <= past
    ncp_s = _round_up(past // CMP_STRIDE, LANES)
    assert ncp_s == past // CMP_STRIDE
    o_nsa_s = _attn_s(page_table, cache_kv.reshape(n_phys, page * _KV_STREAMS, HEAD_DIM),
                      q_s.reshape(bs, r, Q_COLS), kv_s.reshape(bs, r, KV_COLS),
                      state_win_kv.reshape(bs, WINDOW, WIN_COLS), rest_s.reshape(bs, r, _REST_COLS),
                      gate_s.reshape(bs, r, 2 * LANES), w1cat, cconst, w2b, b2,
                      _overlap_matrix(n_cmp_s, n_sel_s, ncp_s), _expand_matrix(past + LANES),
                      n_cmp_s, n_sel_s, past)
    u_init_s = jnp.pad(state_conv[0], ((0, 0), (CONV_HALO - (CONV_WIDTH - 1), 0), (0, 0)))
    cv_s, st_s = _conv(rest_s, u_init_s, conv_w_pad, db, lg, lb, bs, r, r, ts)
    y_s = _outproj(o_nsa_s.reshape(bs * r, NSA_WIDTH), cv_s, w_out_b, xs, per_row(gate).reshape(n_mod, tm_s, d),
                   g_post, tm_s, tm_s)

    keep = CONV_WIDTH - 1
    y_prompt = y_p.reshape(bp, tp, d)
    y_sample = y_s.reshape(bs, r, d)[:, :ts]
    kv_rows_prompt = kv_p.reshape(1, bp, tp, N_CACHE_SLOTS, N_KV_HEADS, HEAD_DIM)
    kv_rows_sample = kv_s.reshape(bs, r, KV_COLS)[:, :ts].reshape(1, bs, ts, N_CACHE_SLOTS, N_KV_HEADS, HEAD_DIM)
    win_p = rest_p.reshape(bp, tp, _REST_COLS)[:, tp - min(WINDOW, tp):, _REST_WIN * WIN_COLS:]
    win_prompt = win_p.reshape(1, bp, min(WINDOW, tp), N_WIN_SLOTS, N_KV_HEADS, HEAD_DIM)
    win_new_s = rest_s.reshape(bs, r, _REST_COLS)[:, :ts, _REST_WIN * WIN_COLS:]
    win_all_s = jnp.concatenate([state_win_kv[0].reshape(bs, WINDOW, WIN_COLS), win_new_s], axis=1)
    win_sample = win_all_s[:, win_all_s.shape[1] - WINDOW:].reshape(1, bs, WINDOW, N_WIN_SLOTS, N_KV_HEADS, HEAD_DIM)
    conv_prompt = st_p[:, CONV_HALO - keep:].reshape(1, bp, keep, CONV_DIM)
    conv_sample = st_s[:, CONV_HALO - keep:].reshape(1, bs, keep, CONV_DIM)
    return (y_prompt, y_sample, kv_rows_prompt, kv_rows_sample, win_prompt, win_sample, conv_prompt, conv_sample)
```

```python
import functools

import numpy as np
import jax
import jax.numpy as jnp
from jax import lax
from jax.experimental import pallas as pl
from jax.experimental.pallas import tpu as pltpu

D_MODEL = 2048
HEAD_DIM = 128
N_HEADS = 8
N_KV_HEADS = 2
HEADS_PER_GROUP = N_HEADS // N_KV_HEADS
NSA_WIDTH = N_HEADS * HEAD_DIM
CONV_DIM = D_MODEL - NSA_WIDTH
CMP_BLOCK = 32
CMP_STRIDE = 16
SEL_BLOCK = 64
N_SELECT = 16
WINDOW = 512
CONV_WIDTH = 31
N_CACHE_SLOTS = 4
N_WIN_SLOTS = 2
N_BRANCH = 3
Q_COLS = N_HEADS * HEAD_DIM
KV_COLS = N_CACHE_SLOTS * N_KV_HEADS * HEAD_DIM
WIN_COLS = N_WIN_SLOTS * N_KV_HEADS * HEAD_DIM
GATE_COLS = N_BRANCH * N_HEADS
_KV_STREAMS = N_CACHE_SLOTS * N_KV_HEADS
NORM_EPS = 1e-6
NEG_INF = -1e30
FORCE_BONUS = 1e6
SM_SCALE = HEAD_DIM ** -0.5

LANES = 128
SUBLANES = 8
CONV_HALO = 32
SAMPLE_ROWS = 8
VMEM_LIMIT = 56 * 1024 * 1024

F32 = jnp.float32
BF16 = jnp.bfloat16


def _sigmoid(x):
    return 1.0 / (1.0 + jnp.exp(-x))


def _silu(x):
    return x * _sigmoid(x)


def _dot(a, b):
    return jnp.dot(a, b, preferred_element_type=F32)


def _dot_nt(a, b):
    return lax.dot_general(a, b, (((1,), (1,)), ((), ())), preferred_element_type=F32)


def _masked_softmax(s, mask):
    s = jnp.where(mask, s, NEG_INF)
    e = jnp.where(mask, jnp.exp(s - jnp.max(s, axis=-1, keepdims=True)), 0.0)
    return e / jnp.maximum(jnp.sum(e, axis=-1, keepdims=True), 1e-30)


def _params(*sem):
    return pltpu.CompilerParams(dimension_semantics=sem, vmem_limit_bytes=VMEM_LIMIT)


def _ada_kernel(c_ref, w_ref, b_ref, o_ref):
    o_ref[...] = _dot(_silu(c_ref[...]).astype(BF16), w_ref[...].astype(BF16)) + b_ref[...]


def _ada(c_all, w_ada, b_ada, tn=512):
    rows, d = c_all.shape
    n = w_ada.shape[1]
    return pl.pallas_call(
        _ada_kernel,
        grid=(n // tn,),
        in_specs=[pl.BlockSpec((rows, d), lambda j: (0, 0)),
                  pl.BlockSpec((d, tn), lambda j: (0, j)),
                  pl.BlockSpec((1, tn), lambda j: (0, j))],
        out_specs=pl.BlockSpec((rows, tn), lambda j: (0, j)),
        out_shape=jax.ShapeDtypeStruct((rows, n), F32),
        compiler_params=_params("parallel"),
        name="ada",
    )(c_all, w_ada, b_ada.reshape(1, n))


def _cmpconst_kernel(pe_ref, w1_ref, b1_ref, o_ref):
    y = _dot(pe_ref[...].astype(BF16), w1_ref[...])
    o_ref[...] = y[0:1, :HEAD_DIM] + y[1:2, HEAD_DIM:] + b1_ref[...]


def _cmpconst(pe_rows, w1cat, b1):
    return pl.pallas_call(
        _cmpconst_kernel,
        grid=(2,),
        in_specs=[pl.BlockSpec((None, SUBLANES, 16 * HEAD_DIM), lambda s: (s, 0, 0)),
                  pl.BlockSpec((None, 16 * HEAD_DIM, 2 * HEAD_DIM), lambda s: (s, 0, 0)),
                  pl.BlockSpec((None, 1, HEAD_DIM), lambda s: (s, 0, 0))],
        out_specs=pl.BlockSpec((None, 1, HEAD_DIM), lambda s: (s, 0, 0)),
        out_shape=jax.ShapeDtypeStruct((2, 1, HEAD_DIM), F32),
        compiler_params=_params("parallel"),
        name="cmpconst",
    )(pe_rows, w1cat, b1)


_TN = 512
_J_KV = Q_COLS // _TN
_J_REST = (Q_COLS + KV_COLS) // _TN
_REST_COLS = NSA_WIDTH + 3 * CONV_DIM + WIN_COLS
_N_MAIN = Q_COLS + KV_COLS + _REST_COLS
_J_WIN = (_N_MAIN - WIN_COLS) // _TN
_NJ = _N_MAIN // _TN
_REST_Z = 0
_REST_A = 1
_REST_GL = 2
_REST_ZC = 3
_REST_WIN = (NSA_WIDTH + 3 * CONV_DIM) // WIN_COLS


def _inproj_kernel(x_ref, g_ref, sc_ref, sh_ref, w_ref, wg_ref,
                   q_ref, kv_ref, kvb_ref, rest_ref, winb_ref, gate_ref, h_ref):
    j = pl.program_id(1)

    @pl.when(j == 0)
    def _():
        x = x_ref[...]
        y = x * lax.rsqrt(jnp.mean(x * x, axis=-1, keepdims=True) + NORM_EPS) * g_ref[...]
        h = (y * (1.0 + sc_ref[...]) + sh_ref[...]).astype(BF16)
        h_ref[...] = h
        gate_ref[...] = _dot(h, wg_ref[...])

    acc = _dot(h_ref[...], w_ref[...])

    @pl.when(j < _J_KV)
    def _():
        q_ref[...] = acc.astype(BF16)

    tm = acc.shape[0]
    streams_per_tile = _TN // HEAD_DIM
    for jj in range(_J_KV, _J_REST):
        @pl.when(j == jj)
        def _(jj=jj):
            kvb_ref[...] = acc.astype(BF16)
            for cc in range(streams_per_tile):
                c = (jj - _J_KV) * streams_per_tile + cc
                kv_ref[pl.ds(c, tm, stride=_KV_STREAMS), :] = acc[:, cc * HEAD_DIM:(cc + 1) * HEAD_DIM]

    @pl.when(j >= _J_REST)
    def _():
        rest_ref[...] = acc

    @pl.when(j == _J_WIN)
    def _():
        winb_ref[...] = acc.astype(BF16)


def _inproj(x, g, scale3, shift3, w_main, w_gate, tm, rows_per_mod):
    m, d = x.shape
    mod_rows = scale3.shape[1]
    tiles_per_mod = rows_per_mod // tm

    def mod_map(i, j):
        return (i // tiles_per_mod, 0, 0)

    clampj = lambda j, lo, n: jnp.clip(j - lo, 0, n - 1)
    return pl.pallas_call(
        _inproj_kernel,
        grid=(m // tm, _NJ),
        in_specs=[pl.BlockSpec((tm, d), lambda i, j: (i, 0)),
                  pl.BlockSpec((1, d), lambda i, j: (0, 0)),
                  pl.BlockSpec((None, mod_rows, d), mod_map),
                  pl.BlockSpec((None, mod_rows, d), mod_map),
                  pl.BlockSpec((d, _TN), lambda i, j: (0, j)),
                  pl.BlockSpec((d, 2 * LANES), lambda i, j: (0, 0))],
        out_specs=[pl.BlockSpec((tm, _TN), lambda i, j: (i, clampj(j, 0, _J_KV))),
                   pl.BlockSpec((tm * _KV_STREAMS, HEAD_DIM), lambda i, j: (i, 0)),
                   pl.BlockSpec((tm, _TN), lambda i, j: (i, clampj(j, _J_KV, _J_REST - _J_KV))),
                   pl.BlockSpec((tm, _TN), lambda i, j: (i, clampj(j, _J_REST, _NJ - _J_REST))),
                   pl.BlockSpec((tm, _TN), lambda i, j: (i, 0)),
                   pl.BlockSpec((tm, 2 * LANES), lambda i, j: (i, 0))],
        out_shape=[jax.ShapeDtypeStruct((m, Q_COLS), BF16),
                   jax.ShapeDtypeStruct((m * _KV_STREAMS, HEAD_DIM), F32),
                   jax.ShapeDtypeStruct((m, KV_COLS), BF16),
                   jax.ShapeDtypeStruct((m, _REST_COLS), F32),
                   jax.ShapeDtypeStruct((m, WIN_COLS), BF16),
                   jax.ShapeDtypeStruct((m, 2 * LANES), F32)],
        scratch_shapes=[pltpu.VMEM((tm, d), BF16)],
        compiler_params=_params("parallel", "arbitrary"),
        name="inproj",
    )(x, g, scale3, shift3, w_main, w_gate)


def _compress_rows(load_rows, nch, w1, cconst, w2, b2):
    lhs = jnp.concatenate([load_rows(l) for l in range(CMP_STRIDE)], axis=1).astype(BF16)
    y = _dot(lhs, w1)
    second = pltpu.roll(y[:, HEAD_DIM:], shift=nch - 1, axis=0)
    h = _silu(y[:, :HEAD_DIM] + second + cconst)
    out = _dot(h.astype(BF16), w2) + b2
    row = lax.broadcasted_iota(jnp.int32, out.shape, 0)
    return jnp.where(row < nch - 1, out, 0.0)


def _compress_kernel(x_ref, w1_ref, cc_ref, w2_ref, b2_ref, o_ref, *, nch, ncp):
    for c in range(2 * N_KV_HEADS):
        s = c // N_KV_HEADS

        def load_rows(l, c=c):
            return x_ref[pl.ds(l * _KV_STREAMS + c, nch, stride=CMP_STRIDE * _KV_STREAMS), :]

        out = _compress_rows(load_rows, nch, w1_ref[s], cc_ref[s], w2_ref[s], b2_ref[s])
        if ncp > nch:
            out = jnp.concatenate([out, jnp.zeros((ncp - nch, HEAD_DIM), F32)], axis=0)
        o_ref[c] = out.astype(BF16)


def _compress(kv_streams, w1cat, cconst, w2b, b2, ncp):
    b, rows, _ = kv_streams.shape
    t = rows // _KV_STREAMS
    nch = t // CMP_STRIDE
    return pl.pallas_call(
        functools.partial(_compress_kernel, nch=nch, ncp=ncp),
        grid=(b,),
        in_specs=[pl.BlockSpec((None, rows, HEAD_DIM), lambda i: (i, 0, 0)),
                  pl.BlockSpec((2, 16 * HEAD_DIM, 2 * HEAD_DIM), lambda i: (0, 0, 0)),
                  pl.BlockSpec((2, 1, HEAD_DIM), lambda i: (0, 0, 0)),
                  pl.BlockSpec((2, HEAD_DIM, HEAD_DIM), lambda i: (0, 0, 0)),
                  pl.BlockSpec((2, 1, HEAD_DIM), lambda i: (0, 0, 0))],
        out_specs=pl.BlockSpec((None, 4, ncp, HEAD_DIM), lambda i: (i, 0, 0, 0)),
        out_shape=jax.ShapeDtypeStruct((b, 4, ncp, HEAD_DIM), BF16),
        compiler_params=_params("parallel"),
        name="compress",
    )(kv_streams, w1cat, cconst, w2b, b2)


def _cmp_branch(q_heads, kc, vc, qpos, n_cmp):
    rows = q_heads[0].shape[0]
    nio = lax.broadcasted_iota(jnp.int32, (rows, kc.shape[0]), 1)
    mask = (nio < n_cmp) & (nio * CMP_STRIDE + (CMP_BLOCK - 1) <= qpos)
    outs, probs = [], []
    for q in q_heads:
        p = _masked_softmax(_dot_nt(q, kc) * SM_SCALE, mask)
        outs.append(_dot(p.astype(BF16), vc))
        probs.append(p)
    return outs, probs


def _select_mask(psum, ov, qpos, n_sel):
    hi = psum.astype(BF16)
    lo = (psum - hi.astype(F32)).astype(BF16)
    imp = _dot(hi, ov) + _dot(lo, ov)
    rows = imp.shape[0]
    jio = lax.broadcasted_iota(jnp.int32, (rows, LANES), 1)
    allowed = (jio * SEL_BLOCK <= qpos) & (jio < n_sel)
    cur = qpos // SEL_BLOCK
    forced = (jio == 0) | (jio == cur) | (jio == cur - 1)
    score = jnp.where(allowed, imp + jnp.where(forced, FORCE_BONUS, 0.0), NEG_INF)
    rank = jnp.zeros((rows, LANES), F32)
    for i in range(n_sel):
        si = jnp.broadcast_to(score[:, i:i + 1], (rows, LANES))
        tie = jnp.where(jio > i, 1.0, 0.0)
        rank = rank + jnp.where(si > score, 1.0, jnp.where(si == score, tie, 0.0))
    return (rank < float(min(N_SELECT, n_sel))) & allowed


def _select_mask_t(psum, ovt, qpos_row, n_sel):
    hi = psum.astype(BF16)
    lo = (psum - hi.astype(F32)).astype(BF16)
    imp = _dot_nt(ovt, hi) + _dot_nt(ovt, lo)
    sr, rows = imp.shape
    jio = lax.broadcasted_iota(jnp.int32, (sr, rows), 0)
    allowed = (jio * SEL_BLOCK <= qpos_row) & (jio < n_sel)
    cur = qpos_row // SEL_BLOCK
    forced = (jio == 0) | (jio == cur) | (jio == cur - 1)
    score = jnp.where(allowed, imp + jnp.where(forced, FORCE_BONUS, 0.0), NEG_INF)
    rank = jnp.zeros((sr, rows), F32)
    for i in range(n_sel):
        si = jnp.broadcast_to(score[i:i + 1, :], (sr, rows))
        tie = jnp.where(jio > i, 1.0, 0.0)
        rank = rank + jnp.where(si > score, 1.0, jnp.where(si == score, tie, 0.0))
    sel_t = jnp.where((rank < float(min(N_SELECT, n_sel))) & allowed, 1.0, 0.0)
    sel_t = jnp.concatenate([sel_t, jnp.zeros((LANES - sr, rows), F32)], axis=0)
    return sel_t.T


def _biased_softmax(s, bias):
    s = s + bias
    e = jnp.exp(s - jnp.max(s, axis=-1, keepdims=True))
    return e / jnp.maximum(jnp.sum(e, axis=-1, keepdims=True), 1e-30)


def _attn_p_kernel(q_ref, kc_ref, vc_ref, ks_ref, vs_ref, kw_ref, vw_ref, gate_ref, z_ref, ovt_ref, e_ref,
                   o_ref, m_s, l_s, acc_s, *, tq, tk, n_cmp, n_sel):
    qi = pl.program_id(2)
    q0 = qi * tq
    qpos = q0 + lax.broadcasted_iota(jnp.int32, (tq, 1), 0)
    q_heads = [q_ref[:, r * HEAD_DIM:(r + 1) * HEAD_DIM] for r in range(HEADS_PER_GROUP)]

    o_cmp, probs = _cmp_branch(q_heads, kc_ref[...], vc_ref[...], qpos, n_cmp)
    psum = probs[0] + probs[1] + probs[2] + probs[3]
    qpos_row = q0 + lax.broadcasted_iota(jnp.int32, (1, tq), 1)
    selb = _select_mask_t(psum, ovt_ref[...], qpos_row, n_sel).astype(BF16)

    m_s[...] = jnp.full(m_s.shape, NEG_INF, F32)
    l_s[...] = jnp.zeros(l_s.shape, F32)
    acc_s[...] = jnp.zeros(acc_s.shape, F32)
    n_kt = (q0 + tq - 1) // tk + 1

    def sweep(kt, carry):
        k0 = pl.multiple_of(kt * tk, tk)
        k_t = ks_ref[pl.ds(k0, tk), :]
        v_t = vs_ref[pl.ds(k0, tk), :]
        kpos = k0 + lax.broadcasted_iota(jnp.int32, (tq, tk), 1)
        bias = jnp.where((_dot(selb, e_ref[kt]) > 0.5) & (kpos <= qpos), 0.0, NEG_INF)
        for r in range(HEADS_PER_GROUP):
            s = _dot_nt(q_heads[r], k_t) * SM_SCALE + bias
            m_prev = m_s[r]
            m_new = jnp.maximum(m_prev, jnp.max(s, axis=-1, keepdims=True))
            p = jnp.exp(s - jnp.concatenate([m_new] * (tk // LANES), axis=1))
            alpha = jnp.exp(m_prev - m_new)
            l_s[r] = alpha * l_s[r] + jnp.sum(p, axis=-1, keepdims=True)
            acc_s[r] = alpha * acc_s[r] + _dot(p.astype(BF16), v_t)
            m_s[r] = m_new
        return carry

    lax.fori_loop(0, n_kt, sweep, 0)

    w0 = pl.multiple_of(jnp.maximum(q0 - WINDOW, 0), LANES)
    k_w = kw_ref[pl.ds(w0, WINDOW + tq), :]
    v_w = vw_ref[pl.ds(w0, WINDOW + tq), :]
    dist = qpos - (w0 + lax.broadcasted_iota(jnp.int32, (tq, WINDOW + tq), 1))
    wbias = jnp.where((dist >= 0) & (dist <= WINDOW), 0.0, NEG_INF)

    gates = _sigmoid(gate_ref[...])
    for r in range(HEADS_PER_GROUP):
        o_slc = acc_s[r] / jnp.maximum(l_s[r], 1e-30)
        p = _biased_softmax(_dot_nt(q_heads[r], k_w) * SM_SCALE, wbias)
        o_win = _dot(p.astype(BF16), v_w)
        g_of = lambda br: jnp.broadcast_to(
            gates[:, br * HEADS_PER_GROUP + r:br * HEADS_PER_GROUP + r + 1], (tq, HEAD_DIM))
        o = g_of(0) * o_cmp[r] + g_of(1) * o_slc + g_of(2) * o_win
        o_ref[:, r * HEAD_DIM:(r + 1) * HEAD_DIM] = (
            o * _silu(z_ref[:, r * HEAD_DIM:(r + 1) * HEAD_DIM])).astype(BF16)


def _attn_p(q_bf, kcv, kv_bf3, win_bf3, gate, rest, ovt, e_p, b, t, n_cmp, n_sel, tq=128, tk=256):
    assert t >= WINDOW + tq and t % tk == 0 and tq == LANES
    nq = t // tq
    ncp = kcv.shape[2]
    sr = ovt.shape[0]
    gw = HEADS_PER_GROUP * HEAD_DIM
    row = lambda bi, g, qi: bi * nq + qi
    return pl.pallas_call(
        functools.partial(_attn_p_kernel, tq=tq, tk=tk, n_cmp=n_cmp, n_sel=n_sel),
        grid=(b, N_KV_HEADS, nq),
        in_specs=[pl.BlockSpec((tq, gw), lambda bi, g, qi: (row(bi, g, qi), g)),
                  pl.BlockSpec((None, None, ncp, HEAD_DIM), lambda bi, g, qi: (bi, g, 0, 0)),
                  pl.BlockSpec((None, None, ncp, HEAD_DIM), lambda bi, g, qi: (bi, N_KV_HEADS + g, 0, 0)),
                  pl.BlockSpec((None, t, HEAD_DIM), lambda bi, g, qi: (bi, 0, 2 * N_KV_HEADS + g)),
                  pl.BlockSpec((None, t, HEAD_DIM), lambda bi, g, qi: (bi, 0, 3 * N_KV_HEADS + g)),
                  pl.BlockSpec((None, t, HEAD_DIM), lambda bi, g, qi: (bi, 0, g)),
                  pl.BlockSpec((None, t, HEAD_DIM), lambda bi, g, qi: (bi, 0, N_KV_HEADS + g)),
                  pl.BlockSpec((tq, LANES), lambda bi, g, qi: (row(bi, g, qi), g)),
                  pl.BlockSpec((tq, gw), lambda bi, g, qi: (row(bi, g, qi), g)),
                  pl.BlockSpec((sr, ncp), lambda bi, g, qi: (0, 0)),
                  pl.BlockSpec((t // tk, LANES, tk), lambda bi, g, qi: (0, 0, 0))],
        out_specs=pl.BlockSpec((tq, gw), lambda bi, g, qi: (row(bi, g, qi), g)),
        out_shape=jax.ShapeDtypeStruct((b * t, NSA_WIDTH), BF16),
        scratch_shapes=[pltpu.VMEM((HEADS_PER_GROUP, tq, LANES), F32),
                        pltpu.VMEM((HEADS_PER_GROUP, tq, LANES), F32),
                        pltpu.VMEM((HEADS_PER_GROUP, tq, HEAD_DIM), F32)],
        compiler_params=_params("parallel", "parallel", "arbitrary"),
        name="attn_p",
    )(q_bf, kcv, kcv, kv_bf3, kv_bf3, win_bf3, win_bf3, gate, rest, ovt, e_p)


def _attn_s_kernel(pt_ref, cache_ref, wst4_ref, wst3_ref, q_ref, kvn_ref, winn_ref, gate_ref, z_ref,
                   w1_ref, cc_ref, w2_ref, b2_ref, ov_ref, e_ref, perm_ref, o_ref, wout_ref,
                   kvbuf, wbuf, wnew, sem, wsem, osem,
                   *, n_pages, page, n_cmp, n_sel, pos0, t_new):
    b = pl.program_id(0)
    nb = pl.num_programs(0)
    past = n_pages * page
    nch = past // CMP_STRIDE
    rows = HEADS_PER_GROUP * SAMPLE_ROWS
    n_ws = N_WIN_SLOTS * N_KV_HEADS
    state_rows = WINDOW * n_ws
    new_rows = t_new * n_ws

    def page_copy(seq, p, c, slot):
        return pltpu.make_async_copy(cache_ref.at[pt_ref[seq, p], :, c, :],
                                     kvbuf.at[slot, c, pl.ds(p * page, page), :], sem.at[slot])

    def win_copy(seq, c, slot):
        return pltpu.make_async_copy(wst4_ref.at[seq, :, c, :], wbuf.at[slot, c], wsem.at[slot])

    def fetch(seq, slot):
        def body(p, carry):
            for c in range(_KV_STREAMS):
                page_copy(seq, p, c, slot).start()
            return carry
        lax.fori_loop(0, n_pages, body, 0)
        for c in range(n_ws):
            win_copy(seq, c, slot).start()

    def fetch_wait(seq, slot):
        def body(p, carry):
            for c in range(_KV_STREAMS):
                page_copy(seq, p, c, slot).wait()
            return carry
        lax.fori_loop(0, n_pages, body, 0)
        for c in range(n_ws):
            win_copy(seq, c, slot).wait()

    def state_copies(seq, slot):
        kept = pltpu.make_async_copy(wst3_ref.at[seq, pl.ds(new_rows, state_rows - new_rows)],
                                     wout_ref.at[seq, pl.ds(0, state_rows - new_rows)], osem.at[slot])
        fresh = pltpu.make_async_copy(wnew.at[slot], wout_ref.at[seq, pl.ds(state_rows - new_rows, new_rows)],
                                      osem.at[slot])
        return kept, fresh

    @pl.when(b == 0)
    def _():
        fetch(0, 0)

    @pl.when(b + 1 < nb)
    def _():
        fetch(b + 1, (b + 1) % 2)

    slot = b % 2

    @pl.when(b >= 2)
    def _():
        for cp in state_copies(b - 2, slot):
            cp.wait()

    for t in range(t_new):
        for c in range(n_ws):
            wnew[slot, pl.ds(t * n_ws + c, 1), :] = winn_ref[pl.ds(t, 1), c * HEAD_DIM:(c + 1) * HEAD_DIM]
    for cp in state_copies(b, slot):
        cp.start()

    fetch_wait(b, slot)

    qpos = pos0 + lax.broadcasted_iota(jnp.int32, (rows, 1), 0) % SAMPLE_ROWS
    kvn = kvn_ref[...].astype(F32)
    perm = perm_ref[...]
    blk = LANES
    cpb = blk // CMP_STRIDE

    def chunk_rows(c):
        xs = kvbuf[slot, c].astype(BF16)
        out = []
        for i in range(past // blk):
            y = _dot(perm, xs[i * blk:(i + 1) * blk])
            out.append(jnp.concatenate([y[l * cpb:(l + 1) * cpb] for l in range(CMP_STRIDE)], axis=1))
        return jnp.concatenate(out, axis=0).astype(BF16)

    def compress_slot(s):
        lhs = jnp.concatenate([chunk_rows(s * N_KV_HEADS + g) for g in range(N_KV_HEADS)], axis=0)
        y = _dot(lhs, w1_ref[s])
        second = pltpu.roll(y[:, HEAD_DIM:], shift=lhs.shape[0] - 1, axis=0)
        h = _silu(y[:, :HEAD_DIM] + second + cc_ref[s])
        out = _dot(h.astype(BF16), w2_ref[s]) + b2_ref[s]
        row = lax.broadcasted_iota(jnp.int32, out.shape, 0) % nch
        out = jnp.where(row < nch - 1, out, 0.0).astype(BF16)
        return [out[g * nch:(g + 1) * nch] for g in range(N_KV_HEADS)]

    kc_all = compress_slot(0)
    vc_all = compress_slot(1)
    qf = q_ref[...].astype(F32)
    gates = _sigmoid(gate_ref[...])
    zero_tail = jnp.zeros((LANES - 2 * SUBLANES, HEAD_DIM), BF16)
    pad8 = jnp.zeros((SUBLANES, HEAD_DIM), F32)

    def with_new_rows(old_bf, new_f32):
        new16 = jnp.concatenate([new_f32, pad8], axis=0).astype(BF16)
        return jnp.concatenate([old_bf, new16, zero_tail], axis=0)

    def stack_heads(x8):
        return jnp.concatenate([x8] * HEADS_PER_GROUP, axis=0)

    for g in range(N_KV_HEADS):
        heads = [qf[:, (g * HEADS_PER_GROUP + r) * HEAD_DIM:(g * HEADS_PER_GROUP + r + 1) * HEAD_DIM]
                 for r in range(HEADS_PER_GROUP)]
        qg = jnp.concatenate(heads, axis=0).astype(BF16)

        (o_cmp,), (p_cmp,) = _cmp_branch([qg], kc_all[g], vc_all[g], qpos, n_cmp)
        psum = (p_cmp[0:8] + p_cmp[8:16]) + (p_cmp[16:24] + p_cmp[24:32])
        sel = _select_mask(psum, ov_ref[...], qpos[0:SAMPLE_ROWS], n_sel)
        selb = jnp.where(sel, 1.0, 0.0).astype(BF16)

        c_k = (2 * N_KV_HEADS + g) * HEAD_DIM
        c_v = (3 * N_KV_HEADS + g) * HEAD_DIM
        stream = lambda c: kvbuf[slot, c].astype(BF16)
        k_all = with_new_rows(stream(2 * N_KV_HEADS + g), kvn[:, c_k:c_k + HEAD_DIM])
        v_all = with_new_rows(stream(3 * N_KV_HEADS + g), kvn[:, c_v:c_v + HEAD_DIM])
        nk = past + LANES
        kpos = lax.broadcasted_iota(jnp.int32, (rows, nk), 1)
        smask = (stack_heads(_dot(selb, e_ref[...])) > 0.5) & (kpos <= qpos)
        p = _masked_softmax(_dot_nt(qg, k_all) * SM_SCALE, smask)
        o_slc = _dot(p.astype(BF16), v_all)

        wk = g * HEAD_DIM
        wv = (N_KV_HEADS + g) * HEAD_DIM
        kw_all = with_new_rows(wbuf[slot, g].astype(BF16), winn_ref[:, wk:wk + HEAD_DIM])
        vw_all = with_new_rows(wbuf[slot, N_KV_HEADS + g].astype(BF16), winn_ref[:, wv:wv + HEAD_DIM])
        wpos = (pos0 - WINDOW) + lax.broadcasted_iota(jnp.int32, (rows, WINDOW + LANES), 1)
        dist = qpos - wpos
        wmask = (dist >= 0) & (dist <= WINDOW) & (wpos >= 0)
        p = _masked_softmax(_dot_nt(qg, kw_all) * SM_SCALE, wmask)
        o_win = _dot(p.astype(BF16), vw_all)

        for r in range(HEADS_PER_GROUP):
            h = g * HEADS_PER_GROUP + r
            rs = slice(r * SAMPLE_ROWS, (r + 1) * SAMPLE_ROWS)
            g_of = lambda br: jnp.broadcast_to(
                gates[:, g * LANES + br * HEADS_PER_GROUP + r:g * LANES + br * HEADS_PER_GROUP + r + 1],
                (SAMPLE_ROWS, HEAD_DIM))
            o = g_of(0) * o_cmp[rs] + g_of(1) * o_slc[rs] + g_of(2) * o_win[rs]
            o_ref[:, h * HEAD_DIM:(h + 1) * HEAD_DIM] = (
                o * _silu(z_ref[:, h * HEAD_DIM:(h + 1) * HEAD_DIM])).astype(BF16)

    @pl.when((b == nb - 1) & (b >= 1))
    def _():
        for cp in state_copies(b - 1, 1 - slot):
            cp.wait()

    @pl.when(b == nb - 1)
    def _():
        for cp in state_copies(b, slot):
            cp.wait()


def _attn_s(page_table, cache4, wst4, q3, kvn3, rest3, gate3, w1cat, cconst, w2b, b2, ov, e_s,
            n_cmp, n_sel, pos0, t_new):
    nb, n_pages = page_table.shape
    page = cache4.shape[1]
    past = n_pages * page
    n_ws = N_WIN_SLOTS * N_KV_HEADS
    assert wst4.shape[1:] == (WINDOW, n_ws, HEAD_DIM) and past % LANES == 0
    ncp = ov.shape[0]
    wst3 = wst4.reshape(nb, WINDOW * n_ws, HEAD_DIM)
    grid_spec = pltpu.PrefetchScalarGridSpec(
        num_scalar_prefetch=1,
        grid=(nb,),
        in_specs=[pl.BlockSpec(memory_space=pl.ANY),
                  pl.BlockSpec(memory_space=pl.ANY),
                  pl.BlockSpec(memory_space=pl.ANY),
                  pl.BlockSpec((None, SAMPLE_ROWS, Q_COLS), lambda i, pt: (i, 0, 0)),
                  pl.BlockSpec((None, SAMPLE_ROWS, KV_COLS), lambda i, pt: (i, 0, 0)),
                  pl.BlockSpec((None, SAMPLE_ROWS, WIN_COLS), lambda i, pt: (i, 0, _REST_WIN)),
                  pl.BlockSpec((None, SAMPLE_ROWS, 2 * LANES), lambda i, pt: (i, 0, 0)),
                  pl.BlockSpec((None, SAMPLE_ROWS, NSA_WIDTH), lambda i, pt: (i, 0, _REST_Z)),
                  pl.BlockSpec((2, 16 * HEAD_DIM, 2 * HEAD_DIM), lambda i, pt: (0, 0, 0)),
                  pl.BlockSpec((2, 1, HEAD_DIM), lambda i, pt: (0, 0, 0)),
                  pl.BlockSpec((2, HEAD_DIM, HEAD_DIM), lambda i, pt: (0, 0, 0)),
                  pl.BlockSpec((2, 1, HEAD_DIM), lambda i, pt: (0, 0, 0)),
                  pl.BlockSpec((ncp, LANES), lambda i, pt: (0, 0)),
                  pl.BlockSpec((LANES, past + LANES), lambda i, pt: (0, 0)),
                  pl.BlockSpec((LANES, LANES), lambda i, pt: (0, 0))],
        out_specs=[pl.BlockSpec((None, SAMPLE_ROWS, NSA_WIDTH), lambda i, pt: (i, 0, 0)),
                   pl.BlockSpec(memory_space=pl.ANY)],
        scratch_shapes=[pltpu.VMEM((2, _KV_STREAMS, past, HEAD_DIM), F32),
                        pltpu.VMEM((2, n_ws, WINDOW, HEAD_DIM), F32),
                        pltpu.VMEM((2, t_new * n_ws, HEAD_DIM), F32),
                        pltpu.SemaphoreType.DMA((2,)),
                        pltpu.SemaphoreType.DMA((2,)),
                        pltpu.SemaphoreType.DMA((2,))],
    )
    return pl.pallas_call(
        functools.partial(_attn_s_kernel, n_pages=n_pages, page=page, n_cmp=n_cmp, n_sel=n_sel, pos0=pos0,
                          t_new=t_new),
        grid_spec=grid_spec,
        out_shape=[jax.ShapeDtypeStruct((nb, SAMPLE_ROWS, NSA_WIDTH), BF16),
                   jax.ShapeDtypeStruct((nb, WINDOW * n_ws, HEAD_DIM), F32)],
        compiler_params=_params("arbitrary"),
        name="attn_s",
    )(page_table, cache4, wst4, wst3, q3, kvn3, rest3, gate3, rest3, w1cat, cconst, w2b, b2, ov, e_s,
      _chunk_permutation())


_CONV_RC = 64
_CONV_LC = 256


def _conv_kernel(a_ref, gl_ref, ah_ref, glh_ref, init_ref, zc_ref, w_ref, db_ref, lg_ref, lb_ref,
                 cv_ref, st_ref, uext, ybuf, *, tt, t_valid):
    ti = pl.program_id(1)
    nt = pl.num_programs(1)
    halo = ah_ref[...] * _sigmoid(glh_ref[...])
    uext[0:CONV_HALO] = jnp.where(ti == 0, init_ref[...], halo)
    uext[CONV_HALO:CONV_HALO + tt] = a_ref[...] * _sigmoid(gl_ref[...])

    off = CONV_HALO - (CONV_WIDTH - 1)
    rc = min(_CONV_RC, tt)
    for r0 in range(0, tt, rc):
        for c0 in range(0, CONV_DIM, _CONV_LC):
            acc = jnp.broadcast_to(db_ref[:, c0:c0 + _CONV_LC], (rc, _CONV_LC))
            for k in range(CONV_WIDTH):
                acc = acc + w_ref[k:k + 1, c0:c0 + _CONV_LC] * uext[pl.ds(r0 + off + k, rc), c0:c0 + _CONV_LC]
            ybuf[r0:r0 + rc, c0:c0 + _CONV_LC] = acc

    y = ybuf[...]
    mu = jnp.mean(y, axis=-1, keepdims=True)
    var = jnp.mean(jnp.square(y - mu), axis=-1, keepdims=True)
    ln = (y - mu) * lax.rsqrt(var + NORM_EPS) * lg_ref[...] + lb_ref[...]
    cv_ref[...] = (_silu(ln) * _silu(zc_ref[...])).astype(BF16)

    @pl.when(ti == nt - 1)
    def _():
        st_ref[...] = uext[pl.ds(t_valid, CONV_HALO), :]


def _conv(rest, u_init, w_pad, db, lg, lb, b, t, tt, t_valid):
    nt = t // tt
    hb = tt // CONV_HALO if tt >= CONV_HALO else None
    row = lambda bi, ti: bi * nt + ti
    if hb is None:
        assert nt == 1
        halo_map = lambda col: (lambda bi, ti: (0, col))
        halo_rows = tt
    else:
        halo_map = lambda col: (lambda bi, ti: (jnp.maximum(row(bi, ti) * hb - 1, 0), col))
        halo_rows = CONV_HALO
    vec = lambda: pl.BlockSpec((1, CONV_DIM), lambda bi, ti: (0, 0))
    kern = functools.partial(_conv_kernel, tt=tt, t_valid=t_valid)
    if hb is None:
        kern = functools.partial(_conv_kernel_nohalo, tt=tt, t_valid=t_valid)
    return pl.pallas_call(
        kern,
        grid=(b, nt),
        in_specs=[pl.BlockSpec((tt, CONV_DIM), lambda bi, ti: (row(bi, ti), _REST_A)),
                  pl.BlockSpec((tt, CONV_DIM), lambda bi, ti: (row(bi, ti), _REST_GL)),
                  pl.BlockSpec((halo_rows, CONV_DIM), halo_map(_REST_A)),
                  pl.BlockSpec((halo_rows, CONV_DIM), halo_map(_REST_GL)),
                  pl.BlockSpec((None, CONV_HALO, CONV_DIM), lambda bi, ti: (bi, 0, 0)),
                  pl.BlockSpec((tt, CONV_DIM), lambda bi, ti: (row(bi, ti), _REST_ZC)),
                  pl.BlockSpec((CONV_HALO, CONV_DIM), lambda bi, ti: (0, 0)),
                  vec(), vec(), vec()],
        out_specs=[pl.BlockSpec((tt, CONV_DIM), lambda bi, ti: (row(bi, ti), 0)),
                   pl.BlockSpec((None, CONV_HALO, CONV_DIM), lambda bi, ti: (bi, 0, 0))],
        out_shape=[jax.ShapeDtypeStruct((b * t, CONV_DIM), BF16),
                   jax.ShapeDtypeStruct((b, CONV_HALO, CONV_DIM), F32)],
        scratch_shapes=[pltpu.VMEM((CONV_HALO + tt, CONV_DIM), F32),
                        pltpu.VMEM((tt, CONV_DIM), F32)],
        compiler_params=_params("parallel", "arbitrary"),
        name="conv",
    )(rest, rest, rest, rest, u_init, rest, w_pad, db, lg, lb)


def _conv_kernel_nohalo(a_ref, gl_ref, ah_ref, glh_ref, init_ref, zc_ref, w_ref, db_ref, lg_ref, lb_ref,
                        cv_ref, st_ref, uext, ybuf, *, tt, t_valid):
    del ah_ref, glh_ref
    uext[0:CONV_HALO] = init_ref[...]
    uext[CONV_HALO:CONV_HALO + tt] = a_ref[...] * _sigmoid(gl_ref[...])
    off = CONV_HALO - (CONV_WIDTH - 1)
    for c0 in range(0, CONV_DIM, _CONV_LC):
        acc = jnp.broadcast_to(db_ref[:, c0:c0 + _CONV_LC], (tt, _CONV_LC))
        for k in range(CONV_WIDTH):
            acc = acc + w_ref[k:k + 1, c0:c0 + _CONV_LC] * uext[pl.ds(off + k, tt), c0:c0 + _CONV_LC]
        ybuf[:, c0:c0 + _CONV_LC] = acc
    y = ybuf[...]
    mu = jnp.mean(y, axis=-1, keepdims=True)
    var = jnp.mean(jnp.square(y - mu), axis=-1, keepdims=True)
    ln = (y - mu) * lax.rsqrt(var + NORM_EPS) * lg_ref[...] + lb_ref[...]
    cv_ref[...] = (_silu(ln) * _silu(zc_ref[...])).astype(BF16)
    st_ref[...] = uext[pl.ds(t_valid, CONV_HALO), :]


def _outproj_kernel(on_ref, cv_ref, w_ref, x_ref, gate_ref, g_ref, y_ref):
    mix = _dot(on_ref[...], w_ref[0:NSA_WIDTH, :]) + _dot(cv_ref[...], w_ref[NSA_WIDTH:, :])
    nrm = mix * lax.rsqrt(jnp.mean(mix * mix, axis=-1, keepdims=True) + NORM_EPS) * g_ref[...]
    y_ref[...] = x_ref[...] + gate_ref[...] * nrm


def _outproj(o_nsa, cv, w_out_b, x, gate3, g_post, tm, rows_per_mod):
    m, d = x.shape
    mod_rows = gate3.shape[1]
    tiles_per_mod = rows_per_mod // tm
    return pl.pallas_call(
        _outproj_kernel,
        grid=(m // tm,),
        in_specs=[pl.BlockSpec((tm, NSA_WIDTH), lambda i: (i, 0)),
                  pl.BlockSpec((tm, CONV_DIM), lambda i: (i, 0)),
                  pl.BlockSpec((d, d), lambda i: (0, 0)),
                  pl.BlockSpec((tm, d), lambda i: (i, 0)),
                  pl.BlockSpec((None, mod_rows, d), lambda i: (i // tiles_per_mod, 0, 0)),
                  pl.BlockSpec((1, d), lambda i: (0, 0))],
        out_specs=pl.BlockSpec((tm, d), lambda i: (i, 0)),
        out_shape=jax.ShapeDtypeStruct((m, d), F32),
        compiler_params=_params("parallel"),
        name="outproj",
    )(o_nsa, cv, w_out_b, x, gate3, g_post)


def _overlap_matrix(n_cmp, n_sel, ncp):
    ci = np.arange(ncp)[:, None] * CMP_STRIDE
    sj = np.arange(LANES)[None, :] * SEL_BLOCK
    ov = (ci < sj + SEL_BLOCK) & (ci + CMP_BLOCK > sj)
    ov &= (np.arange(ncp)[:, None] < n_cmp) & (np.arange(LANES)[None, :] < n_sel)
    return jnp.asarray(ov.astype(np.float32), dtype=BF16)


def _overlap_matrix_t(n_cmp, n_sel, ncp):
    sr = _round_up(n_sel, 2 * SUBLANES)
    return _overlap_matrix(n_cmp, n_sel, ncp).T[:sr]


def _chunk_permutation():
    p = np.zeros((LANES, LANES), np.float32)
    for j in range(LANES // CMP_STRIDE):
        for l in range(CMP_STRIDE):
            p[l * (LANES // CMP_STRIDE) + j, CMP_STRIDE * j + l] = 1.0
    return jnp.asarray(p, dtype=BF16)


def _expand_matrix(n_keys):
    e = (np.arange(n_keys)[None, :] // SEL_BLOCK) == np.arange(LANES)[:, None]
    return jnp.asarray(e.astype(np.float32), dtype=BF16)


def _round_up(x, m):
    return -(-x // m) * m


def kernel(x_prompt, x_sample, c_prompt, c_sample, cache_kv, page_table, state_win_kv, state_conv, w_ada, b_ada, norm_pre, norm_post, w_in, cmp_pe, cmp_w1, cmp_b1, cmp_w2, cmp_b2, conv_dw, conv_db, conv_ln_g, conv_ln_b, w_out):
    assert w_ada.shape[0] == 1, "single-layer trunk"
    bp, tp, d = x_prompt.shape
    bs, ts, _ = x_sample.shape
    n_phys, page = cache_kv.shape[1], cache_kv.shape[2]
    n_pages = page_table.shape[1]
    past = n_pages * page
    assert ts <= SAMPLE_ROWS and d == D_MODEL

    w = w_in[0]
    c_q, c_kv, c_win, c_gate = 0, Q_COLS, Q_COLS + KV_COLS, Q_COLS + KV_COLS + WIN_COLS
    c_rest = c_gate + GATE_COLS
    w_main = jnp.concatenate([w[:, c_q:c_win], w[:, c_rest:], w[:, c_win:c_gate]], axis=1).astype(BF16)
    wg = w[:, c_gate:c_rest].reshape(d, N_BRANCH, N_KV_HEADS, HEADS_PER_GROUP).transpose(0, 2, 1, 3)
    wg = wg.reshape(d, N_KV_HEADS, N_BRANCH * HEADS_PER_GROUP)
    w_gate = jnp.pad(wg, ((0, 0), (0, 0), (0, LANES - N_BRANCH * HEADS_PER_GROUP))).reshape(d, 2 * LANES).astype(BF16)
    w_out_b = w_out[0].astype(BF16)
    w1 = cmp_w1[0]
    w1cat = jnp.concatenate([w1[:, :CMP_STRIDE].reshape(2, CMP_STRIDE * HEAD_DIM, HEAD_DIM),
                             w1[:, CMP_STRIDE:].reshape(2, CMP_STRIDE * HEAD_DIM, HEAD_DIM)], axis=2).astype(BF16)
    w2b = cmp_w2[0].astype(BF16)
    pe_rows = jnp.pad(cmp_pe[0].reshape(2, 2, CMP_STRIDE * HEAD_DIM), ((0, 0), (0, SUBLANES - 2), (0, 0)))
    b1 = cmp_b1[0].reshape(2, 1, HEAD_DIM)
    b2 = cmp_b2[0].reshape(2, 1, HEAD_DIM)
    conv_w_pad = jnp.pad(conv_dw[0], ((0, CONV_HALO - CONV_WIDTH), (0, 0)))
    db = conv_db[0].reshape(1, CONV_DIM)
    lg = conv_ln_g[0].reshape(1, CONV_DIM)
    lb = conv_ln_b[0].reshape(1, CONV_DIM)
    g_pre = norm_pre[0].reshape(1, d)
    g_post = norm_post[0].reshape(1, d)

    n_c = bp + bs
    c_all = jnp.pad(jnp.concatenate([c_prompt, c_sample], axis=0), ((0, _round_up(n_c, SUBLANES) - n_c), (0, 0)))
    mod = _ada(c_all, w_ada[0], b_ada[0])
    shift, scale, gate = mod[:, :d], mod[:, d:2 * d], mod[:, 2 * d:]
    cconst = _cmpconst(pe_rows, w1cat, b1)

    tm_p = 512
    xp = x_prompt.reshape(bp * tp, d)
    q_p, kv_p, kvb_p, rest_p, winb_p, gate_p = _inproj(
        xp, g_pre, scale[:bp].reshape(bp, 1, d), shift[:bp].reshape(bp, 1, d), w_main, w_gate, tm_p, tp)
    n_cmp_p = (tp - CMP_BLOCK) // CMP_STRIDE + 1
    n_sel_p = -(-tp // SEL_BLOCK)
    ncp_p = _round_up(tp // CMP_STRIDE, LANES)
    kcv_p = _compress(kv_p.reshape(bp, tp * _KV_STREAMS, HEAD_DIM), w1cat, cconst, w2b, b2, ncp_p)
    tk = 256
    e_p = _expand_matrix(tp).reshape(LANES, tp // tk, tk).transpose(1, 0, 2)
    o_nsa_p = _attn_p(q_p, kcv_p, kvb_p.reshape(bp, tp, KV_COLS), winb_p.reshape(bp, tp, WIN_COLS), gate_p, rest_p,
                      _overlap_matrix_t(n_cmp_p, n_sel_p, ncp_p), e_p, bp, tp, n_cmp_p, n_sel_p, tk=tk)
    cv_p, st_p = _conv(rest_p, jnp.zeros((bp, CONV_HALO, CONV_DIM), F32), conv_w_pad, db, lg, lb, bp, tp, 128, 128)
    y_p = _outproj(o_nsa_p, cv_p, w_out_b, xp, gate[:bp].reshape(bp, 1, d), g_post, 256, tp)

    r = SAMPLE_ROWS
    xs = jnp.pad(x_sample, ((0, 0), (0, r - ts), (0, 0))).reshape(bs * r, d)
    per_row = lambda v: jnp.repeat(v[bp:bp + bs], r, axis=0)
    tm_s = min(512, bs * r)
    n_mod = (bs * r) // tm_s
    q_s, kv_s, kvb_s, rest_s, _, gate_s = _inproj(
        xs, g_pre, per_row(scale).reshape(n_mod, tm_s, d), per_row(shift).reshape(n_mod, tm_s, d),
        w_main, w_gate, tm_s, tm_s)
    total = past + ts
    n_cmp_s = (total - CMP_BLOCK) // CMP_STRIDE + 1
    n_sel_s = -(-total // SEL_BLOCK)
    assert n_cmp_s <= past // CMP_STRIDE - 1 + 1 and (n_cmp_s - 1) * CMP_STRIDE + CMP_BLOCK <= past
    ncp_s = _round_up(past // CMP_STRIDE, LANES)
    assert ncp_s == past // CMP_STRIDE
    n_ws = N_WIN_SLOTS * N_KV_HEADS
    o_nsa_s, win_state_s = _attn_s(
        page_table, cache_kv.reshape(n_phys, page, _KV_STREAMS, HEAD_DIM),
        state_win_kv.reshape(bs, WINDOW, n_ws, HEAD_DIM),
        q_s.reshape(bs, r, Q_COLS), kvb_s.reshape(bs, r, KV_COLS), rest_s.reshape(bs, r, _REST_COLS),
        gate_s.reshape(bs, r, 2 * LANES), w1cat, cconst, w2b, b2,
        _overlap_matrix(n_cmp_s, n_sel_s, ncp_s), _expand_matrix(past + LANES),
        n_cmp_s, n_sel_s, past, ts)
    u_init_s = jnp.pad(state_conv[0], ((0, 0), (CONV_HALO - (CONV_WIDTH - 1), 0), (0, 0)))
    cv_s, st_s = _conv(rest_s, u_init_s, conv_w_pad, db, lg, lb, bs, r, r, ts)
    y_s = _outproj(o_nsa_s.reshape(bs * r, NSA_WIDTH), cv_s, w_out_b, xs, per_row(gate).reshape(n_mod, tm_s, d),
                   g_post, tm_s, tm_s)

    keep = CONV_WIDTH - 1
    y_prompt = y_p.reshape(bp, tp, d)
    y_sample = y_s.reshape(bs, r, d)[:, :ts]
    kv_rows_prompt = kv_p.reshape(1, bp, tp, N_CACHE_SLOTS, N_KV_HEADS, HEAD_DIM)
    kv_rows_sample = kv_s.reshape(1, bs, r, N_CACHE_SLOTS, N_KV_HEADS, HEAD_DIM)[:, :, :ts]
    win_p = rest_p.reshape(bp, tp, _REST_COLS)[:, tp - min(WINDOW, tp):, _REST_WIN * WIN_COLS:]
    win_prompt = win_p.reshape(1, bp, min(WINDOW, tp), N_WIN_SLOTS, N_KV_HEADS, HEAD_DIM)
    win_sample = win_state_s.reshape(1, bs, WINDOW, N_WIN_SLOTS, N_KV_HEADS, HEAD_DIM)
    conv_prompt = st_p[:, CONV_HALO - keep:].reshape(1, bp, keep, CONV_DIM)
    conv_sample = st_s[:, CONV_HALO - keep:].reshape(1, bs, keep, CONV_DIM)
    return (y_prompt, y_sample, kv_rows_prompt, kv_rows_sample, win_prompt, win_sample, conv_prompt, conv_sample)
```

```python
import functools

import numpy as np
import jax
import jax.numpy as jnp
from jax import lax
from jax.experimental import pallas as pl
from jax.experimental.pallas import tpu as pltpu

D_MODEL = 2048
HEAD_DIM = 128
N_HEADS = 8
N_KV_HEADS = 2
HEADS_PER_GROUP = N_HEADS // N_KV_HEADS
NSA_WIDTH = N_HEADS * HEAD_DIM
CONV_DIM = D_MODEL - NSA_WIDTH
CMP_BLOCK = 32
CMP_STRIDE = 16
SEL_BLOCK = 64
N_SELECT = 16
WINDOW = 512
CONV_WIDTH = 31
N_CACHE_SLOTS = 4
N_WIN_SLOTS = 2
N_BRANCH = 3
Q_COLS = N_HEADS * HEAD_DIM
KV_COLS = N_CACHE_SLOTS * N_KV_HEADS * HEAD_DIM
WIN_COLS = N_WIN_SLOTS * N_KV_HEADS * HEAD_DIM
GATE_COLS = N_BRANCH * N_HEADS
_KV_STREAMS = N_CACHE_SLOTS * N_KV_HEADS
NORM_EPS = 1e-6
NEG_INF = -1e30
FORCE_BONUS = 1e6
SM_SCALE = HEAD_DIM ** -0.5

LANES = 128
SUBLANES = 8
CONV_HALO = 32
SAMPLE_ROWS = 8
VMEM_LIMIT = 56 * 1024 * 1024

F32 = jnp.float32
BF16 = jnp.bfloat16


def _sigmoid(x):
    return 1.0 / (1.0 + jnp.exp(-x))


def _silu(x):
    return x * _sigmoid(x)


def _dot(a, b):
    return jnp.dot(a, b, preferred_element_type=F32)


def _dot_nt(a, b):
    return lax.dot_general(a, b, (((1,), (1,)), ((), ())), preferred_element_type=F32)


def _masked_softmax(s, mask):
    s = jnp.where(mask, s, NEG_INF)
    e = jnp.where(mask, jnp.exp(s - jnp.max(s, axis=-1, keepdims=True)), 0.0)
    return e / jnp.maximum(jnp.sum(e, axis=-1, keepdims=True), 1e-30)


def _params(*sem):
    return pltpu.CompilerParams(dimension_semantics=sem, vmem_limit_bytes=VMEM_LIMIT)


def _ada_kernel(c_ref, w_ref, b_ref, o_ref):
    o_ref[...] = _dot(_silu(c_ref[...]).astype(BF16), w_ref[...].astype(BF16)) + b_ref[...]


def _ada(c_all, w_ada, b_ada, tn=512):
    rows, d = c_all.shape
    n = w_ada.shape[1]
    return pl.pallas_call(
        _ada_kernel,
        grid=(n // tn,),
        in_specs=[pl.BlockSpec((rows, d), lambda j: (0, 0)),
                  pl.BlockSpec((d, tn), lambda j: (0, j)),
                  pl.BlockSpec((1, tn), lambda j: (0, j))],
        out_specs=pl.BlockSpec((rows, tn), lambda j: (0, j)),
        out_shape=jax.ShapeDtypeStruct((rows, n), F32),
        compiler_params=_params("parallel"),
        name="ada",
    )(c_all, w_ada, b_ada.reshape(1, n))


def _cmpconst_kernel(pe_ref, w1_ref, b1_ref, o_ref):
    y = _dot(pe_ref[...].astype(BF16), w1_ref[...])
    o_ref[...] = y[0:1, :HEAD_DIM] + y[1:2, HEAD_DIM:] + b1_ref[...]


def _cmpconst(pe_rows, w1cat, b1):
    return pl.pallas_call(
        _cmpconst_kernel,
        grid=(2,),
        in_specs=[pl.BlockSpec((None, SUBLANES, 16 * HEAD_DIM), lambda s: (s, 0, 0)),
                  pl.BlockSpec((None, 16 * HEAD_DIM, 2 * HEAD_DIM), lambda s: (s, 0, 0)),
                  pl.BlockSpec((None, 1, HEAD_DIM), lambda s: (s, 0, 0))],
        out_specs=pl.BlockSpec((None, 1, HEAD_DIM), lambda s: (s, 0, 0)),
        out_shape=jax.ShapeDtypeStruct((2, 1, HEAD_DIM), F32),
        compiler_params=_params("parallel"),
        name="cmpconst",
    )(pe_rows, w1cat, b1)


_TN = 512
_J_KV = Q_COLS // _TN
_J_REST = (Q_COLS + KV_COLS) // _TN
_REST_COLS = NSA_WIDTH + 3 * CONV_DIM + WIN_COLS
_N_MAIN = Q_COLS + KV_COLS + _REST_COLS
_J_WIN = (_N_MAIN - WIN_COLS) // _TN
_NJ = _N_MAIN // _TN
_REST_Z = 0
_REST_A = 1
_REST_GL = 2
_REST_ZC = 3
_REST_WIN = (NSA_WIDTH + 3 * CONV_DIM) // WIN_COLS


def _inproj_kernel(x_ref, g_ref, sc_ref, sh_ref, w_ref, wg_ref,
                   q_ref, kv_ref, kvb_ref, rest_ref, winb_ref, gate_ref, h_ref):
    j = pl.program_id(1)

    @pl.when(j == 0)
    def _():
        x = x_ref[...]
        y = x * lax.rsqrt(jnp.mean(x * x, axis=-1, keepdims=True) + NORM_EPS) * g_ref[...]
        h = (y * (1.0 + sc_ref[...]) + sh_ref[...]).astype(BF16)
        h_ref[...] = h
        gate_ref[...] = _dot(h, wg_ref[...])

    acc = _dot(h_ref[...], w_ref[...])

    @pl.when(j < _J_KV)
    def _():
        q_ref[...] = acc.astype(BF16)

    tm = acc.shape[0]
    streams_per_tile = _TN // HEAD_DIM
    for jj in range(_J_KV, _J_REST):
        @pl.when(j == jj)
        def _(jj=jj):
            kvb_ref[...] = acc.astype(BF16)
            for cc in range(streams_per_tile):
                c = (jj - _J_KV) * streams_per_tile + cc
                kv_ref[pl.ds(c, tm, stride=_KV_STREAMS), :] = acc[:, cc * HEAD_DIM:(cc + 1) * HEAD_DIM]

    @pl.when(j >= _J_REST)
    def _():
        rest_ref[...] = acc

    @pl.when(j == _J_WIN)
    def _():
        winb_ref[...] = acc.astype(BF16)


def _inproj(x, g, scale3, shift3, w_main, w_gate, tm, rows_per_mod):
    m, d = x.shape
    mod_rows = scale3.shape[1]
    tiles_per_mod = rows_per_mod // tm

    def mod_map(i, j):
        return (i // tiles_per_mod, 0, 0)

    clampj = lambda j, lo, n: jnp.clip(j - lo, 0, n - 1)
    return pl.pallas_call(
        _inproj_kernel,
        grid=(m // tm, _NJ),
        in_specs=[pl.BlockSpec((tm, d), lambda i, j: (i, 0)),
                  pl.BlockSpec((1, d), lambda i, j: (0, 0)),
                  pl.BlockSpec((None, mod_rows, d), mod_map),
                  pl.BlockSpec((None, mod_rows, d), mod_map),
                  pl.BlockSpec((d, _TN), lambda i, j: (0, j)),
                  pl.BlockSpec((d, 2 * LANES), lambda i, j: (0, 0))],
        out_specs=[pl.BlockSpec((tm, _TN), lambda i, j: (i, clampj(j, 0, _J_KV))),
                   pl.BlockSpec((tm * _KV_STREAMS, HEAD_DIM), lambda i, j: (i, 0)),
                   pl.BlockSpec((tm, _TN), lambda i, j: (i, clampj(j, _J_KV, _J_REST - _J_KV))),
                   pl.BlockSpec((tm, _TN), lambda i, j: (i, clampj(j, _J_REST, _NJ - _J_REST))),
                   pl.BlockSpec((tm, _TN), lambda i, j: (i, 0)),
                   pl.BlockSpec((tm, 2 * LANES), lambda i, j: (i, 0))],
        out_shape=[jax.ShapeDtypeStruct((m, Q_COLS), BF16),
                   jax.ShapeDtypeStruct((m * _KV_STREAMS, HEAD_DIM), F32),
                   jax.ShapeDtypeStruct((m, KV_COLS), BF16),
                   jax.ShapeDtypeStruct((m, _REST_COLS), F32),
                   jax.ShapeDtypeStruct((m, WIN_COLS), BF16),
                   jax.ShapeDtypeStruct((m, 2 * LANES), F32)],
        scratch_shapes=[pltpu.VMEM((tm, d), BF16)],
        compiler_params=_params("parallel", "arbitrary"),
        name="inproj",
    )(x, g, scale3, shift3, w_main, w_gate)


def _compress_rows(load_rows, nch, w1, cconst, w2, b2):
    lhs = jnp.concatenate([load_rows(l) for l in range(CMP_STRIDE)], axis=1).astype(BF16)
    y = _dot(lhs, w1)
    second = pltpu.roll(y[:, HEAD_DIM:], shift=nch - 1, axis=0)
    h = _silu(y[:, :HEAD_DIM] + second + cconst)
    out = _dot(h.astype(BF16), w2) + b2
    row = lax.broadcasted_iota(jnp.int32, out.shape, 0)
    return jnp.where(row < nch - 1, out, 0.0)


def _compress_kernel(x_ref, w1_ref, cc_ref, w2_ref, b2_ref, o_ref, *, nch, ncp):
    for c in range(2 * N_KV_HEADS):
        s = c // N_KV_HEADS

        def load_rows(l, c=c):
            return x_ref[pl.ds(l * _KV_STREAMS + c, nch, stride=CMP_STRIDE * _KV_STREAMS), :]

        out = _compress_rows(load_rows, nch, w1_ref[s], cc_ref[s], w2_ref[s], b2_ref[s])
        if ncp > nch:
            out = jnp.concatenate([out, jnp.zeros((ncp - nch, HEAD_DIM), F32)], axis=0)
        o_ref[c] = out.astype(BF16)


def _compress(kv_streams, w1cat, cconst, w2b, b2, ncp):
    b, rows, _ = kv_streams.shape
    t = rows // _KV_STREAMS
    nch = t // CMP_STRIDE
    return pl.pallas_call(
        functools.partial(_compress_kernel, nch=nch, ncp=ncp),
        grid=(b,),
        in_specs=[pl.BlockSpec((None, rows, HEAD_DIM), lambda i: (i, 0, 0)),
                  pl.BlockSpec((2, 16 * HEAD_DIM, 2 * HEAD_DIM), lambda i: (0, 0, 0)),
                  pl.BlockSpec((2, 1, HEAD_DIM), lambda i: (0, 0, 0)),
                  pl.BlockSpec((2, HEAD_DIM, HEAD_DIM), lambda i: (0, 0, 0)),
                  pl.BlockSpec((2, 1, HEAD_DIM), lambda i: (0, 0, 0))],
        out_specs=pl.BlockSpec((None, 4, ncp, HEAD_DIM), lambda i: (i, 0, 0, 0)),
        out_shape=jax.ShapeDtypeStruct((b, 4, ncp, HEAD_DIM), BF16),
        compiler_params=_params("parallel"),
        name="compress",
    )(kv_streams, w1cat, cconst, w2b, b2)


def _cmp_branch(q_heads, kc, vc, qpos, n_cmp):
    rows = q_heads[0].shape[0]
    nio = lax.broadcasted_iota(jnp.int32, (rows, kc.shape[0]), 1)
    mask = (nio < n_cmp) & (nio * CMP_STRIDE + (CMP_BLOCK - 1) <= qpos)
    outs, probs = [], []
    for q in q_heads:
        p = _masked_softmax(_dot_nt(q, kc) * SM_SCALE, mask)
        outs.append(_dot(p.astype(BF16), vc))
        probs.append(p)
    return outs, probs


def _select_mask(psum, ov, qpos, n_sel):
    hi = psum.astype(BF16)
    lo = (psum - hi.astype(F32)).astype(BF16)
    imp = _dot(hi, ov) + _dot(lo, ov)
    rows = imp.shape[0]
    jio = lax.broadcasted_iota(jnp.int32, (rows, LANES), 1)
    allowed = (jio * SEL_BLOCK <= qpos) & (jio < n_sel)
    cur = qpos // SEL_BLOCK
    forced = (jio == 0) | (jio == cur) | (jio == cur - 1)
    score = jnp.where(allowed, imp + jnp.where(forced, FORCE_BONUS, 0.0), NEG_INF)
    rank = jnp.zeros((rows, LANES), F32)
    for i in range(n_sel):
        si = jnp.broadcast_to(score[:, i:i + 1], (rows, LANES))
        tie = jnp.where(jio > i, 1.0, 0.0)
        rank = rank + jnp.where(si > score, 1.0, jnp.where(si == score, tie, 0.0))
    return (rank < float(min(N_SELECT, n_sel))) & allowed


def _select_mask_t(psum, ovt, qpos_row, n_sel):
    hi = psum.astype(BF16)
    lo = (psum - hi.astype(F32)).astype(BF16)
    imp = _dot_nt(ovt, hi) + _dot_nt(ovt, lo)
    sr, rows = imp.shape
    jio = lax.broadcasted_iota(jnp.int32, (sr, rows), 0)
    allowed = (jio * SEL_BLOCK <= qpos_row) & (jio < n_sel)
    cur = qpos_row // SEL_BLOCK
    forced = (jio == 0) | (jio == cur) | (jio == cur - 1)
    score = jnp.where(allowed, imp + jnp.where(forced, FORCE_BONUS, 0.0), NEG_INF)
    rank = jnp.zeros((sr, rows), F32)
    for i in range(n_sel):
        si = jnp.broadcast_to(score[i:i + 1, :], (sr, rows))
        tie = jnp.where(jio > i, 1.0, 0.0)
        rank = rank + jnp.where(si > score, 1.0, jnp.where(si == score, tie, 0.0))
    sel_t = jnp.where((rank < float(min(N_SELECT, n_sel))) & allowed, 1.0, 0.0)
    sel_t = jnp.concatenate([sel_t, jnp.zeros((LANES - sr, rows), F32)], axis=0)
    return sel_t.T


def _biased_softmax(s, bias):
    s = s + bias
    e = jnp.exp(s - jnp.max(s, axis=-1, keepdims=True))
    return e / jnp.maximum(jnp.sum(e, axis=-1, keepdims=True), 1e-30)


def _attn_p_kernel(q_ref, kc_ref, vc_ref, ks_ref, vs_ref, kw_ref, vw_ref, gate_ref, z_ref, ovt_ref, e_ref,
                   o_ref, m_s, l_s, acc_s, *, tq, tk, n_cmp, n_sel):
    qi = pl.program_id(2)
    q0 = qi * tq
    qpos = q0 + lax.broadcasted_iota(jnp.int32, (tq, 1), 0)
    q_heads = [q_ref[:, r * HEAD_DIM:(r + 1) * HEAD_DIM] for r in range(HEADS_PER_GROUP)]

    o_cmp, probs = _cmp_branch(q_heads, kc_ref[...], vc_ref[...], qpos, n_cmp)
    psum = probs[0] + probs[1] + probs[2] + probs[3]
    qpos_row = q0 + lax.broadcasted_iota(jnp.int32, (1, tq), 1)
    selb = _select_mask_t(psum, ovt_ref[...], qpos_row, n_sel).astype(BF16)

    m_s[...] = jnp.full(m_s.shape, NEG_INF, F32)
    l_s[...] = jnp.zeros(l_s.shape, F32)
    acc_s[...] = jnp.zeros(acc_s.shape, F32)
    n_kt = (q0 + tq - 1) // tk + 1

    def sweep(kt, carry):
        k0 = pl.multiple_of(kt * tk, tk)
        k_t = ks_ref[pl.ds(k0, tk), :]
        v_t = vs_ref[pl.ds(k0, tk), :]
        kpos = k0 + lax.broadcasted_iota(jnp.int32, (tq, tk), 1)
        bias = jnp.where((_dot(selb, e_ref[kt]) > 0.5) & (kpos <= qpos), 0.0, NEG_INF)
        for r in range(HEADS_PER_GROUP):
            s = _dot_nt(q_heads[r], k_t) * SM_SCALE + bias
            m_prev = m_s[r]
            m_new = jnp.maximum(m_prev, jnp.max(s, axis=-1, keepdims=True))
            p = jnp.exp(s - jnp.concatenate([m_new] * (tk // LANES), axis=1))
            alpha = jnp.exp(m_prev - m_new)
            l_s[r] = alpha * l_s[r] + jnp.sum(p, axis=-1, keepdims=True)
            acc_s[r] = alpha * acc_s[r] + _dot(p.astype(BF16), v_t)
            m_s[r] = m_new
        return carry

    lax.fori_loop(0, n_kt, sweep, 0)

    w0 = pl.multiple_of(jnp.maximum(q0 - WINDOW, 0), LANES)
    k_w = kw_ref[pl.ds(w0, WINDOW + tq), :]
    v_w = vw_ref[pl.ds(w0, WINDOW + tq), :]
    dist = qpos - (w0 + lax.broadcasted_iota(jnp.int32, (tq, WINDOW + tq), 1))
    wbias = jnp.where((dist >= 0) & (dist <= WINDOW), 0.0, NEG_INF)

    gates = _sigmoid(gate_ref[...])
    for r in range(HEADS_PER_GROUP):
        o_slc = acc_s[r] / jnp.maximum(l_s[r], 1e-30)
        p = _biased_softmax(_dot_nt(q_heads[r], k_w) * SM_SCALE, wbias)
        o_win = _dot(p.astype(BF16), v_w)
        g_of = lambda br: jnp.broadcast_to(
            gates[:, br * HEADS_PER_GROUP + r:br * HEADS_PER_GROUP + r + 1], (tq, HEAD_DIM))
        o = g_of(0) * o_cmp[r] + g_of(1) * o_slc + g_of(2) * o_win
        o_ref[:, r * HEAD_DIM:(r + 1) * HEAD_DIM] = (
            o * _silu(z_ref[:, r * HEAD_DIM:(r + 1) * HEAD_DIM])).astype(BF16)


def _attn_p(q_bf, kcv, kv_bf3, win_bf3, gate, rest, ovt, e_p, b, t, n_cmp, n_sel, tq=128, tk=256):
    assert t >= WINDOW + tq and t % tk == 0 and tq == LANES
    nq = t // tq
    ncp = kcv.shape[2]
    sr = ovt.shape[0]
    gw = HEADS_PER_GROUP * HEAD_DIM
    row = lambda bi, g, qi: bi * nq + qi
    return pl.pallas_call(
        functools.partial(_attn_p_kernel, tq=tq, tk=tk, n_cmp=n_cmp, n_sel=n_sel),
        grid=(b, N_KV_HEADS, nq),
        in_specs=[pl.BlockSpec((tq, gw), lambda bi, g, qi: (row(bi, g, qi), g)),
                  pl.BlockSpec((None, None, ncp, HEAD_DIM), lambda bi, g, qi: (bi, g, 0, 0)),
                  pl.BlockSpec((None, None, ncp, HEAD_DIM), lambda bi, g, qi: (bi, N_KV_HEADS + g, 0, 0)),
                  pl.BlockSpec((None, t, HEAD_DIM), lambda bi, g, qi: (bi, 0, 2 * N_KV_HEADS + g)),
                  pl.BlockSpec((None, t, HEAD_DIM), lambda bi, g, qi: (bi, 0, 3 * N_KV_HEADS + g)),
                  pl.BlockSpec((None, t, HEAD_DIM), lambda bi, g, qi: (bi, 0, g)),
                  pl.BlockSpec((None, t, HEAD_DIM), lambda bi, g, qi: (bi, 0, N_KV_HEADS + g)),
                  pl.BlockSpec((tq, LANES), lambda bi, g, qi: (row(bi, g, qi), g)),
                  pl.BlockSpec((tq, gw), lambda bi, g, qi: (row(bi, g, qi), g)),
                  pl.BlockSpec((sr, ncp), lambda bi, g, qi: (0, 0)),
                  pl.BlockSpec((t // tk, LANES, tk), lambda bi, g, qi: (0, 0, 0))],
        out_specs=pl.BlockSpec((tq, gw), lambda bi, g, qi: (row(bi, g, qi), g)),
        out_shape=jax.ShapeDtypeStruct((b * t, NSA_WIDTH), BF16),
        scratch_shapes=[pltpu.VMEM((HEADS_PER_GROUP, tq, LANES), F32),
                        pltpu.VMEM((HEADS_PER_GROUP, tq, LANES), F32),
                        pltpu.VMEM((HEADS_PER_GROUP, tq, HEAD_DIM), F32)],
        compiler_params=_params("parallel", "parallel", "arbitrary"),
        name="attn_p",
    )(q_bf, kcv, kcv, kv_bf3, kv_bf3, win_bf3, win_bf3, gate, rest, ovt, e_p)


def _attn_s_kernel(pt_ref, cache_ref, wst_ref, q_ref, kvn_ref, winn_ref, gate_ref, z_ref,
                   w1_ref, cc_ref, w2_ref, b2_ref, ov_ref, e_ref, perm_ref, o_ref, wout_ref,
                   kvbuf, wbuf, wnew, kcs, ksl, kwn, sem, wsem, osem,
                   *, n_pages, page, n_cmp, n_sel, pos0, t_new):
    i = pl.program_id(0)
    nb = pl.num_programs(0) - 1
    past = n_pages * page
    nch = past // CMP_STRIDE
    rows = HEADS_PER_GROUP * SAMPLE_ROWS
    n_ws = N_WIN_SLOTS * N_KV_HEADS
    state_rows = WINDOW * n_ws
    new_rows = t_new * n_ws
    prow = page * _KV_STREAMS
    c_seq = jnp.minimum(i, nb - 1)
    a_seq = jnp.maximum(i - 1, 0)
    cs = c_seq % 2
    asl = a_seq % 2

    def page_copy(seq, p, slot):
        return pltpu.make_async_copy(cache_ref.at[pt_ref[seq, p]], kvbuf.at[slot, pl.ds(p * prow, prow)],
                                     sem.at[slot])

    def win_copy(seq, slot):
        return pltpu.make_async_copy(wst_ref.at[seq], wbuf.at[slot], wsem.at[slot])

    def fetch(seq, slot):
        for p in range(n_pages):
            page_copy(seq, p, slot).start()
        win_copy(seq, slot).start()

    def state_copies(seq, slot):
        kept = pltpu.make_async_copy(wst_ref.at[seq, pl.ds(new_rows, state_rows - new_rows)],
                                     wout_ref.at[seq, pl.ds(0, state_rows - new_rows)], osem.at[slot])
        fresh = pltpu.make_async_copy(wnew.at[slot], wout_ref.at[seq, pl.ds(state_rows - new_rows, new_rows)],
                                      osem.at[slot])
        return kept, fresh

    @pl.when(i == 0)
    def _():
        fetch(0, 0)
        kcs[0] = jnp.zeros(kcs.shape[1:], BF16)
        ksl[0] = jnp.zeros(ksl.shape[1:], BF16)
        kwn[0] = jnp.zeros(kwn.shape[1:], BF16)

    @pl.when(i + 1 < nb)
    def _():
        fetch(i + 1, (i + 1) % 2)

    @pl.when(i < nb)
    def _():
        for p in range(n_pages):
            page_copy(i, p, i % 2).wait()
        win_copy(i, i % 2).wait()

    @pl.when(i >= 3)
    def _():
        for cp in state_copies(a_seq - 2, asl):
            cp.wait()

    @pl.when(i >= 1)
    def _():
        for t in range(t_new):
            for c in range(n_ws):
                wnew[asl, pl.ds(t * n_ws + c, 1), :] = winn_ref[pl.ds(t, 1), c * HEAD_DIM:(c + 1) * HEAD_DIM]
        for cp in state_copies(a_seq, asl):
            cp.start()

    qpos = pos0 + lax.broadcasted_iota(jnp.int32, (rows, 1), 0) % SAMPLE_ROWS
    kvn = kvn_ref[...].astype(F32)
    qf = q_ref[...].astype(F32)
    gates = _sigmoid(gate_ref[...])
    zero_tail = jnp.zeros((LANES - 2 * SUBLANES, HEAD_DIM), BF16)
    pad8 = jnp.zeros((SUBLANES, HEAD_DIM), F32)

    def with_new_rows(old_bf, new_f32):
        new16 = jnp.concatenate([new_f32, pad8], axis=0).astype(BF16)
        return jnp.concatenate([old_bf, new16, zero_tail], axis=0)

    def stack_heads(x8):
        return jnp.concatenate([x8] * HEADS_PER_GROUP, axis=0)

    groups = range(N_KV_HEADS)
    nk = past + LANES
    kpos = lax.broadcasted_iota(jnp.int32, (rows, nk), 1)
    wpos = (pos0 - WINDOW) + lax.broadcasted_iota(jnp.int32, (rows, WINDOW + LANES), 1)
    dist = qpos - wpos
    wmask = (dist >= 0) & (dist <= WINDOW) & (wpos >= 0)
    nio = lax.broadcasted_iota(jnp.int32, (rows, nch), 1)
    cmask = (nio < n_cmp) & (nio * CMP_STRIDE + (CMP_BLOCK - 1) <= qpos)

    qgs, s_cmp, s_slc, s_win, v_alls, vw_alls = [], [], [], [], [], []
    for g in groups:
        heads = [qf[:, (g * HEADS_PER_GROUP + r) * HEAD_DIM:(g * HEADS_PER_GROUP + r + 1) * HEAD_DIM]
                 for r in range(HEADS_PER_GROUP)]
        qgs.append(jnp.concatenate(heads, axis=0).astype(BF16))
    for g in groups:
        s_cmp.append(_dot_nt(qgs[g], kcs[asl, g]) * SM_SCALE)
    for g in groups:
        c_k = (2 * N_KV_HEADS + g) * HEAD_DIM
        c_v = (3 * N_KV_HEADS + g) * HEAD_DIM
        k_all = with_new_rows(ksl[asl, g], kvn[:, c_k:c_k + HEAD_DIM])
        v_alls.append(with_new_rows(ksl[asl, N_KV_HEADS + g], kvn[:, c_v:c_v + HEAD_DIM]))
        s_slc.append(_dot_nt(qgs[g], k_all) * SM_SCALE)
    for g in groups:
        wk = g * HEAD_DIM
        wv = (N_KV_HEADS + g) * HEAD_DIM
        kw_all = with_new_rows(kwn[asl, g], winn_ref[:, wk:wk + HEAD_DIM])
        vw_alls.append(with_new_rows(kwn[asl, N_KV_HEADS + g], winn_ref[:, wv:wv + HEAD_DIM]))
        s_win.append(_dot_nt(qgs[g], kw_all) * SM_SCALE)

    vcs = [kcs[asl, N_KV_HEADS + g] for g in groups]

    perm = perm_ref[...]
    blk = LANES
    cpb = blk // CMP_STRIDE
    stream = lambda c: kvbuf[cs, pl.ds(c, past, stride=_KV_STREAMS), :].astype(BF16)

    def chunk_rows(c):
        xs = stream(c)
        out = []
        for k in range(past // blk):
            y = _dot(perm, xs[k * blk:(k + 1) * blk])
            out.append(jnp.concatenate([y[l * cpb:(l + 1) * cpb] for l in range(CMP_STRIDE)], axis=1))
        return jnp.concatenate(out, axis=0).astype(BF16)

    def slot_chunks(s):
        return jnp.concatenate([chunk_rows(s * N_KV_HEADS + g) for g in groups], axis=0)

    def compress_slot(s, lhs):
        y = _dot(lhs, w1_ref[s])
        second = pltpu.roll(y[:, HEAD_DIM:], shift=lhs.shape[0] - 1, axis=0)
        h = _silu(y[:, :HEAD_DIM] + second + cc_ref[s])
        out = _dot(h.astype(BF16), w2_ref[s]) + b2_ref[s]
        row = lax.broadcasted_iota(jnp.int32, out.shape, 0) % nch
        out = jnp.where(row < nch - 1, out, 0.0).astype(BF16)
        for g in groups:
            kcs[cs, s * N_KV_HEADS + g] = out[g * nch:(g + 1) * nch]

    lhs0 = slot_chunks(0)

    p_cmps = [_masked_softmax(s_cmp[g], cmask) for g in groups]
    psums = [(p[0:8] + p[8:16]) + (p[16:24] + p[24:32]) for p in p_cmps]
    sels = [_select_mask(psums[g], ov_ref[...], qpos[0:SAMPLE_ROWS], n_sel) for g in groups]
    selbs = [jnp.where(s, 1.0, 0.0).astype(BF16) for s in sels]

    compress_slot(0, lhs0)
    lhs1 = slot_chunks(1)

    smasks = [(stack_heads(_dot(selbs[g], e_ref[...])) > 0.5) & (kpos <= qpos) for g in groups]
    o_cmps = [_dot(p_cmps[g].astype(BF16), vcs[g]) for g in groups]
    p_wins = [_masked_softmax(s_win[g], wmask) for g in groups]
    o_wins = [_dot(p_wins[g].astype(BF16), vw_alls[g]) for g in groups]
    p_slcs = [_masked_softmax(s_slc[g], smasks[g]) for g in groups]
    o_slcs = [_dot(p_slcs[g].astype(BF16), v_alls[g]) for g in groups]

    compress_slot(1, lhs1)
    for k in range(2 * N_KV_HEADS):
        ksl[cs, k] = stream(2 * N_KV_HEADS + k)
    for k in range(n_ws):
        kwn[cs, k] = wbuf[cs, pl.ds(k, WINDOW, stride=n_ws), :].astype(BF16)

    for g in groups:
        o_cmp, o_slc, o_win = o_cmps[g], o_slcs[g], o_wins[g]
        for r in range(HEADS_PER_GROUP):
            h = g * HEADS_PER_GROUP + r
            rs = slice(r * SAMPLE_ROWS, (r + 1) * SAMPLE_ROWS)
            g_of = lambda br: jnp.broadcast_to(
                gates[:, g * LANES + br * HEADS_PER_GROUP + r:g * LANES + br * HEADS_PER_GROUP + r + 1],
                (SAMPLE_ROWS, HEAD_DIM))
            o = g_of(0) * o_cmp[rs] + g_of(1) * o_slc[rs] + g_of(2) * o_win[rs]
            o_ref[:, h * HEAD_DIM:(h + 1) * HEAD_DIM] = (
                o * _silu(z_ref[:, h * HEAD_DIM:(h + 1) * HEAD_DIM])).astype(BF16)

    @pl.when((i == nb) & (i >= 2))
    def _():
        for cp in state_copies(a_seq - 1, 1 - asl):
            cp.wait()

    @pl.when((i == nb) & (i >= 1))
    def _():
        for cp in state_copies(a_seq, asl):
            cp.wait()


def _attn_s(page_table, cache3, wst3, q3, kvn3, rest3, gate3, w1cat, cconst, w2b, b2, ov, e_s,
            n_cmp, n_sel, pos0, t_new):
    nb, n_pages = page_table.shape
    page = cache3.shape[1] // _KV_STREAMS
    past = n_pages * page
    nch = past // CMP_STRIDE
    n_ws = N_WIN_SLOTS * N_KV_HEADS
    assert wst3.shape[1:] == (WINDOW * n_ws, HEAD_DIM) and past % LANES == 0
    ncp = ov.shape[0]
    prev = lambda i: jnp.maximum(i - 1, 0)
    grid_spec = pltpu.PrefetchScalarGridSpec(
        num_scalar_prefetch=1,
        grid=(nb + 1,),
        in_specs=[pl.BlockSpec(memory_space=pl.ANY),
                  pl.BlockSpec(memory_space=pl.ANY),
                  pl.BlockSpec((None, SAMPLE_ROWS, Q_COLS), lambda i, pt: (prev(i), 0, 0)),
                  pl.BlockSpec((None, SAMPLE_ROWS, KV_COLS), lambda i, pt: (prev(i), 0, 0)),
                  pl.BlockSpec((None, SAMPLE_ROWS, WIN_COLS), lambda i, pt: (prev(i), 0, _REST_WIN)),
                  pl.BlockSpec((None, SAMPLE_ROWS, 2 * LANES), lambda i, pt: (prev(i), 0, 0)),
                  pl.BlockSpec((None, SAMPLE_ROWS, NSA_WIDTH), lambda i, pt: (prev(i), 0, _REST_Z)),
                  pl.BlockSpec((2, 16 * HEAD_DIM, 2 * HEAD_DIM), lambda i, pt: (0, 0, 0)),
                  pl.BlockSpec((2, 1, HEAD_DIM), lambda i, pt: (0, 0, 0)),
                  pl.BlockSpec((2, HEAD_DIM, HEAD_DIM), lambda i, pt: (0, 0, 0)),
                  pl.BlockSpec((2, 1, HEAD_DIM), lambda i, pt: (0, 0, 0)),
                  pl.BlockSpec((ncp, LANES), lambda i, pt: (0, 0)),
                  pl.BlockSpec((LANES, past + LANES), lambda i, pt: (0, 0)),
                  pl.BlockSpec((LANES, LANES), lambda i, pt: (0, 0))],
        out_specs=[pl.BlockSpec((None, SAMPLE_ROWS, NSA_WIDTH), lambda i, pt: (prev(i), 0, 0)),
                   pl.BlockSpec(memory_space=pl.ANY)],
        scratch_shapes=[pltpu.VMEM((2, past * _KV_STREAMS, HEAD_DIM), F32),
                        pltpu.VMEM((2, WINDOW * n_ws, HEAD_DIM), F32),
                        pltpu.VMEM((2, t_new * n_ws, HEAD_DIM), F32),
                        pltpu.VMEM((2, 2 * N_KV_HEADS, nch, HEAD_DIM), BF16),
                        pltpu.VMEM((2, 2 * N_KV_HEADS, past, HEAD_DIM), BF16),
                        pltpu.VMEM((2, n_ws, WINDOW, HEAD_DIM), BF16),
                        pltpu.SemaphoreType.DMA((2,)),
                        pltpu.SemaphoreType.DMA((2,)),
                        pltpu.SemaphoreType.DMA((2,))],
    )
    return pl.pallas_call(
        functools.partial(_attn_s_kernel, n_pages=n_pages, page=page, n_cmp=n_cmp, n_sel=n_sel, pos0=pos0,
                          t_new=t_new),
        grid_spec=grid_spec,
        out_shape=[jax.ShapeDtypeStruct((nb, SAMPLE_ROWS, NSA_WIDTH), BF16),
                   jax.ShapeDtypeStruct((nb, WINDOW * n_ws, HEAD_DIM), F32)],
        compiler_params=_params("arbitrary"),
        name="attn_s",
    )(page_table, cache3, wst3, q3, kvn3, rest3, gate3, rest3, w1cat, cconst, w2b, b2, ov, e_s,
      _chunk_permutation())


_CONV_RC = 64
_CONV_LC = 256


def _conv_kernel(a_ref, gl_ref, ah_ref, glh_ref, init_ref, zc_ref, w_ref, db_ref, lg_ref, lb_ref,
                 cv_ref, st_ref, uext, ybuf, *, tt, t_valid):
    ti = pl.program_id(1)
    nt = pl.num_programs(1)
    halo = ah_ref[...] * _sigmoid(glh_ref[...])
    uext[0:CONV_HALO] = jnp.where(ti == 0, init_ref[...], halo)
    uext[CONV_HALO:CONV_HALO + tt] = a_ref[...] * _sigmoid(gl_ref[...])

    off = CONV_HALO - (CONV_WIDTH - 1)
    rc = min(_CONV_RC, tt)
    for r0 in range(0, tt, rc):
        for c0 in range(0, CONV_DIM, _CONV_LC):
            acc = jnp.broadcast_to(db_ref[:, c0:c0 + _CONV_LC], (rc, _CONV_LC))
            for k in range(CONV_WIDTH):
                acc = acc + w_ref[k:k + 1, c0:c0 + _CONV_LC] * uext[pl.ds(r0 + off + k, rc), c0:c0 + _CONV_LC]
            ybuf[r0:r0 + rc, c0:c0 + _CONV_LC] = acc

    y = ybuf[...]
    mu = jnp.mean(y, axis=-1, keepdims=True)
    var = jnp.mean(jnp.square(y - mu), axis=-1, keepdims=True)
    ln = (y - mu) * lax.rsqrt(var + NORM_EPS) * lg_ref[...] + lb_ref[...]
    cv_ref[...] = (_silu(ln) * _silu(zc_ref[...])).astype(BF16)

    @pl.when(ti == nt - 1)
    def _():
        st_ref[...] = uext[pl.ds(t_valid, CONV_HALO), :]


def _conv(rest, u_init, w_pad, db, lg, lb, b, t, tt, t_valid):
    nt = t // tt
    hb = tt // CONV_HALO if tt >= CONV_HALO else None
    row = lambda bi, ti: bi * nt + ti
    if hb is None:
        assert nt == 1
        halo_map = lambda col: (lambda bi, ti: (0, col))
        halo_rows = tt
    else:
        halo_map = lambda col: (lambda bi, ti: (jnp.maximum(row(bi, ti) * hb - 1, 0), col))
        halo_rows = CONV_HALO
    vec = lambda: pl.BlockSpec((1, CONV_DIM), lambda bi, ti: (0, 0))
    kern = functools.partial(_conv_kernel, tt=tt, t_valid=t_valid)
    if hb is None:
        kern = functools.partial(_conv_kernel_nohalo, tt=tt, t_valid=t_valid)
    return pl.pallas_call(
        kern,
        grid=(b, nt),
        in_specs=[pl.BlockSpec((tt, CONV_DIM), lambda bi, ti: (row(bi, ti), _REST_A)),
                  pl.BlockSpec((tt, CONV_DIM), lambda bi, ti: (row(bi, ti), _REST_GL)),
                  pl.BlockSpec((halo_rows, CONV_DIM), halo_map(_REST_A)),
                  pl.BlockSpec((halo_rows, CONV_DIM), halo_map(_REST_GL)),
                  pl.BlockSpec((None, CONV_HALO, CONV_DIM), lambda bi, ti: (bi, 0, 0)),
                  pl.BlockSpec((tt, CONV_DIM), lambda bi, ti: (row(bi, ti), _REST_ZC)),
                  pl.BlockSpec((CONV_HALO, CONV_DIM), lambda bi, ti: (0, 0)),
                  vec(), vec(), vec()],
        out_specs=[pl.BlockSpec((tt, CONV_DIM), lambda bi, ti: (row(bi, ti), 0)),
                   pl.BlockSpec((None, CONV_HALO, CONV_DIM), lambda bi, ti: (bi, 0, 0))],
        out_shape=[jax.ShapeDtypeStruct((b * t, CONV_DIM), BF16),
                   jax.ShapeDtypeStruct((b, CONV_HALO, CONV_DIM), F32)],
        scratch_shapes=[pltpu.VMEM((CONV_HALO + tt, CONV_DIM), F32),
                        pltpu.VMEM((tt, CONV_DIM), F32)],
        compiler_params=_params("parallel", "arbitrary"),
        name="conv",
    )(rest, rest, rest, rest, u_init, rest, w_pad, db, lg, lb)


def _conv_kernel_nohalo(a_ref, gl_ref, ah_ref, glh_ref, init_ref, zc_ref, w_ref, db_ref, lg_ref, lb_ref,
                        cv_ref, st_ref, uext, ybuf, *, tt, t_valid):
    del ah_ref, glh_ref
    uext[0:CONV_HALO] = init_ref[...]
    uext[CONV_HALO:CONV_HALO + tt] = a_ref[...] * _sigmoid(gl_ref[...])
    off = CONV_HALO - (CONV_WIDTH - 1)
    for c0 in range(0, CONV_DIM, _CONV_LC):
        acc = jnp.broadcast_to(db_ref[:, c0:c0 + _CONV_LC], (tt, _CONV_LC))
        for k in range(CONV_WIDTH):
            acc = acc + w_ref[k:k + 1, c0:c0 + _CONV_LC] * uext[pl.ds(off + k, tt), c0:c0 + _CONV_LC]
        ybuf[:, c0:c0 + _CONV_LC] = acc
    y = ybuf[...]
    mu = jnp.mean(y, axis=-1, keepdims=True)
    var = jnp.mean(jnp.square(y - mu), axis=-1, keepdims=True)
    ln = (y - mu) * lax.rsqrt(var + NORM_EPS) * lg_ref[...] + lb_ref[...]
    cv_ref[...] = (_silu(ln) * _silu(zc_ref[...])).astype(BF16)
    st_ref[...] = uext[pl.ds(t_valid, CONV_HALO), :]


def _outproj_kernel(on_ref, cv_ref, w_ref, x_ref, gate_ref, g_ref, y_ref):
    mix = _dot(on_ref[...], w_ref[0:NSA_WIDTH, :]) + _dot(cv_ref[...], w_ref[NSA_WIDTH:, :])
    nrm = mix * lax.rsqrt(jnp.mean(mix * mix, axis=-1, keepdims=True) + NORM_EPS) * g_ref[...]
    y_ref[...] = x_ref[...] + gate_ref[...] * nrm


def _outproj(o_nsa, cv, w_out_b, x, gate3, g_post, tm, rows_per_mod):
    m, d = x.shape
    mod_rows = gate3.shape[1]
    tiles_per_mod = rows_per_mod // tm
    return pl.pallas_call(
        _outproj_kernel,
        grid=(m // tm,),
        in_specs=[pl.BlockSpec((tm, NSA_WIDTH), lambda i: (i, 0)),
                  pl.BlockSpec((tm, CONV_DIM), lambda i: (i, 0)),
                  pl.BlockSpec((d, d), lambda i: (0, 0)),
                  pl.BlockSpec((tm, d), lambda i: (i, 0)),
                  pl.BlockSpec((None, mod_rows, d), lambda i: (i // tiles_per_mod, 0, 0)),
                  pl.BlockSpec((1, d), lambda i: (0, 0))],
        out_specs=pl.BlockSpec((tm, d), lambda i: (i, 0)),
        out_shape=jax.ShapeDtypeStruct((m, d), F32),
        compiler_params=_params("parallel"),
        name="outproj",
    )(o_nsa, cv, w_out_b, x, gate3, g_post)


def _overlap_matrix(n_cmp, n_sel, ncp):
    ci = np.arange(ncp)[:, None] * CMP_STRIDE
    sj = np.arange(LANES)[None, :] * SEL_BLOCK
    ov = (ci < sj + SEL_BLOCK) & (ci + CMP_BLOCK > sj)
    ov &= (np.arange(ncp)[:, None] < n_cmp) & (np.arange(LANES)[None, :] < n_sel)
    return jnp.asarray(ov.astype(np.float32), dtype=BF16)


def _overlap_matrix_t(n_cmp, n_sel, ncp):
    sr = _round_up(n_sel, 2 * SUBLANES)
    return _overlap_matrix(n_cmp, n_sel, ncp).T[:sr]


def _chunk_permutation():
    p = np.zeros((LANES, LANES), np.float32)
    for j in range(LANES // CMP_STRIDE):
        for l in range(CMP_STRIDE):
            p[l * (LANES // CMP_STRIDE) + j, CMP_STRIDE * j + l] = 1.0
    return jnp.asarray(p, dtype=BF16)


def _expand_matrix(n_keys):
    e = (np.arange(n_keys)[None, :] // SEL_BLOCK) == np.arange(LANES)[:, None]
    return jnp.asarray(e.astype(np.float32), dtype=BF16)


def _round_up(x, m):
    return -(-x // m) * m


def kernel(x_prompt, x_sample, c_prompt, c_sample, cache_kv, page_table, state_win_kv, state_conv, w_ada, b_ada, norm_pre, norm_post, w_in, cmp_pe, cmp_w1, cmp_b1, cmp_w2, cmp_b2, conv_dw, conv_db, conv_ln_g, conv_ln_b, w_out):
    assert w_ada.shape[0] == 1, "single-layer trunk"
    bp, tp, d = x_prompt.shape
    bs, ts, _ = x_sample.shape
    n_phys, page = cache_kv.shape[1], cache_kv.shape[2]
    n_pages = page_table.shape[1]
    past = n_pages * page
    assert ts <= SAMPLE_ROWS and d == D_MODEL

    w = w_in[0]
    c_q, c_kv, c_win, c_gate = 0, Q_COLS, Q_COLS + KV_COLS, Q_COLS + KV_COLS + WIN_COLS
    c_rest = c_gate + GATE_COLS
    w_main = jnp.concatenate([w[:, c_q:c_win], w[:, c_rest:], w[:, c_win:c_gate]], axis=1).astype(BF16)
    wg = w[:, c_gate:c_rest].reshape(d, N_BRANCH, N_KV_HEADS, HEADS_PER_GROUP).transpose(0, 2, 1, 3)
    wg = wg.reshape(d, N_KV_HEADS, N_BRANCH * HEADS_PER_GROUP)
    w_gate = jnp.pad(wg, ((0, 0), (0, 0), (0, LANES - N_BRANCH * HEADS_PER_GROUP))).reshape(d, 2 * LANES).astype(BF16)
    w_out_b = w_out[0].astype(BF16)
    w1 = cmp_w1[0]
    w1cat = jnp.concatenate([w1[:, :CMP_STRIDE].reshape(2, CMP_STRIDE * HEAD_DIM, HEAD_DIM),
                             w1[:, CMP_STRIDE:].reshape(2, CMP_STRIDE * HEAD_DIM, HEAD_DIM)], axis=2).astype(BF16)
    w2b = cmp_w2[0].astype(BF16)
    pe_rows = jnp.pad(cmp_pe[0].reshape(2, 2, CMP_STRIDE * HEAD_DIM), ((0, 0), (0, SUBLANES - 2), (0, 0)))
    b1 = cmp_b1[0].reshape(2, 1, HEAD_DIM)
    b2 = cmp_b2[0].reshape(2, 1, HEAD_DIM)
    conv_w_pad = jnp.pad(conv_dw[0], ((0, CONV_HALO - CONV_WIDTH), (0, 0)))
    db = conv_db[0].reshape(1, CONV_DIM)
    lg = conv_ln_g[0].reshape(1, CONV_DIM)
    lb = conv_ln_b[0].reshape(1, CONV_DIM)
    g_pre = norm_pre[0].reshape(1, d)
    g_post = norm_post[0].reshape(1, d)

    n_c = bp + bs
    c_all = jnp.pad(jnp.concatenate([c_prompt, c_sample], axis=0), ((0, _round_up(n_c, SUBLANES) - n_c), (0, 0)))
    mod = _ada(c_all, w_ada[0], b_ada[0])
    shift, scale, gate = mod[:, :d], mod[:, d:2 * d], mod[:, 2 * d:]
    cconst = _cmpconst(pe_rows, w1cat, b1)

    tm_p = 512
    xp = x_prompt.reshape(bp * tp, d)
    q_p, kv_p, kvb_p, rest_p, winb_p, gate_p = _inproj(
        xp, g_pre, scale[:bp].reshape(bp, 1, d), shift[:bp].reshape(bp, 1, d), w_main, w_gate, tm_p, tp)
    n_cmp_p = (tp - CMP_BLOCK) // CMP_STRIDE + 1
    n_sel_p = -(-tp // SEL_BLOCK)
    ncp_p = _round_up(tp // CMP_STRIDE, LANES)
    kcv_p = _compress(kv_p.reshape(bp, tp * _KV_STREAMS, HEAD_DIM), w1cat, cconst, w2b, b2, ncp_p)
    tk = 256
    e_p = _expand_matrix(tp).reshape(LANES, tp // tk, tk).transpose(1, 0, 2)
    o_nsa_p = _attn_p(q_p, kcv_p, kvb_p.reshape(bp, tp, KV_COLS), winb_p.reshape(bp, tp, WIN_COLS), gate_p, rest_p,
                      _overlap_matrix_t(n_cmp_p, n_sel_p, ncp_p), e_p, bp, tp, n_cmp_p, n_sel_p, tk=tk)
    cv_p, st_p = _conv(rest_p, jnp.zeros((bp, CONV_HALO, CONV_DIM), F32), conv_w_pad, db, lg, lb, bp, tp, 128, 128)
    y_p = _outproj(o_nsa_p, cv_p, w_out_b, xp, gate[:bp].reshape(bp, 1, d), g_post, 256, tp)

    r = SAMPLE_ROWS
    xs = jnp.pad(x_sample, ((0, 0), (0, r - ts), (0, 0))).reshape(bs * r, d)
    per_row = lambda v: jnp.repeat(v[bp:bp + bs], r, axis=0)
    tm_s = min(512, bs * r)
    n_mod = (bs * r) // tm_s
    q_s, kv_s, kvb_s, rest_s, _, gate_s = _inproj(
        xs, g_pre, per_row(scale).reshape(n_mod, tm_s, d), per_row(shift).reshape(n_mod, tm_s, d),
        w_main, w_gate, tm_s, tm_s)
    total = past + ts
    n_cmp_s = (total - CMP_BLOCK) // CMP_STRIDE + 1
    n_sel_s = -(-total // SEL_BLOCK)
    assert n_cmp_s <= past // CMP_STRIDE - 1 + 1 and (n_cmp_s - 1) * CMP_STRIDE + CMP_BLOCK <= past
    ncp_s = _round_up(past // CMP_STRIDE, LANES)
    assert ncp_s == past // CMP_STRIDE
    n_ws = N_WIN_SLOTS * N_KV_HEADS
    o_nsa_s, win_state_s = _attn_s(
        page_table, cache_kv.reshape(n_phys, page * _KV_STREAMS, HEAD_DIM),
        state_win_kv.reshape(bs, WINDOW * n_ws, HEAD_DIM),
        q_s.reshape(bs, r, Q_COLS), kvb_s.reshape(bs, r, KV_COLS), rest_s.reshape(bs, r, _REST_COLS),
        gate_s.reshape(bs, r, 2 * LANES), w1cat, cconst, w2b, b2,
        _overlap_matrix(n_cmp_s, n_sel_s, ncp_s), _expand_matrix(past + LANES),
        n_cmp_s, n_sel_s, past, ts)
    u_init_s = jnp.pad(state_conv[0], ((0, 0), (CONV_HALO - (CONV_WIDTH - 1), 0), (0, 0)))
    cv_s, st_s = _conv(rest_s, u_init_s, conv_w_pad, db, lg, lb, bs, r, r, ts)
    y_s = _outproj(o_nsa_s.reshape(bs * r, NSA_WIDTH), cv_s, w_out_b, xs, per_row(gate).reshape(n_mod, tm_s, d),
                   g_post, tm_s, tm_s)

    keep = CONV_WIDTH - 1
    y_prompt = y_p.reshape(bp, tp, d)
    y_sample = y_s.reshape(bs, r, d)[:, :ts]
    kv_rows_prompt = kv_p.reshape(1, bp, tp, N_CACHE_SLOTS, N_KV_HEADS, HEAD_DIM)
    kv_rows_sample = kv_s.reshape(1, bs, r, N_CACHE_SLOTS, N_KV_HEADS, HEAD_DIM)[:, :, :ts]
    win_p = rest_p.reshape(bp, tp, _REST_COLS)[:, tp - min(WINDOW, tp):, _REST_WIN * WIN_COLS:]
    win_prompt = win_p.reshape(1, bp, min(WINDOW, tp), N_WIN_SLOTS, N_KV_HEADS, HEAD_DIM)
    win_sample = win_state_s.reshape(1, bs, WINDOW, N_WIN_SLOTS, N_KV_HEADS, HEAD_DIM)
    conv_prompt = st_p[:, CONV_HALO - keep:].reshape(1, bp, keep, CONV_DIM)
    conv_sample = st_s[:, CONV_HALO - keep:].reshape(1, bs, keep, CONV_DIM)
    return (y_prompt, y_sample, kv_rows_prompt, kv_rows_sample, win_prompt, win_sample, conv_prompt, conv_sample)
```

```python
import functools

import numpy as np
import jax
import jax.numpy as jnp
from jax import lax
from jax.experimental import pallas as pl
from jax.experimental.pallas import tpu as pltpu

D_MODEL = 2048
HEAD_DIM = 128
N_HEADS = 8
N_KV_HEADS = 2
HEADS_PER_GROUP = N_HEADS // N_KV_HEADS
NSA_WIDTH = N_HEADS * HEAD_DIM
CONV_DIM = D_MODEL - NSA_WIDTH
CMP_BLOCK = 32
CMP_STRIDE = 16
SEL_BLOCK = 64
N_SELECT = 16
WINDOW = 512
CONV_WIDTH = 31
N_CACHE_SLOTS = 4
N_WIN_SLOTS = 2
N_BRANCH = 3
Q_COLS = N_HEADS * HEAD_DIM
KV_COLS = N_CACHE_SLOTS * N_KV_HEADS * HEAD_DIM
WIN_COLS = N_WIN_SLOTS * N_KV_HEADS * HEAD_DIM
GATE_COLS = N_BRANCH * N_HEADS
_KV_STREAMS = N_CACHE_SLOTS * N_KV_HEADS
NORM_EPS = 1e-6
NEG_INF = -1e30
FORCE_BONUS = 1e6
SM_SCALE = HEAD_DIM ** -0.5

LANES = 128
SUBLANES = 8
CONV_HALO = 32
SAMPLE_ROWS = 8
VMEM_LIMIT = 56 * 1024 * 1024

F32 = jnp.float32
BF16 = jnp.bfloat16


def _sigmoid(x):
    return 1.0 / (1.0 + jnp.exp(-x))


def _silu(x):
    return x * _sigmoid(x)


def _dot(a, b):
    return jnp.dot(a, b, preferred_element_type=F32)


def _dot_nt(a, b):
    return lax.dot_general(a, b, (((1,), (1,)), ((), ())), preferred_element_type=F32)


def _masked_softmax(s, mask):
    s = jnp.where(mask, s, NEG_INF)
    e = jnp.where(mask, jnp.exp(s - jnp.max(s, axis=-1, keepdims=True)), 0.0)
    return e / jnp.maximum(jnp.sum(e, axis=-1, keepdims=True), 1e-30)


def _params(*sem):
    return pltpu.CompilerParams(dimension_semantics=sem, vmem_limit_bytes=VMEM_LIMIT)


def _ada_kernel(c_ref, w_ref, b_ref, o_ref):
    o_ref[...] = _dot(_silu(c_ref[...]).astype(BF16), w_ref[...].astype(BF16)) + b_ref[...]


def _ada(c_all, w_ada, b_ada, tn=512):
    rows, d = c_all.shape
    n = w_ada.shape[1]
    return pl.pallas_call(
        _ada_kernel,
        grid=(n // tn,),
        in_specs=[pl.BlockSpec((rows, d), lambda j: (0, 0)),
                  pl.BlockSpec((d, tn), lambda j: (0, j)),
                  pl.BlockSpec((1, tn), lambda j: (0, j))],
        out_specs=pl.BlockSpec((rows, tn), lambda j: (0, j)),
        out_shape=jax.ShapeDtypeStruct((rows, n), F32),
        compiler_params=_params("parallel"),
        name="ada",
    )(c_all, w_ada, b_ada.reshape(1, n))


def _cmpconst_kernel(pe_ref, w1_ref, b1_ref, o_ref):
    y = _dot(pe_ref[...].astype(BF16), w1_ref[...])
    o_ref[...] = y[0:1, :HEAD_DIM] + y[1:2, HEAD_DIM:] + b1_ref[...]


def _cmpconst(pe_rows, w1cat, b1):
    return pl.pallas_call(
        _cmpconst_kernel,
        grid=(2,),
        in_specs=[pl.BlockSpec((None, SUBLANES, 16 * HEAD_DIM), lambda s: (s, 0, 0)),
                  pl.BlockSpec((None, 16 * HEAD_DIM, 2 * HEAD_DIM), lambda s: (s, 0, 0)),
                  pl.BlockSpec((None, 1, HEAD_DIM), lambda s: (s, 0, 0))],
        out_specs=pl.BlockSpec((None, 1, HEAD_DIM), lambda s: (s, 0, 0)),
        out_shape=jax.ShapeDtypeStruct((2, 1, HEAD_DIM), F32),
        compiler_params=_params("parallel"),
        name="cmpconst",
    )(pe_rows, w1cat, b1)


_TN = 512
_J_KV = Q_COLS // _TN
_J_REST = (Q_COLS + KV_COLS) // _TN
_REST_COLS = NSA_WIDTH + 3 * CONV_DIM + WIN_COLS
_N_MAIN = Q_COLS + KV_COLS + _REST_COLS
_J_WIN = (_N_MAIN - WIN_COLS) // _TN
_NJ = _N_MAIN // _TN
_REST_Z = 0
_REST_A = 1
_REST_GL = 2
_REST_ZC = 3
_REST_WIN = (NSA_WIDTH + 3 * CONV_DIM) // WIN_COLS


def _inproj_kernel(x_ref, g_ref, sc_ref, sh_ref, w_ref, wg_ref,
                   q_ref, kv_ref, kvb_ref, rest_ref, winb_ref, gate_ref, h_ref):
    j = pl.program_id(1)

    @pl.when(j == 0)
    def _():
        x = x_ref[...]
        y = x * lax.rsqrt(jnp.mean(x * x, axis=-1, keepdims=True) + NORM_EPS) * g_ref[...]
        h = (y * (1.0 + sc_ref[...]) + sh_ref[...]).astype(BF16)
        h_ref[...] = h
        gate_ref[...] = _dot(h, wg_ref[...])

    acc = _dot(h_ref[...], w_ref[...])

    @pl.when(j < _J_KV)
    def _():
        q_ref[...] = acc.astype(BF16)

    tm = acc.shape[0]
    streams_per_tile = _TN // HEAD_DIM
    for jj in range(_J_KV, _J_REST):
        @pl.when(j == jj)
        def _(jj=jj):
            kvb_ref[...] = acc.astype(BF16)
            for cc in range(streams_per_tile):
                c = (jj - _J_KV) * streams_per_tile + cc
                kv_ref[pl.ds(c, tm, stride=_KV_STREAMS), :] = acc[:, cc * HEAD_DIM:(cc + 1) * HEAD_DIM]

    @pl.when(j >= _J_REST)
    def _():
        rest_ref[...] = acc

    @pl.when(j == _J_WIN)
    def _():
        winb_ref[...] = acc.astype(BF16)


def _inproj(x, g, scale3, shift3, w_main, w_gate, tm, rows_per_mod):
    m, d = x.shape
    mod_rows = scale3.shape[1]
    tiles_per_mod = rows_per_mod // tm

    def mod_map(i, j):
        return (i // tiles_per_mod, 0, 0)

    clampj = lambda j, lo, n: jnp.clip(j - lo, 0, n - 1)
    return pl.pallas_call(
        _inproj_kernel,
        grid=(m // tm, _NJ),
        in_specs=[pl.BlockSpec((tm, d), lambda i, j: (i, 0)),
                  pl.BlockSpec((1, d), lambda i, j: (0, 0)),
                  pl.BlockSpec((None, mod_rows, d), mod_map),
                  pl.BlockSpec((None, mod_rows, d), mod_map),
                  pl.BlockSpec((d, _TN), lambda i, j: (0, j)),
                  pl.BlockSpec((d, 2 * LANES), lambda i, j: (0, 0))],
        out_specs=[pl.BlockSpec((tm, _TN), lambda i, j: (i, clampj(j, 0, _J_KV))),
                   pl.BlockSpec((tm * _KV_STREAMS, HEAD_DIM), lambda i, j: (i, 0)),
                   pl.BlockSpec((tm, _TN), lambda i, j: (i, clampj(j, _J_KV, _J_REST - _J_KV))),
                   pl.BlockSpec((tm, _TN), lambda i, j: (i, clampj(j, _J_REST, _NJ - _J_REST))),
                   pl.BlockSpec((tm, _TN), lambda i, j: (i, 0)),
                   pl.BlockSpec((tm, 2 * LANES), lambda i, j: (i, 0))],
        out_shape=[jax.ShapeDtypeStruct((m, Q_COLS), BF16),
                   jax.ShapeDtypeStruct((m * _KV_STREAMS, HEAD_DIM), F32),
                   jax.ShapeDtypeStruct((m, KV_COLS), BF16),
                   jax.ShapeDtypeStruct((m, _REST_COLS), F32),
                   jax.ShapeDtypeStruct((m, WIN_COLS), BF16),
                   jax.ShapeDtypeStruct((m, 2 * LANES), F32)],
        scratch_shapes=[pltpu.VMEM((tm, d), BF16)],
        compiler_params=_params("parallel", "arbitrary"),
        name="inproj",
    )(x, g, scale3, shift3, w_main, w_gate)


def _compress_rows(load_rows, nch, w1, cconst, w2, b2):
    lhs = jnp.concatenate([load_rows(l) for l in range(CMP_STRIDE)], axis=1).astype(BF16)
    y = _dot(lhs, w1)
    second = pltpu.roll(y[:, HEAD_DIM:], shift=nch - 1, axis=0)
    h = _silu(y[:, :HEAD_DIM] + second + cconst)
    out = _dot(h.astype(BF16), w2) + b2
    row = lax.broadcasted_iota(jnp.int32, out.shape, 0)
    return jnp.where(row < nch - 1, out, 0.0)


def _compress_kernel(x_ref, w1_ref, cc_ref, w2_ref, b2_ref, o_ref, *, nch, ncp):
    for c in range(2 * N_KV_HEADS):
        s = c // N_KV_HEADS

        def load_rows(l, c=c):
            return x_ref[pl.ds(l * _KV_STREAMS + c, nch, stride=CMP_STRIDE * _KV_STREAMS), :]

        out = _compress_rows(load_rows, nch, w1_ref[s], cc_ref[s], w2_ref[s], b2_ref[s])
        if ncp > nch:
            out = jnp.concatenate([out, jnp.zeros((ncp - nch, HEAD_DIM), F32)], axis=0)
        o_ref[c] = out.astype(BF16)


def _compress(kv_streams, w1cat, cconst, w2b, b2, ncp):
    b, rows, _ = kv_streams.shape
    t = rows // _KV_STREAMS
    nch = t // CMP_STRIDE
    return pl.pallas_call(
        functools.partial(_compress_kernel, nch=nch, ncp=ncp),
        grid=(b,),
        in_specs=[pl.BlockSpec((None, rows, HEAD_DIM), lambda i: (i, 0, 0)),
                  pl.BlockSpec((2, 16 * HEAD_DIM, 2 * HEAD_DIM), lambda i: (0, 0, 0)),
                  pl.BlockSpec((2, 1, HEAD_DIM), lambda i: (0, 0, 0)),
                  pl.BlockSpec((2, HEAD_DIM, HEAD_DIM), lambda i: (0, 0, 0)),
                  pl.BlockSpec((2, 1, HEAD_DIM), lambda i: (0, 0, 0))],
        out_specs=pl.BlockSpec((None, 4, ncp, HEAD_DIM), lambda i: (i, 0, 0, 0)),
        out_shape=jax.ShapeDtypeStruct((b, 4, ncp, HEAD_DIM), BF16),
        compiler_params=_params("parallel"),
        name="compress",
    )(kv_streams, w1cat, cconst, w2b, b2)


def _cmp_branch(q_heads, kc, vc, qpos, n_cmp):
    rows = q_heads[0].shape[0]
    nio = lax.broadcasted_iota(jnp.int32, (rows, kc.shape[0]), 1)
    mask = (nio < n_cmp) & (nio * CMP_STRIDE + (CMP_BLOCK - 1) <= qpos)
    outs, probs = [], []
    for q in q_heads:
        p = _masked_softmax(_dot_nt(q, kc) * SM_SCALE, mask)
        outs.append(_dot(p.astype(BF16), vc))
        probs.append(p)
    return outs, probs


def _select_mask(psum, ov, qpos, n_sel):
    hi = psum.astype(BF16)
    lo = (psum - hi.astype(F32)).astype(BF16)
    imp = _dot(hi, ov) + _dot(lo, ov)
    rows = imp.shape[0]
    jio = lax.broadcasted_iota(jnp.int32, (rows, LANES), 1)
    allowed = (jio * SEL_BLOCK <= qpos) & (jio < n_sel)
    cur = qpos // SEL_BLOCK
    forced = (jio == 0) | (jio == cur) | (jio == cur - 1)
    score = jnp.where(allowed, imp + jnp.where(forced, FORCE_BONUS, 0.0), NEG_INF)
    rank = jnp.zeros((rows, LANES), F32)
    for i in range(n_sel):
        si = jnp.broadcast_to(score[:, i:i + 1], (rows, LANES))
        tie = jnp.where(jio > i, 1.0, 0.0)
        rank = rank + jnp.where(si > score, 1.0, jnp.where(si == score, tie, 0.0))
    return (rank < float(min(N_SELECT, n_sel))) & allowed


def _select_mask_t(psum, ovt, qpos_row, n_sel):
    hi = psum.astype(BF16)
    lo = (psum - hi.astype(F32)).astype(BF16)
    imp = _dot_nt(ovt, hi) + _dot_nt(ovt, lo)
    sr, rows = imp.shape
    jio = lax.broadcasted_iota(jnp.int32, (sr, rows), 0)
    allowed = (jio * SEL_BLOCK <= qpos_row) & (jio < n_sel)
    cur = qpos_row // SEL_BLOCK
    forced = (jio == 0) | (jio == cur) | (jio == cur - 1)
    score = jnp.where(allowed, imp + jnp.where(forced, FORCE_BONUS, 0.0), NEG_INF)
    rank = jnp.zeros((sr, rows), F32)
    for i in range(n_sel):
        si = jnp.broadcast_to(score[i:i + 1, :], (sr, rows))
        tie = jnp.where(jio > i, 1.0, 0.0)
        rank = rank + jnp.where(si > score, 1.0, jnp.where(si == score, tie, 0.0))
    sel_t = jnp.where((rank < float(min(N_SELECT, n_sel))) & allowed, 1.0, 0.0)
    sel_t = jnp.concatenate([sel_t, jnp.zeros((LANES - sr, rows), F32)], axis=0)
    return sel_t.T


def _biased_softmax(s, bias):
    s = s + bias
    e = jnp.exp(s - jnp.max(s, axis=-1, keepdims=True))
    return e / jnp.maximum(jnp.sum(e, axis=-1, keepdims=True), 1e-30)


def _attn_p_kernel(q_ref, kc_ref, vc_ref, ks_ref, vs_ref, kw_ref, vw_ref, gate_ref, z_ref, ovt_ref, e_ref,
                   o_ref, m_s, l_s, acc_s, *, tq, tk, n_cmp, n_sel):
    qi = pl.program_id(2)
    q0 = qi * tq
    qpos = q0 + lax.broadcasted_iota(jnp.int32, (tq, 1), 0)
    q_heads = [q_ref[:, r * HEAD_DIM:(r + 1) * HEAD_DIM] for r in range(HEADS_PER_GROUP)]

    o_cmp, probs = _cmp_branch(q_heads, kc_ref[...], vc_ref[...], qpos, n_cmp)
    psum = probs[0] + probs[1] + probs[2] + probs[3]
    qpos_row = q0 + lax.broadcasted_iota(jnp.int32, (1, tq), 1)
    selb = _select_mask_t(psum, ovt_ref[...], qpos_row, n_sel).astype(BF16)

    m_s[...] = jnp.full(m_s.shape, NEG_INF, F32)
    l_s[...] = jnp.zeros(l_s.shape, F32)
    acc_s[...] = jnp.zeros(acc_s.shape, F32)
    n_kt = (q0 + tq - 1) // tk + 1

    def sweep(kt, carry):
        k0 = pl.multiple_of(kt * tk, tk)
        k_t = ks_ref[pl.ds(k0, tk), :]
        v_t = vs_ref[pl.ds(k0, tk), :]
        kpos = k0 + lax.broadcasted_iota(jnp.int32, (tq, tk), 1)
        bias = jnp.where((_dot(selb, e_ref[kt]) > 0.5) & (kpos <= qpos), 0.0, NEG_INF)
        for r in range(HEADS_PER_GROUP):
            s = _dot_nt(q_heads[r], k_t) * SM_SCALE + bias
            m_prev = m_s[r]
            m_new = jnp.maximum(m_prev, jnp.max(s, axis=-1, keepdims=True))
            p = jnp.exp(s - jnp.concatenate([m_new] * (tk // LANES), axis=1))
            alpha = jnp.exp(m_prev - m_new)
            l_s[r] = alpha * l_s[r] + jnp.sum(p, axis=-1, keepdims=True)
            acc_s[r] = alpha * acc_s[r] + _dot(p.astype(BF16), v_t)
            m_s[r] = m_new
        return carry

    lax.fori_loop(0, n_kt, sweep, 0)

    w0 = pl.multiple_of(jnp.maximum(q0 - WINDOW, 0), LANES)
    k_w = kw_ref[pl.ds(w0, WINDOW + tq), :]
    v_w = vw_ref[pl.ds(w0, WINDOW + tq), :]
    dist = qpos - (w0 + lax.broadcasted_iota(jnp.int32, (tq, WINDOW + tq), 1))
    wbias = jnp.where((dist >= 0) & (dist <= WINDOW), 0.0, NEG_INF)

    gates = _sigmoid(gate_ref[...])
    for r in range(HEADS_PER_GROUP):
        o_slc = acc_s[r] / jnp.maximum(l_s[r], 1e-30)
        p = _biased_softmax(_dot_nt(q_heads[r], k_w) * SM_SCALE, wbias)
        o_win = _dot(p.astype(BF16), v_w)
        g_of = lambda br: jnp.broadcast_to(
            gates[:, br * HEADS_PER_GROUP + r:br * HEADS_PER_GROUP + r + 1], (tq, HEAD_DIM))
        o = g_of(0) * o_cmp[r] + g_of(1) * o_slc + g_of(2) * o_win
        o_ref[:, r * HEAD_DIM:(r + 1) * HEAD_DIM] = (
            o * _silu(z_ref[:, r * HEAD_DIM:(r + 1) * HEAD_DIM])).astype(BF16)


def _attn_p(q_bf, kcv, kv_bf3, win_bf3, gate, rest, ovt, e_p, b, t, n_cmp, n_sel, tq=128, tk=256):
    assert t >= WINDOW + tq and t % tk == 0 and tq == LANES
    nq = t // tq
    ncp = kcv.shape[2]
    sr = ovt.shape[0]
    gw = HEADS_PER_GROUP * HEAD_DIM
    row = lambda bi, g, qi: bi * nq + qi
    return pl.pallas_call(
        functools.partial(_attn_p_kernel, tq=tq, tk=tk, n_cmp=n_cmp, n_sel=n_sel),
        grid=(b, N_KV_HEADS, nq),
        in_specs=[pl.BlockSpec((tq, gw), lambda bi, g, qi: (row(bi, g, qi), g)),
                  pl.BlockSpec((None, None, ncp, HEAD_DIM), lambda bi, g, qi: (bi, g, 0, 0)),
                  pl.BlockSpec((None, None, ncp, HEAD_DIM), lambda bi, g, qi: (bi, N_KV_HEADS + g, 0, 0)),
                  pl.BlockSpec((None, t, HEAD_DIM), lambda bi, g, qi: (bi, 0, 2 * N_KV_HEADS + g)),
                  pl.BlockSpec((None, t, HEAD_DIM), lambda bi, g, qi: (bi, 0, 3 * N_KV_HEADS + g)),
                  pl.BlockSpec((None, t, HEAD_DIM), lambda bi, g, qi: (bi, 0, g)),
                  pl.BlockSpec((None, t, HEAD_DIM), lambda bi, g, qi: (bi, 0, N_KV_HEADS + g)),
                  pl.BlockSpec((tq, LANES), lambda bi, g, qi: (row(bi, g, qi), g)),
                  pl.BlockSpec((tq, gw), lambda bi, g, qi: (row(bi, g, qi), g)),
                  pl.BlockSpec((sr, ncp), lambda bi, g, qi: (0, 0)),
                  pl.BlockSpec((t // tk, LANES, tk), lambda bi, g, qi: (0, 0, 0))],
        out_specs=pl.BlockSpec((tq, gw), lambda bi, g, qi: (row(bi, g, qi), g)),
        out_shape=jax.ShapeDtypeStruct((b * t, NSA_WIDTH), BF16),
        scratch_shapes=[pltpu.VMEM((HEADS_PER_GROUP, tq, LANES), F32),
                        pltpu.VMEM((HEADS_PER_GROUP, tq, LANES), F32),
                        pltpu.VMEM((HEADS_PER_GROUP, tq, HEAD_DIM), F32)],
        compiler_params=_params("parallel", "parallel", "arbitrary"),
        name="attn_p",
    )(q_bf, kcv, kcv, kv_bf3, kv_bf3, win_bf3, win_bf3, gate, rest, ovt, e_p)


def _attn_s_kernel(pt_ref, cache_ref, wst_ref, q_ref, kvn_ref, winn_ref, gate_ref, z_ref, winc_ref,
                   w1_ref, cc_ref, w2_ref, b2_ref, ov_ref, e_ref, perm_ref, o_ref, wout_ref,
                   kvbuf, wbuf, wnew, kcs, ksl, kwn, sem, wsem, osem,
                   *, n_pages, page, n_cmp, n_sel, pos0, t_new):
    i = pl.program_id(0)
    nb = pl.num_programs(0) - 1
    past = n_pages * page
    nch = past // CMP_STRIDE
    rows = HEADS_PER_GROUP * SAMPLE_ROWS
    n_ws = N_WIN_SLOTS * N_KV_HEADS
    state_rows = WINDOW * n_ws
    new_rows = t_new * n_ws
    prow = page * _KV_STREAMS
    c_seq = jnp.minimum(i, nb - 1)
    a_seq = jnp.maximum(i - 1, 0)
    cs = c_seq % 2
    asl = a_seq % 2

    def page_copy(seq, p, slot):
        return pltpu.make_async_copy(cache_ref.at[pt_ref[seq, p]], kvbuf.at[slot, pl.ds(p * prow, prow)],
                                     sem.at[slot])

    def win_copy(seq, slot):
        return pltpu.make_async_copy(wst_ref.at[seq], wbuf.at[slot], wsem.at[slot])

    def fetch(seq, slot):
        for p in range(n_pages):
            page_copy(seq, p, slot).start()
        win_copy(seq, slot).start()

    def state_copies(seq, slot):
        kept = pltpu.make_async_copy(wbuf.at[slot, pl.ds(new_rows, state_rows - new_rows)],
                                     wout_ref.at[seq, pl.ds(0, state_rows - new_rows)], osem.at[slot])
        fresh = pltpu.make_async_copy(wnew.at[slot], wout_ref.at[seq, pl.ds(state_rows - new_rows, new_rows)],
                                      osem.at[slot])
        return kept, fresh

    @pl.when(i == 0)
    def _():
        fetch(0, 0)
        kcs[0] = jnp.zeros(kcs.shape[1:], BF16)
        ksl[0] = jnp.zeros(ksl.shape[1:], BF16)
        kwn[0] = jnp.zeros(kwn.shape[1:], BF16)

    @pl.when(i >= 1)
    def _():
        for cp in state_copies(i - 1, (i - 1) % 2):
            cp.wait()

    @pl.when(i + 1 < nb)
    def _():
        fetch(i + 1, (i + 1) % 2)

    @pl.when(i < nb)
    def _():
        for p in range(n_pages):
            page_copy(i, p, i % 2).wait()
        win_copy(i, i % 2).wait()
        for t in range(t_new):
            for c in range(n_ws):
                wnew[i % 2, pl.ds(t * n_ws + c, 1), :] = winc_ref[pl.ds(t, 1), c * HEAD_DIM:(c + 1) * HEAD_DIM]
        for cp in state_copies(i, i % 2):
            cp.start()

    qpos = pos0 + lax.broadcasted_iota(jnp.int32, (rows, 1), 0) % SAMPLE_ROWS
    kvn = kvn_ref[...].astype(F32)
    qf = q_ref[...].astype(F32)
    gates = _sigmoid(gate_ref[...])
    zero_tail = jnp.zeros((LANES - 2 * SUBLANES, HEAD_DIM), BF16)
    pad8 = jnp.zeros((SUBLANES, HEAD_DIM), F32)

    def with_new_rows(old_bf, new_f32):
        new16 = jnp.concatenate([new_f32, pad8], axis=0).astype(BF16)
        return jnp.concatenate([old_bf, new16, zero_tail], axis=0)

    def stack_heads(x8):
        return jnp.concatenate([x8] * HEADS_PER_GROUP, axis=0)

    groups = range(N_KV_HEADS)
    nk = past + LANES
    kpos = lax.broadcasted_iota(jnp.int32, (rows, nk), 1)
    wpos = (pos0 - WINDOW) + lax.broadcasted_iota(jnp.int32, (rows, WINDOW + LANES), 1)
    dist = qpos - wpos
    wmask = (dist >= 0) & (dist <= WINDOW) & (wpos >= 0)
    nio = lax.broadcasted_iota(jnp.int32, (rows, nch), 1)
    cmask = (nio < n_cmp) & (nio * CMP_STRIDE + (CMP_BLOCK - 1) <= qpos)

    qgs, s_cmp, s_slc, s_win, v_alls, vw_alls = [], [], [], [], [], []
    for g in groups:
        heads = [qf[:, (g * HEADS_PER_GROUP + r) * HEAD_DIM:(g * HEADS_PER_GROUP + r + 1) * HEAD_DIM]
                 for r in range(HEADS_PER_GROUP)]
        qgs.append(jnp.concatenate(heads, axis=0).astype(BF16))
    for g in groups:
        s_cmp.append(_dot_nt(qgs[g], kcs[asl, g]) * SM_SCALE)
    for g in groups:
        c_k = (2 * N_KV_HEADS + g) * HEAD_DIM
        c_v = (3 * N_KV_HEADS + g) * HEAD_DIM
        k_all = with_new_rows(ksl[asl, g], kvn[:, c_k:c_k + HEAD_DIM])
        v_alls.append(with_new_rows(ksl[asl, N_KV_HEADS + g], kvn[:, c_v:c_v + HEAD_DIM]))
        s_slc.append(_dot_nt(qgs[g], k_all) * SM_SCALE)
    for g in groups:
        wk = g * HEAD_DIM
        wv = (N_KV_HEADS + g) * HEAD_DIM
        kw_all = with_new_rows(kwn[asl, g], winn_ref[:, wk:wk + HEAD_DIM])
        vw_alls.append(with_new_rows(kwn[asl, N_KV_HEADS + g], winn_ref[:, wv:wv + HEAD_DIM]))
        s_win.append(_dot_nt(qgs[g], kw_all) * SM_SCALE)

    vcs = [kcs[asl, N_KV_HEADS + g] for g in groups]

    perm = perm_ref[...]
    blk = LANES
    cpb = blk // CMP_STRIDE
    stream = lambda c: kvbuf[cs, pl.ds(c, past, stride=_KV_STREAMS), :].astype(BF16)

    def chunk_rows(c):
        xs = stream(c)
        out = []
        for k in range(past // blk):
            y = _dot(perm, xs[k * blk:(k + 1) * blk])
            out.append(jnp.concatenate([y[l * cpb:(l + 1) * cpb] for l in range(CMP_STRIDE)], axis=1))
        return jnp.concatenate(out, axis=0).astype(BF16)

    def slot_chunks(s):
        return jnp.concatenate([chunk_rows(s * N_KV_HEADS + g) for g in groups], axis=0)

    def compress_slot(s, lhs):
        y = _dot(lhs, w1_ref[s])
        second = pltpu.roll(y[:, HEAD_DIM:], shift=lhs.shape[0] - 1, axis=0)
        h = _silu(y[:, :HEAD_DIM] + second + cc_ref[s])
        out = _dot(h.astype(BF16), w2_ref[s]) + b2_ref[s]
        row = lax.broadcasted_iota(jnp.int32, out.shape, 0) % nch
        out = jnp.where(row < nch - 1, out, 0.0).astype(BF16)
        for g in groups:
            kcs[cs, s * N_KV_HEADS + g] = out[g * nch:(g + 1) * nch]

    lhs0 = slot_chunks(0)

    p_cmps = [_masked_softmax(s_cmp[g], cmask) for g in groups]
    psums = [(p[0:8] + p[8:16]) + (p[16:24] + p[24:32]) for p in p_cmps]
    sels = [_select_mask(psums[g], ov_ref[...], qpos[0:SAMPLE_ROWS], n_sel) for g in groups]
    selbs = [jnp.where(s, 1.0, 0.0).astype(BF16) for s in sels]

    compress_slot(0, lhs0)
    lhs1 = slot_chunks(1)

    smasks = [(stack_heads(_dot(selbs[g], e_ref[...])) > 0.5) & (kpos <= qpos) for g in groups]
    o_cmps = [_dot(p_cmps[g].astype(BF16), vcs[g]) for g in groups]
    p_wins = [_masked_softmax(s_win[g], wmask) for g in groups]
    o_wins = [_dot(p_wins[g].astype(BF16), vw_alls[g]) for g in groups]
    p_slcs = [_masked_softmax(s_slc[g], smasks[g]) for g in groups]
    o_slcs = [_dot(p_slcs[g].astype(BF16), v_alls[g]) for g in groups]

    compress_slot(1, lhs1)
    for k in range(2 * N_KV_HEADS):
        ksl[cs, k] = stream(2 * N_KV_HEADS + k)
    for k in range(n_ws):
        kwn[cs, k] = wbuf[cs, pl.ds(k, WINDOW, stride=n_ws), :].astype(BF16)

    for g in groups:
        o_cmp, o_slc, o_win = o_cmps[g], o_slcs[g], o_wins[g]
        for r in range(HEADS_PER_GROUP):
            h = g * HEADS_PER_GROUP + r
            rs = slice(r * SAMPLE_ROWS, (r + 1) * SAMPLE_ROWS)
            g_of = lambda br: jnp.broadcast_to(
                gates[:, g * LANES + br * HEADS_PER_GROUP + r:g * LANES + br * HEADS_PER_GROUP + r + 1],
                (SAMPLE_ROWS, HEAD_DIM))
            o = g_of(0) * o_cmp[rs] + g_of(1) * o_slc[rs] + g_of(2) * o_win[rs]
            o_ref[:, h * HEAD_DIM:(h + 1) * HEAD_DIM] = (
                o * _silu(z_ref[:, h * HEAD_DIM:(h + 1) * HEAD_DIM])).astype(BF16)


def _attn_s(page_table, cache3, wst3, q3, kvn3, rest3, gate3, w1cat, cconst, w2b, b2, ov, e_s,
            n_cmp, n_sel, pos0, t_new):
    nb, n_pages = page_table.shape
    page = cache3.shape[1] // _KV_STREAMS
    past = n_pages * page
    nch = past // CMP_STRIDE
    n_ws = N_WIN_SLOTS * N_KV_HEADS
    assert wst3.shape[1:] == (WINDOW * n_ws, HEAD_DIM) and past % LANES == 0
    ncp = ov.shape[0]
    prev = lambda i: jnp.maximum(i - 1, 0)
    grid_spec = pltpu.PrefetchScalarGridSpec(
        num_scalar_prefetch=1,
        grid=(nb + 1,),
        in_specs=[pl.BlockSpec(memory_space=pl.ANY),
                  pl.BlockSpec(memory_space=pl.ANY),
                  pl.BlockSpec((None, SAMPLE_ROWS, Q_COLS), lambda i, pt: (prev(i), 0, 0)),
                  pl.BlockSpec((None, SAMPLE_ROWS, KV_COLS), lambda i, pt: (prev(i), 0, 0)),
                  pl.BlockSpec((None, SAMPLE_ROWS, WIN_COLS), lambda i, pt: (prev(i), 0, _REST_WIN)),
                  pl.BlockSpec((None, SAMPLE_ROWS, 2 * LANES), lambda i, pt: (prev(i), 0, 0)),
                  pl.BlockSpec((None, SAMPLE_ROWS, NSA_WIDTH), lambda i, pt: (prev(i), 0, _REST_Z)),
                  pl.BlockSpec((None, SAMPLE_ROWS, WIN_COLS), lambda i, pt: (jnp.minimum(i, nb - 1), 0, _REST_WIN)),
                  pl.BlockSpec((2, 16 * HEAD_DIM, 2 * HEAD_DIM), lambda i, pt: (0, 0, 0)),
                  pl.BlockSpec((2, 1, HEAD_DIM), lambda i, pt: (0, 0, 0)),
                  pl.BlockSpec((2, HEAD_DIM, HEAD_DIM), lambda i, pt: (0, 0, 0)),
                  pl.BlockSpec((2, 1, HEAD_DIM), lambda i, pt: (0, 0, 0)),
                  pl.BlockSpec((ncp, LANES), lambda i, pt: (0, 0)),
                  pl.BlockSpec((LANES, past + LANES), lambda i, pt: (0, 0)),
                  pl.BlockSpec((LANES, LANES), lambda i, pt: (0, 0))],
        out_specs=[pl.BlockSpec((None, SAMPLE_ROWS, NSA_WIDTH), lambda i, pt: (prev(i), 0, 0)),
                   pl.BlockSpec(memory_space=pl.ANY)],
        scratch_shapes=[pltpu.VMEM((2, past * _KV_STREAMS, HEAD_DIM), F32),
                        pltpu.VMEM((2, WINDOW * n_ws, HEAD_DIM), F32),
                        pltpu.VMEM((2, t_new * n_ws, HEAD_DIM), F32),
                        pltpu.VMEM((2, 2 * N_KV_HEADS, nch, HEAD_DIM), BF16),
                        pltpu.VMEM((2, 2 * N_KV_HEADS, past, HEAD_DIM), BF16),
                        pltpu.VMEM((2, n_ws, WINDOW, HEAD_DIM), BF16),
                        pltpu.SemaphoreType.DMA((2,)),
                        pltpu.SemaphoreType.DMA((2,)),
                        pltpu.SemaphoreType.DMA((2,))],
    )
    return pl.pallas_call(
        functools.partial(_attn_s_kernel, n_pages=n_pages, page=page, n_cmp=n_cmp, n_sel=n_sel, pos0=pos0,
                          t_new=t_new),
        grid_spec=grid_spec,
        out_shape=[jax.ShapeDtypeStruct((nb, SAMPLE_ROWS, NSA_WIDTH), BF16),
                   jax.ShapeDtypeStruct((nb, WINDOW * n_ws, HEAD_DIM), F32)],
        compiler_params=_params("arbitrary"),
        name="attn_s",
    )(page_table, cache3, wst3, q3, kvn3, rest3, gate3, rest3, rest3, w1cat, cconst, w2b, b2, ov, e_s,
      _chunk_permutation())


_CONV_RC = 64
_CONV_LC = 256


def _depthwise_conv(uext, w_ref, db_ref, ybuf, tt):
    off = CONV_HALO - (CONV_WIDTH - 1)
    uext[CONV_HALO + tt:CONV_HALO + tt + SUBLANES] = jnp.zeros((SUBLANES, CONV_DIM), F32)
    rc = min(_CONV_RC, tt)
    for r0 in range(0, tt, rc):
        for c0 in range(0, CONV_DIM, _CONV_LC):
            lanes = slice(c0, c0 + _CONV_LC)
            acc = jnp.broadcast_to(db_ref[:, lanes], (rc, _CONV_LC))
            for r in range(SUBLANES):
                z = None
                for a in range((CONV_WIDTH + off) // SUBLANES + 1):
                    k = SUBLANES * a + r - off
                    if 0 <= k < CONV_WIDTH:
                        term = w_ref[k:k + 1, lanes] * uext[r0 + SUBLANES * a:r0 + SUBLANES * a + rc + SUBLANES, lanes]
                        z = term if z is None else z + term
                acc = acc + z[r:r + rc]
            ybuf[r0:r0 + rc, lanes] = acc


def _conv_kernel(a_ref, gl_ref, ah_ref, glh_ref, init_ref, zc_ref, w_ref, db_ref, lg_ref, lb_ref,
                 cv_ref, st_ref, uext, ybuf, *, tt, t_valid):
    ti = pl.program_id(1)
    nt = pl.num_programs(1)
    halo = ah_ref[...] * _sigmoid(glh_ref[...])
    uext[0:CONV_HALO] = jnp.where(ti == 0, init_ref[...], halo)
    uext[CONV_HALO:CONV_HALO + tt] = a_ref[...] * _sigmoid(gl_ref[...])
    _depthwise_conv(uext, w_ref, db_ref, ybuf, tt)

    y = ybuf[...]
    mu = jnp.mean(y, axis=-1, keepdims=True)
    var = jnp.mean(jnp.square(y - mu), axis=-1, keepdims=True)
    ln = (y - mu) * lax.rsqrt(var + NORM_EPS) * lg_ref[...] + lb_ref[...]
    cv_ref[...] = (_silu(ln) * _silu(zc_ref[...])).astype(BF16)

    @pl.when(ti == nt - 1)
    def _():
        st_ref[...] = uext[pl.ds(t_valid, CONV_HALO), :]


def _conv(rest, u_init, w_pad, db, lg, lb, b, t, tt, t_valid):
    nt = t // tt
    hb = tt // CONV_HALO if tt >= CONV_HALO else None
    row = lambda bi, ti: bi * nt + ti
    if hb is None:
        assert nt == 1
        halo_map = lambda col: (lambda bi, ti: (0, col))
        halo_rows = tt
    else:
        halo_map = lambda col: (lambda bi, ti: (jnp.maximum(row(bi, ti) * hb - 1, 0), col))
        halo_rows = CONV_HALO
    vec = lambda: pl.BlockSpec((1, CONV_DIM), lambda bi, ti: (0, 0))
    kern = functools.partial(_conv_kernel, tt=tt, t_valid=t_valid)
    if hb is None:
        kern = functools.partial(_conv_kernel_nohalo, tt=tt, t_valid=t_valid)
    return pl.pallas_call(
        kern,
        grid=(b, nt),
        in_specs=[pl.BlockSpec((tt, CONV_DIM), lambda bi, ti: (row(bi, ti), _REST_A)),
                  pl.BlockSpec((tt, CONV_DIM), lambda bi, ti: (row(bi, ti), _REST_GL)),
                  pl.BlockSpec((halo_rows, CONV_DIM), halo_map(_REST_A)),
                  pl.BlockSpec((halo_rows, CONV_DIM), halo_map(_REST_GL)),
                  pl.BlockSpec((None, CONV_HALO, CONV_DIM), lambda bi, ti: (bi, 0, 0)),
                  pl.BlockSpec((tt, CONV_DIM), lambda bi, ti: (row(bi, ti), _REST_ZC)),
                  pl.BlockSpec((CONV_HALO, CONV_DIM), lambda bi, ti: (0, 0)),
                  vec(), vec(), vec()],
        out_specs=[pl.BlockSpec((tt, CONV_DIM), lambda bi, ti: (row(bi, ti), 0)),
                   pl.BlockSpec((None, CONV_HALO, CONV_DIM), lambda bi, ti: (bi, 0, 0))],
        out_shape=[jax.ShapeDtypeStruct((b * t, CONV_DIM), BF16),
                   jax.ShapeDtypeStruct((b, CONV_HALO, CONV_DIM), F32)],
        scratch_shapes=[pltpu.VMEM((CONV_HALO + tt + SUBLANES, CONV_DIM), F32),
                        pltpu.VMEM((tt, CONV_DIM), F32)],
        compiler_params=_params("parallel", "arbitrary"),
        name="conv",
    )(rest, rest, rest, rest, u_init, rest, w_pad, db, lg, lb)


def _conv_kernel_nohalo(a_ref, gl_ref, ah_ref, glh_ref, init_ref, zc_ref, w_ref, db_ref, lg_ref, lb_ref,
                        cv_ref, st_ref, uext, ybuf, *, tt, t_valid):
    del ah_ref, glh_ref
    uext[0:CONV_HALO] = init_ref[...]
    uext[CONV_HALO:CONV_HALO + tt] = a_ref[...] * _sigmoid(gl_ref[...])
    _depthwise_conv(uext, w_ref, db_ref, ybuf, tt)
    y = ybuf[...]
    mu = jnp.mean(y, axis=-1, keepdims=True)
    var = jnp.mean(jnp.square(y - mu), axis=-1, keepdims=True)
    ln = (y - mu) * lax.rsqrt(var + NORM_EPS) * lg_ref[...] + lb_ref[...]
    cv_ref[...] = (_silu(ln) * _silu(zc_ref[...])).astype(BF16)
    st_ref[...] = uext[pl.ds(t_valid, CONV_HALO), :]


def _outproj_kernel(on_ref, cv_ref, w_ref, x_ref, gate_ref, g_ref, y_ref):
    mix = _dot(on_ref[...], w_ref[0:NSA_WIDTH, :]) + _dot(cv_ref[...], w_ref[NSA_WIDTH:, :])
    nrm = mix * lax.rsqrt(jnp.mean(mix * mix, axis=-1, keepdims=True) + NORM_EPS) * g_ref[...]
    y_ref[...] = x_ref[...] + gate_ref[...] * nrm


def _outproj(o_nsa, cv, w_out_b, x, gate3, g_post, tm, rows_per_mod):
    m, d = x.shape
    mod_rows = gate3.shape[1]
    tiles_per_mod = rows_per_mod // tm
    return pl.pallas_call(
        _outproj_kernel,
        grid=(m // tm,),
        in_specs=[pl.BlockSpec((tm, NSA_WIDTH), lambda i: (i, 0)),
                  pl.BlockSpec((tm, CONV_DIM), lambda i: (i, 0)),
                  pl.BlockSpec((d, d), lambda i: (0, 0)),
                  pl.BlockSpec((tm, d), lambda i: (i, 0)),
                  pl.BlockSpec((None, mod_rows, d), lambda i: (i // tiles_per_mod, 0, 0)),
                  pl.BlockSpec((1, d), lambda i: (0, 0))],
        out_specs=pl.BlockSpec((tm, d), lambda i: (i, 0)),
        out_shape=jax.ShapeDtypeStruct((m, d), F32),
        compiler_params=_params("parallel"),
        name="outproj",
    )(o_nsa, cv, w_out_b, x, gate3, g_post)


def _overlap_matrix(n_cmp, n_sel, ncp):
    ci = np.arange(ncp)[:, None] * CMP_STRIDE
    sj = np.arange(LANES)[None, :] * SEL_BLOCK
    ov = (ci < sj + SEL_BLOCK) & (ci + CMP_BLOCK > sj)
    ov &= (np.arange(ncp)[:, None] < n_cmp) & (np.arange(LANES)[None, :] < n_sel)
    return jnp.asarray(ov.astype(np.float32), dtype=BF16)


def _overlap_matrix_t(n_cmp, n_sel, ncp):
    sr = _round_up(n_sel, 2 * SUBLANES)
    return _overlap_matrix(n_cmp, n_sel, ncp).T[:sr]


def _chunk_permutation():
    p = np.zeros((LANES, LANES), np.float32)
    for j in range(LANES // CMP_STRIDE):
        for l in range(CMP_STRIDE):
            p[l * (LANES // CMP_STRIDE) + j, CMP_STRIDE * j + l] = 1.0
    return jnp.asarray(p, dtype=BF16)


def _expand_matrix(n_keys):
    e = (np.arange(n_keys)[None, :] // SEL_BLOCK) == np.arange(LANES)[:, None]
    return jnp.asarray(e.astype(np.float32), dtype=BF16)


def _round_up(x, m):
    return -(-x // m) * m


def kernel(x_prompt, x_sample, c_prompt, c_sample, cache_kv, page_table, state_win_kv, state_conv, w_ada, b_ada, norm_pre, norm_post, w_in, cmp_pe, cmp_w1, cmp_b1, cmp_w2, cmp_b2, conv_dw, conv_db, conv_ln_g, conv_ln_b, w_out):
    assert w_ada.shape[0] == 1, "single-layer trunk"
    bp, tp, d = x_prompt.shape
    bs, ts, _ = x_sample.shape
    n_phys, page = cache_kv.shape[1], cache_kv.shape[2]
    n_pages = page_table.shape[1]
    past = n_pages * page
    assert ts <= SAMPLE_ROWS and d == D_MODEL

    w = w_in[0]
    c_q, c_kv, c_win, c_gate = 0, Q_COLS, Q_COLS + KV_COLS, Q_COLS + KV_COLS + WIN_COLS
    c_rest = c_gate + GATE_COLS
    w_main = jnp.concatenate([w[:, c_q:c_win], w[:, c_rest:], w[:, c_win:c_gate]], axis=1).astype(BF16)
    wg = w[:, c_gate:c_rest].reshape(d, N_BRANCH, N_KV_HEADS, HEADS_PER_GROUP).transpose(0, 2, 1, 3)
    wg = wg.reshape(d, N_KV_HEADS, N_BRANCH * HEADS_PER_GROUP)
    w_gate = jnp.pad(wg, ((0, 0), (0, 0), (0, LANES - N_BRANCH * HEADS_PER_GROUP))).reshape(d, 2 * LANES).astype(BF16)
    w_out_b = w_out[0].astype(BF16)
    w1 = cmp_w1[0]
    w1cat = jnp.concatenate([w1[:, :CMP_STRIDE].reshape(2, CMP_STRIDE * HEAD_DIM, HEAD_DIM),
                             w1[:, CMP_STRIDE:].reshape(2, CMP_STRIDE * HEAD_DIM, HEAD_DIM)], axis=2).astype(BF16)
    w2b = cmp_w2[0].astype(BF16)
    pe_rows = jnp.pad(cmp_pe[0].reshape(2, 2, CMP_STRIDE * HEAD_DIM), ((0, 0), (0, SUBLANES - 2), (0, 0)))
    b1 = cmp_b1[0].reshape(2, 1, HEAD_DIM)
    b2 = cmp_b2[0].reshape(2, 1, HEAD_DIM)
    conv_w_pad = jnp.pad(conv_dw[0], ((0, CONV_HALO - CONV_WIDTH), (0, 0)))
    db = conv_db[0].reshape(1, CONV_DIM)
    lg = conv_ln_g[0].reshape(1, CONV_DIM)
    lb = conv_ln_b[0].reshape(1, CONV_DIM)
    g_pre = norm_pre[0].reshape(1, d)
    g_post = norm_post[0].reshape(1, d)

    n_c = bp + bs
    c_all = jnp.pad(jnp.concatenate([c_prompt, c_sample], axis=0), ((0, _round_up(n_c, SUBLANES) - n_c), (0, 0)))
    mod = _ada(c_all, w_ada[0], b_ada[0])
    shift, scale, gate = mod[:, :d], mod[:, d:2 * d], mod[:, 2 * d:]
    cconst = _cmpconst(pe_rows, w1cat, b1)

    tm_p = 512
    xp = x_prompt.reshape(bp * tp, d)
    q_p, kv_p, kvb_p, rest_p, winb_p, gate_p = _inproj(
        xp, g_pre, scale[:bp].reshape(bp, 1, d), shift[:bp].reshape(bp, 1, d), w_main, w_gate, tm_p, tp)
    n_cmp_p = (tp - CMP_BLOCK) // CMP_STRIDE + 1
    n_sel_p = -(-tp // SEL_BLOCK)
    ncp_p = _round_up(tp // CMP_STRIDE, LANES)
    kcv_p = _compress(kv_p.reshape(bp, tp * _KV_STREAMS, HEAD_DIM), w1cat, cconst, w2b, b2, ncp_p)
    tk = 256
    e_p = _expand_matrix(tp).reshape(LANES, tp // tk, tk).transpose(1, 0, 2)
    o_nsa_p = _attn_p(q_p, kcv_p, kvb_p.reshape(bp, tp, KV_COLS), winb_p.reshape(bp, tp, WIN_COLS), gate_p, rest_p,
                      _overlap_matrix_t(n_cmp_p, n_sel_p, ncp_p), e_p, bp, tp, n_cmp_p, n_sel_p, tk=tk)
    cv_p, st_p = _conv(rest_p, jnp.zeros((bp, CONV_HALO, CONV_DIM), F32), conv_w_pad, db, lg, lb, bp, tp, 128, 128)
    y_p = _outproj(o_nsa_p, cv_p, w_out_b, xp, gate[:bp].reshape(bp, 1, d), g_post, 256, tp)

    r = SAMPLE_ROWS
    xs = jnp.pad(x_sample, ((0, 0), (0, r - ts), (0, 0))).reshape(bs * r, d)
    per_row = lambda v: jnp.repeat(v[bp:bp + bs], r, axis=0)
    tm_s = min(512, bs * r)
    n_mod = (bs * r) // tm_s
    q_s, kv_s, kvb_s, rest_s, _, gate_s = _inproj(
        xs, g_pre, per_row(scale).reshape(n_mod, tm_s, d), per_row(shift).reshape(n_mod, tm_s, d),
        w_main, w_gate, tm_s, tm_s)
    total = past + ts
    n_cmp_s = (total - CMP_BLOCK) // CMP_STRIDE + 1
    n_sel_s = -(-total // SEL_BLOCK)
    assert n_cmp_s <= past // CMP_STRIDE - 1 + 1 and (n_cmp_s - 1) * CMP_STRIDE + CMP_BLOCK <= past
    ncp_s = _round_up(past // CMP_STRIDE, LANES)
    assert ncp_s == past // CMP_STRIDE
    n_ws = N_WIN_SLOTS * N_KV_HEADS
    o_nsa_s, win_state_s = _attn_s(
        page_table, cache_kv.reshape(n_phys, page * _KV_STREAMS, HEAD_DIM),
        state_win_kv.reshape(bs, WINDOW * n_ws, HEAD_DIM),
        q_s.reshape(bs, r, Q_COLS), kvb_s.reshape(bs, r, KV_COLS), rest_s.reshape(bs, r, _REST_COLS),
        gate_s.reshape(bs, r, 2 * LANES), w1cat, cconst, w2b, b2,
        _overlap_matrix(n_cmp_s, n_sel_s, ncp_s), _expand_matrix(past + LANES),
        n_cmp_s, n_sel_s, past, ts)
    u_init_s = jnp.pad(state_conv[0], ((0, 0), (CONV_HALO - (CONV_WIDTH - 1), 0), (0, 0)))
    cv_s, st_s = _conv(rest_s, u_init_s, conv_w_pad, db, lg, lb, bs, r, r, ts)
    y_s = _outproj(o_nsa_s.reshape(bs * r, NSA_WIDTH), cv_s, w_out_b, xs, per_row(gate).reshape(n_mod, tm_s, d),
                   g_post, tm_s, tm_s)

    keep = CONV_WIDTH - 1
    y_prompt = y_p.reshape(bp, tp, d)
    y_sample = y_s.reshape(bs, r, d)[:, :ts]
    kv_rows_prompt = kv_p.reshape(1, bp, tp, N_CACHE_SLOTS, N_KV_HEADS, HEAD_DIM)
    kv_rows_sample = kv_s.reshape(1, bs, r, N_CACHE_SLOTS, N_KV_HEADS, HEAD_DIM)[:, :, :ts]
    win_p = rest_p.reshape(bp, tp, _REST_COLS)[:, tp - min(WINDOW, tp):, _REST_WIN * WIN_COLS:]
    win_prompt = win_p.reshape(1, bp, min(WINDOW, tp), N_WIN_SLOTS, N_KV_HEADS, HEAD_DIM)
    win_sample = win_state_s.reshape(1, bs, WINDOW, N_WIN_SLOTS, N_KV_HEADS, HEAD_DIM)
    conv_prompt = st_p[:, CONV_HALO - keep:].reshape(1, bp, keep, CONV_DIM)
    conv_sample = st_s[:, CONV_HALO - keep:].reshape(1, bs, keep, CONV_DIM)
    return (y_prompt, y_sample, kv_rows_prompt, kv_rows_sample, win_prompt, win_sample, conv_prompt, conv_sample)
```

```python
import functools
import math

import numpy as np
import jax
import jax.numpy as jnp
from jax import lax
from jax.experimental import pallas as pl
from jax.experimental.pallas import tpu as pltpu

D_MODEL = 2048
HEAD_DIM = 128
N_HEADS = 8
N_KV_HEADS = 2
HEADS_PER_GROUP = N_HEADS // N_KV_HEADS
NSA_WIDTH = N_HEADS * HEAD_DIM
CONV_DIM = D_MODEL - NSA_WIDTH
CMP_BLOCK = 32
CMP_STRIDE = 16
SEL_BLOCK = 64
N_SELECT = 16
WINDOW = 512
CONV_WIDTH = 31
N_CACHE_SLOTS = 4
N_WIN_SLOTS = 2
N_BRANCH = 3
Q_COLS = N_HEADS * HEAD_DIM
KV_COLS = N_CACHE_SLOTS * N_KV_HEADS * HEAD_DIM
WIN_COLS = N_WIN_SLOTS * N_KV_HEADS * HEAD_DIM
GATE_COLS = N_BRANCH * N_HEADS
_KV_STREAMS = N_CACHE_SLOTS * N_KV_HEADS
NORM_EPS = 1e-6
NEG_INF = -1e30
FORCE_BONUS = 1e6
SM_SCALE = HEAD_DIM ** -0.5

LANES = 128
SUBLANES = 8
CONV_HALO = 32
SAMPLE_ROWS = 8
VMEM_LIMIT = 56 * 1024 * 1024

F32 = jnp.float32
BF16 = jnp.bfloat16


def _sigmoid(x):
    return 1.0 / (1.0 + jnp.exp(-x))


def _silu(x):
    return x * _sigmoid(x)


def _dot(a, b):
    return jnp.dot(a, b, preferred_element_type=F32)


def _dot_nt(a, b):
    return lax.dot_general(a, b, (((1,), (1,)), ((), ())), preferred_element_type=F32)


def _masked_softmax(s, mask):
    s = jnp.where(mask, s, NEG_INF)
    e = jnp.where(mask, jnp.exp(s - jnp.max(s, axis=-1, keepdims=True)), 0.0)
    return e / jnp.maximum(jnp.sum(e, axis=-1, keepdims=True), 1e-30)


def _params(*sem):
    return pltpu.CompilerParams(dimension_semantics=sem, vmem_limit_bytes=VMEM_LIMIT)


def _ada_kernel(c_ref, w_ref, b_ref, o_ref):
    o_ref[...] = _dot(_silu(c_ref[...]).astype(BF16), w_ref[...].astype(BF16)) + b_ref[...]


def _ada(c_all, w_ada, b_ada, tn=512):
    rows, d = c_all.shape
    n = w_ada.shape[1]
    return pl.pallas_call(
        _ada_kernel,
        grid=(n // tn,),
        in_specs=[pl.BlockSpec((rows, d), lambda j: (0, 0)),
                  pl.BlockSpec((d, tn), lambda j: (0, j)),
                  pl.BlockSpec((1, tn), lambda j: (0, j))],
        out_specs=pl.BlockSpec((rows, tn), lambda j: (0, j)),
        out_shape=jax.ShapeDtypeStruct((rows, n), F32),
        compiler_params=_params("parallel"),
        name="ada",
    )(c_all, w_ada, b_ada.reshape(1, n))


def _cmpconst_kernel(pe_ref, w1_ref, b1_ref, o_ref):
    y = _dot(pe_ref[...].astype(BF16), w1_ref[...])
    o_ref[...] = y[0:1, :HEAD_DIM] + y[1:2, HEAD_DIM:] + b1_ref[...]


def _cmpconst(pe_rows, w1cat, b1):
    return pl.pallas_call(
        _cmpconst_kernel,
        grid=(2,),
        in_specs=[pl.BlockSpec((None, SUBLANES, 16 * HEAD_DIM), lambda s: (s, 0, 0)),
                  pl.BlockSpec((None, 16 * HEAD_DIM, 2 * HEAD_DIM), lambda s: (s, 0, 0)),
                  pl.BlockSpec((None, 1, HEAD_DIM), lambda s: (s, 0, 0))],
        out_specs=pl.BlockSpec((None, 1, HEAD_DIM), lambda s: (s, 0, 0)),
        out_shape=jax.ShapeDtypeStruct((2, 1, HEAD_DIM), F32),
        compiler_params=_params("parallel"),
        name="cmpconst",
    )(pe_rows, w1cat, b1)


_TN = 512
_J_KV = Q_COLS // _TN
_J_REST = (Q_COLS + KV_COLS) // _TN
_REST_COLS = NSA_WIDTH + 3 * CONV_DIM + WIN_COLS
_N_MAIN = Q_COLS + KV_COLS + _REST_COLS
_J_WIN = (_N_MAIN - WIN_COLS) // _TN
_NJ = _N_MAIN // _TN
_REST_Z = 0
_REST_A = 1
_REST_GL = 2
_REST_ZC = 3
_REST_WIN = (NSA_WIDTH + 3 * CONV_DIM) // WIN_COLS


def _inproj_kernel(x_ref, g_ref, sc_ref, sh_ref, w_ref, wg_ref,
                   q_ref, kv_ref, kvb_ref, rest_ref, winb_ref, gate_ref, h_ref):
    j = pl.program_id(1)

    @pl.when(j == 0)
    def _():
        x = x_ref[...]
        y = x * lax.rsqrt(jnp.mean(x * x, axis=-1, keepdims=True) + NORM_EPS) * g_ref[...]
        h = (y * (1.0 + sc_ref[...]) + sh_ref[...]).astype(BF16)
        h_ref[...] = h
        gate_ref[...] = _dot(h, wg_ref[...])

    acc = _dot(h_ref[...], w_ref[...])

    @pl.when(j < _J_KV)
    def _():
        q_ref[...] = acc.astype(BF16)

    tm = acc.shape[0]
    streams_per_tile = _TN // HEAD_DIM
    for jj in range(_J_KV, _J_REST):
        @pl.when(j == jj)
        def _(jj=jj):
            kvb_ref[...] = acc.astype(BF16)
            for cc in range(streams_per_tile):
                c = (jj - _J_KV) * streams_per_tile + cc
                kv_ref[pl.ds(c, tm, stride=_KV_STREAMS), :] = acc[:, cc * HEAD_DIM:(cc + 1) * HEAD_DIM]

    @pl.when(j >= _J_REST)
    def _():
        rest_ref[...] = acc

    @pl.when(j == _J_WIN)
    def _():
        winb_ref[...] = acc.astype(BF16)


def _inproj(x, g, scale3, shift3, w_main, w_gate, tm, rows_per_mod):
    m, d = x.shape
    mod_rows = scale3.shape[1]
    tiles_per_mod = rows_per_mod // tm

    def mod_map(i, j):
        return (i // tiles_per_mod, 0, 0)

    clampj = lambda j, lo, n: jnp.clip(j - lo, 0, n - 1)
    return pl.pallas_call(
        _inproj_kernel,
        grid=(m // tm, _NJ),
        in_specs=[pl.BlockSpec((tm, d), lambda i, j: (i, 0)),
                  pl.BlockSpec((1, d), lambda i, j: (0, 0)),
                  pl.BlockSpec((None, mod_rows, d), mod_map),
                  pl.BlockSpec((None, mod_rows, d), mod_map),
                  pl.BlockSpec((d, _TN), lambda i, j: (0, j)),
                  pl.BlockSpec((d, 2 * LANES), lambda i, j: (0, 0))],
        out_specs=[pl.BlockSpec((tm, _TN), lambda i, j: (i, clampj(j, 0, _J_KV))),
                   pl.BlockSpec((tm * _KV_STREAMS, HEAD_DIM), lambda i, j: (i, 0)),
                   pl.BlockSpec((tm, _TN), lambda i, j: (i, clampj(j, _J_KV, _J_REST - _J_KV))),
                   pl.BlockSpec((tm, _TN), lambda i, j: (i, clampj(j, _J_REST, _NJ - _J_REST))),
                   pl.BlockSpec((tm, _TN), lambda i, j: (i, 0)),
                   pl.BlockSpec((tm, 2 * LANES), lambda i, j: (i, 0))],
        out_shape=[jax.ShapeDtypeStruct((m, Q_COLS), BF16),
                   jax.ShapeDtypeStruct((m * _KV_STREAMS, HEAD_DIM), F32),
                   jax.ShapeDtypeStruct((m, KV_COLS), BF16),
                   jax.ShapeDtypeStruct((m, _REST_COLS), F32),
                   jax.ShapeDtypeStruct((m, WIN_COLS), BF16),
                   jax.ShapeDtypeStruct((m, 2 * LANES), F32)],
        scratch_shapes=[pltpu.VMEM((tm, d), BF16)],
        compiler_params=_params("parallel", "arbitrary"),
        name="inproj",
    )(x, g, scale3, shift3, w_main, w_gate)


def _compress_rows(load_rows, nch, w1, cconst, w2, b2):
    lhs = jnp.concatenate([load_rows(l) for l in range(CMP_STRIDE)], axis=1).astype(BF16)
    y = _dot(lhs, w1)
    second = pltpu.roll(y[:, HEAD_DIM:], shift=nch - 1, axis=0)
    h = _silu(y[:, :HEAD_DIM] + second + cconst)
    out = _dot(h.astype(BF16), w2) + b2
    row = lax.broadcasted_iota(jnp.int32, out.shape, 0)
    return jnp.where(row < nch - 1, out, 0.0)


def _compress_kernel(x_ref, w1_ref, cc_ref, w2_ref, b2_ref, o_ref, *, nch, ncp):
    for c in range(2 * N_KV_HEADS):
        s = c // N_KV_HEADS

        def load_rows(l, c=c):
            return x_ref[pl.ds(l * _KV_STREAMS + c, nch, stride=CMP_STRIDE * _KV_STREAMS), :]

        out = _compress_rows(load_rows, nch, w1_ref[s], cc_ref[s], w2_ref[s], b2_ref[s])
        if ncp > nch:
            out = jnp.concatenate([out, jnp.zeros((ncp - nch, HEAD_DIM), F32)], axis=0)
        o_ref[c] = out.astype(BF16)


def _compress(kv_streams, w1cat, cconst, w2b, b2, ncp):
    b, rows, _ = kv_streams.shape
    t = rows // _KV_STREAMS
    nch = t // CMP_STRIDE
    return pl.pallas_call(
        functools.partial(_compress_kernel, nch=nch, ncp=ncp),
        grid=(b,),
        in_specs=[pl.BlockSpec((None, rows, HEAD_DIM), lambda i: (i, 0, 0)),
                  pl.BlockSpec((2, 16 * HEAD_DIM, 2 * HEAD_DIM), lambda i: (0, 0, 0)),
                  pl.BlockSpec((2, 1, HEAD_DIM), lambda i: (0, 0, 0)),
                  pl.BlockSpec((2, HEAD_DIM, HEAD_DIM), lambda i: (0, 0, 0)),
                  pl.BlockSpec((2, 1, HEAD_DIM), lambda i: (0, 0, 0))],
        out_specs=pl.BlockSpec((None, 4, ncp, HEAD_DIM), lambda i: (i, 0, 0, 0)),
        out_shape=jax.ShapeDtypeStruct((b, 4, ncp, HEAD_DIM), BF16),
        compiler_params=_params("parallel"),
        name="compress",
    )(kv_streams, w1cat, cconst, w2b, b2)


def _cmp_branch(q_heads, kc, vc, qpos, n_cmp):
    rows = q_heads[0].shape[0]
    nio = lax.broadcasted_iota(jnp.int32, (rows, kc.shape[0]), 1)
    mask = (nio < n_cmp) & (nio * CMP_STRIDE + (CMP_BLOCK - 1) <= qpos)
    outs, probs = [], []
    for q in q_heads:
        p = _masked_softmax(_dot_nt(q, kc) * SM_SCALE, mask)
        outs.append(_dot(p.astype(BF16), vc))
        probs.append(p)
    return outs, probs


def _select_mask(psum, ov, qpos, n_sel):
    hi = psum.astype(BF16)
    lo = (psum - hi.astype(F32)).astype(BF16)
    imp = _dot(hi, ov) + _dot(lo, ov)
    rows = imp.shape[0]
    jio = lax.broadcasted_iota(jnp.int32, (rows, LANES), 1)
    allowed = (jio * SEL_BLOCK <= qpos) & (jio < n_sel)
    cur = qpos // SEL_BLOCK
    forced = (jio == 0) | (jio == cur) | (jio == cur - 1)
    score = jnp.where(allowed, imp + jnp.where(forced, FORCE_BONUS, 0.0), NEG_INF)
    rank = jnp.zeros((rows, LANES), F32)
    for i in range(n_sel):
        si = jnp.broadcast_to(score[:, i:i + 1], (rows, LANES))
        tie = jnp.where(jio > i, 1.0, 0.0)
        rank = rank + jnp.where(si > score, 1.0, jnp.where(si == score, tie, 0.0))
    return (rank < float(min(N_SELECT, n_sel))) & allowed


def _select_mask_t(psum, ovt, qpos_row, n_sel):
    hi = psum.astype(BF16)
    lo = (psum - hi.astype(F32)).astype(BF16)
    imp = _dot_nt(ovt, hi) + _dot_nt(ovt, lo)
    sr, rows = imp.shape
    jio = lax.broadcasted_iota(jnp.int32, (sr, rows), 0)
    allowed = (jio * SEL_BLOCK <= qpos_row) & (jio < n_sel)
    cur = qpos_row // SEL_BLOCK
    forced = (jio == 0) | (jio == cur) | (jio == cur - 1)
    score = jnp.where(allowed, imp + jnp.where(forced, FORCE_BONUS, 0.0), NEG_INF)
    rank = jnp.zeros((sr, rows), F32)
    for i in range(n_sel):
        si = jnp.broadcast_to(score[i:i + 1, :], (sr, rows))
        tie = jnp.where(jio > i, 1.0, 0.0)
        rank = rank + jnp.where(si > score, 1.0, jnp.where(si == score, tie, 0.0))
    sel_t = jnp.where((rank < float(min(N_SELECT, n_sel))) & allowed, 1.0, 0.0)
    sel_t = jnp.concatenate([sel_t, jnp.zeros((LANES - sr, rows), F32)], axis=0)
    return sel_t.T


def _biased_softmax(s, bias):
    s = s + bias
    e = jnp.exp(s - jnp.max(s, axis=-1, keepdims=True))
    return e / jnp.maximum(jnp.sum(e, axis=-1, keepdims=True), 1e-30)


def _attn_p_kernel(q_ref, kc_ref, vc_ref, ks_ref, vs_ref, kw_ref, vw_ref, gate_ref, z_ref, ovt_ref, e_ref,
                   o_ref, m_s, l_s, acc_s, pre_s, *, tq, tk, n_cmp, n_sel):
    qi = pl.program_id(2)
    q0 = qi * tq
    qpos = q0 + lax.broadcasted_iota(jnp.int32, (tq, 1), 0)
    heads = range(HEADS_PER_GROUP)
    q_heads = [q_ref[:, r * HEAD_DIM:(r + 1) * HEAD_DIM] for r in heads]
    gates = _sigmoid(gate_ref[...])
    g_of = lambda br, r: jnp.broadcast_to(
        gates[:, br * HEADS_PER_GROUP + r:br * HEADS_PER_GROUP + r + 1], (tq, HEAD_DIM))

    kc = kc_ref[...]
    vc = vc_ref[...]
    w0 = pl.multiple_of(jnp.maximum(q0 - WINDOW, 0), LANES)
    k_w = kw_ref[pl.ds(w0, WINDOW + tq), :]
    v_w = vw_ref[pl.ds(w0, WINDOW + tq), :]
    dist = qpos - (w0 + lax.broadcasted_iota(jnp.int32, (tq, WINDOW + tq), 1))
    wbias = jnp.where((dist >= 0) & (dist <= WINDOW), 0.0, NEG_INF)
    nio = lax.broadcasted_iota(jnp.int32, (tq, kc.shape[0]), 1)
    cmask = (nio < n_cmp) & (nio * CMP_STRIDE + (CMP_BLOCK - 1) <= qpos)

    s_cmp = [_dot_nt(q_heads[r], kc) * SM_SCALE for r in heads]
    s_win = [_dot_nt(q_heads[r], k_w) * SM_SCALE for r in heads]
    p_cmp = [_masked_softmax(s_cmp[r], cmask) for r in heads]
    psum = (p_cmp[0] + p_cmp[1]) + (p_cmp[2] + p_cmp[3])
    p_win = [_biased_softmax(s_win[r], wbias) for r in heads]
    qpos_row = q0 + lax.broadcasted_iota(jnp.int32, (1, tq), 1)
    selb = _select_mask_t(psum, ovt_ref[...], qpos_row, n_sel).astype(BF16)
    for r in heads:
        pre_s[r] = (g_of(0, r) * _dot(p_cmp[r].astype(BF16), vc)
                    + g_of(2, r) * _dot(p_win[r].astype(BF16), v_w))

    m_s[...] = jnp.full(m_s.shape, NEG_INF, F32)
    l_s[...] = jnp.zeros(l_s.shape, F32)
    acc_s[...] = jnp.zeros(acc_s.shape, F32)
    n_kt = (q0 + tq - 1) // tk + 1

    def sweep(kt, carry):
        k0 = pl.multiple_of(kt * tk, tk)
        k_t = ks_ref[pl.ds(k0, tk), :]
        v_t = vs_ref[pl.ds(k0, tk), :]
        kpos = k0 + lax.broadcasted_iota(jnp.int32, (tq, tk), 1)
        bias = jnp.where((_dot(selb, e_ref[kt]) > 0.5) & (kpos <= qpos), 0.0, NEG_INF)
        s = [_dot_nt(q_heads[r], k_t) * SM_SCALE + bias for r in heads]
        m_prev = [m_s[r] for r in heads]
        m_new = [jnp.maximum(m_prev[r], jnp.max(s[r], axis=-1, keepdims=True)) for r in heads]
        p = [jnp.exp(s[r] - jnp.concatenate([m_new[r]] * (tk // LANES), axis=1)) for r in heads]
        alpha = [jnp.exp(m_prev[r] - m_new[r]) for r in heads]
        for r in heads:
            l_s[r] = alpha[r] * l_s[r] + jnp.sum(p[r], axis=-1, keepdims=True)
            acc_s[r] = alpha[r] * acc_s[r] + _dot(p[r].astype(BF16), v_t)
            m_s[r] = m_new[r]
        return carry

    lax.fori_loop(0, n_kt, sweep, 0)

    for r in heads:
        o_slc = acc_s[r] / jnp.maximum(l_s[r], 1e-30)
        o = pre_s[r] + g_of(1, r) * o_slc
        o_ref[:, r * HEAD_DIM:(r + 1) * HEAD_DIM] = (
            o * _silu(z_ref[:, r * HEAD_DIM:(r + 1) * HEAD_DIM])).astype(BF16)


def _attn_p(q_bf, kcv, kv_bf3, win_bf3, gate, rest, ovt, e_p, b, t, n_cmp, n_sel, tq=128, tk=256):
    assert t >= WINDOW + tq and t % tk == 0 and tq == LANES
    nq = t // tq
    ncp = kcv.shape[2]
    sr = ovt.shape[0]
    gw = HEADS_PER_GROUP * HEAD_DIM
    row = lambda bi, g, qi: bi * nq + qi
    return pl.pallas_call(
        functools.partial(_attn_p_kernel, tq=tq, tk=tk, n_cmp=n_cmp, n_sel=n_sel),
        grid=(b, N_KV_HEADS, nq),
        in_specs=[pl.BlockSpec((tq, gw), lambda bi, g, qi: (row(bi, g, qi), g)),
                  pl.BlockSpec((None, None, ncp, HEAD_DIM), lambda bi, g, qi: (bi, g, 0, 0)),
                  pl.BlockSpec((None, None, ncp, HEAD_DIM), lambda bi, g, qi: (bi, N_KV_HEADS + g, 0, 0)),
                  pl.BlockSpec((None, t, HEAD_DIM), lambda bi, g, qi: (bi, 0, 2 * N_KV_HEADS + g)),
                  pl.BlockSpec((None, t, HEAD_DIM), lambda bi, g, qi: (bi, 0, 3 * N_KV_HEADS + g)),
                  pl.BlockSpec((None, t, HEAD_DIM), lambda bi, g, qi: (bi, 0, g)),
                  pl.BlockSpec((None, t, HEAD_DIM), lambda bi, g, qi: (bi, 0, N_KV_HEADS + g)),
                  pl.BlockSpec((tq, LANES), lambda bi, g, qi: (row(bi, g, qi), g)),
                  pl.BlockSpec((tq, gw), lambda bi, g, qi: (row(bi, g, qi), g)),
                  pl.BlockSpec((sr, ncp), lambda bi, g, qi: (0, 0)),
                  pl.BlockSpec((t // tk, LANES, tk), lambda bi, g, qi: (0, 0, 0))],
        out_specs=pl.BlockSpec((tq, gw), lambda bi, g, qi: (row(bi, g, qi), g)),
        out_shape=jax.ShapeDtypeStruct((b * t, NSA_WIDTH), BF16),
        scratch_shapes=[pltpu.VMEM((HEADS_PER_GROUP, tq, LANES), F32),
                        pltpu.VMEM((HEADS_PER_GROUP, tq, LANES), F32),
                        pltpu.VMEM((HEADS_PER_GROUP, tq, HEAD_DIM), F32),
                        pltpu.VMEM((HEADS_PER_GROUP, tq, HEAD_DIM), F32)],
        compiler_params=_params("parallel", "parallel", "arbitrary"),
        name="attn_p",
    )(q_bf, kcv, kcv, kv_bf3, kv_bf3, win_bf3, win_bf3, gate, rest, ovt, e_p)


def _attn_s_kernel(pt_ref, cache_ref, wst_ref, q_ref, kvn_ref, winn_ref, gate_ref, z_ref, winc_ref,
                   w1_ref, cc_ref, w2_ref, b2_ref, ov_ref, e_ref, perm_ref, o_ref, wout_ref,
                   kvbuf, wbuf, wnew, kcs, ksl, kwn, sem, wsem, osem,
                   *, n_pages, page, n_cmp, n_sel, pos0, t_new):
    i = pl.program_id(0)
    nb = pl.num_programs(0) - 1
    past = n_pages * page
    nch = past // CMP_STRIDE
    rows = HEADS_PER_GROUP * SAMPLE_ROWS
    n_ws = N_WIN_SLOTS * N_KV_HEADS
    state_rows = WINDOW * n_ws
    new_rows = t_new * n_ws
    prow = page * _KV_STREAMS
    c_seq = jnp.minimum(i, nb - 1)
    a_seq = jnp.maximum(i - 1, 0)
    cs = c_seq % 2
    asl = a_seq % 2

    def page_copy(seq, p, slot):
        return pltpu.make_async_copy(cache_ref.at[pt_ref[seq, p]], kvbuf.at[slot, pl.ds(p * prow, prow)],
                                     sem.at[slot])

    def win_copy(seq, slot):
        return pltpu.make_async_copy(wst_ref.at[seq], wbuf.at[slot], wsem.at[slot])

    def fetch(seq, slot):
        for p in range(n_pages):
            page_copy(seq, p, slot).start()
        win_copy(seq, slot).start()

    def state_copies(seq, slot):
        kept = pltpu.make_async_copy(wbuf.at[slot, pl.ds(new_rows, state_rows - new_rows)],
                                     wout_ref.at[seq, pl.ds(0, state_rows - new_rows)], osem.at[slot])
        fresh = pltpu.make_async_copy(wnew.at[slot], wout_ref.at[seq, pl.ds(state_rows - new_rows, new_rows)],
                                      osem.at[slot])
        return kept, fresh

    @pl.when(i == 0)
    def _():
        fetch(0, 0)
        kcs[0] = jnp.zeros(kcs.shape[1:], BF16)
        ksl[0] = jnp.zeros(ksl.shape[1:], BF16)
        kwn[0] = jnp.zeros(kwn.shape[1:], BF16)

    @pl.when(i >= 1)
    def _():
        for cp in state_copies(i - 1, (i - 1) % 2):
            cp.wait()

    @pl.when(i + 1 < nb)
    def _():
        fetch(i + 1, (i + 1) % 2)

    @pl.when(i < nb)
    def _():
        for p in range(n_pages):
            page_copy(i, p, i % 2).wait()
        win_copy(i, i % 2).wait()
        for t in range(t_new):
            for c in range(n_ws):
                wnew[i % 2, pl.ds(t * n_ws + c, 1), :] = winc_ref[pl.ds(t, 1), c * HEAD_DIM:(c + 1) * HEAD_DIM]
        for cp in state_copies(i, i % 2):
            cp.start()

    qpos = pos0 + lax.broadcasted_iota(jnp.int32, (rows, 1), 0) % SAMPLE_ROWS
    kvn = kvn_ref[...].astype(F32)
    qf = q_ref[...].astype(F32)
    gates = _sigmoid(gate_ref[...])
    zero_tail = jnp.zeros((LANES - 2 * SUBLANES, HEAD_DIM), BF16)
    pad8 = jnp.zeros((SUBLANES, HEAD_DIM), F32)

    def with_new_rows(old_bf, new_f32):
        new16 = jnp.concatenate([new_f32, pad8], axis=0).astype(BF16)
        return jnp.concatenate([old_bf, new16, zero_tail], axis=0)

    def stack_heads(x8):
        return jnp.concatenate([x8] * HEADS_PER_GROUP, axis=0)

    groups = range(N_KV_HEADS)
    nk = past + LANES
    kpos = lax.broadcasted_iota(jnp.int32, (rows, nk), 1)
    wpos = (pos0 - WINDOW) + lax.broadcasted_iota(jnp.int32, (rows, WINDOW + LANES), 1)
    dist = qpos - wpos
    wmask = (dist >= 0) & (dist <= WINDOW) & (wpos >= 0)
    nio = lax.broadcasted_iota(jnp.int32, (rows, nch), 1)
    cmask = (nio < n_cmp) & (nio * CMP_STRIDE + (CMP_BLOCK - 1) <= qpos)

    qgs, s_cmp, s_slc, s_win, v_alls, vw_alls = [], [], [], [], [], []
    for g in groups:
        heads = [qf[:, (g * HEADS_PER_GROUP + r) * HEAD_DIM:(g * HEADS_PER_GROUP + r + 1) * HEAD_DIM]
                 for r in range(HEADS_PER_GROUP)]
        qgs.append(jnp.concatenate(heads, axis=0).astype(BF16))
    for g in groups:
        s_cmp.append(_dot_nt(qgs[g], kcs[asl, g]) * SM_SCALE)
    for g in groups:
        c_k = (2 * N_KV_HEADS + g) * HEAD_DIM
        c_v = (3 * N_KV_HEADS + g) * HEAD_DIM
        k_all = with_new_rows(ksl[asl, g], kvn[:, c_k:c_k + HEAD_DIM])
        v_alls.append(with_new_rows(ksl[asl, N_KV_HEADS + g], kvn[:, c_v:c_v + HEAD_DIM]))
        s_slc.append(_dot_nt(qgs[g], k_all) * SM_SCALE)
    for g in groups:
        wk = g * HEAD_DIM
        wv = (N_KV_HEADS + g) * HEAD_DIM
        kw_all = with_new_rows(kwn[asl, g], winn_ref[:, wk:wk + HEAD_DIM])
        vw_alls.append(with_new_rows(kwn[asl, N_KV_HEADS + g], winn_ref[:, wv:wv + HEAD_DIM]))
        s_win.append(_dot_nt(qgs[g], kw_all) * SM_SCALE)

    vcs = [kcs[asl, N_KV_HEADS + g] for g in groups]

    perm = perm_ref[...]
    blk = LANES
    cpb = blk // CMP_STRIDE
    stream = lambda c: kvbuf[cs, pl.ds(c, past, stride=_KV_STREAMS), :].astype(BF16)

    def slot_chunks(s):
        xs = jnp.concatenate([stream(s * N_KV_HEADS + g) for g in groups], axis=1)
        out = [[] for _ in groups]
        for k in range(past // blk):
            y = _dot(perm, xs[k * blk:(k + 1) * blk])
            for g in groups:
                out[g].append(jnp.concatenate(
                    [y[l * cpb:(l + 1) * cpb, g * HEAD_DIM:(g + 1) * HEAD_DIM] for l in range(CMP_STRIDE)], axis=1))
        return jnp.concatenate([jnp.concatenate(o, axis=0) for o in out], axis=0).astype(BF16)

    def compress_slot(s, lhs):
        y = _dot(lhs, w1_ref[s])
        second = pltpu.roll(y[:, HEAD_DIM:], shift=lhs.shape[0] - 1, axis=0)
        h = _silu(y[:, :HEAD_DIM] + second + cc_ref[s])
        out = _dot(h.astype(BF16), w2_ref[s]) + b2_ref[s]
        row = lax.broadcasted_iota(jnp.int32, out.shape, 0) % nch
        out = jnp.where(row < nch - 1, out, 0.0).astype(BF16)
        for g in groups:
            kcs[cs, s * N_KV_HEADS + g] = out[g * nch:(g + 1) * nch]

    lhs0 = slot_chunks(0)

    p_cmps = [_masked_softmax(s_cmp[g], cmask) for g in groups]
    psums = [(p[0:8] + p[8:16]) + (p[16:24] + p[24:32]) for p in p_cmps]
    sels = [_select_mask(psums[g], ov_ref[...], qpos[0:SAMPLE_ROWS], n_sel) for g in groups]
    selbs = [jnp.where(s, 1.0, 0.0).astype(BF16) for s in sels]

    compress_slot(0, lhs0)
    lhs1 = slot_chunks(1)

    smasks = [(stack_heads(_dot(selbs[g], e_ref[...])) > 0.5) & (kpos <= qpos) for g in groups]
    o_cmps = [_dot(p_cmps[g].astype(BF16), vcs[g]) for g in groups]
    p_wins = [_masked_softmax(s_win[g], wmask) for g in groups]
    o_wins = [_dot(p_wins[g].astype(BF16), vw_alls[g]) for g in groups]
    p_slcs = [_masked_softmax(s_slc[g], smasks[g]) for g in groups]
    o_slcs = [_dot(p_slcs[g].astype(BF16), v_alls[g]) for g in groups]

    compress_slot(1, lhs1)
    for k in range(2 * N_KV_HEADS):
        ksl[cs, k] = stream(2 * N_KV_HEADS + k)
    for k in range(n_ws):
        kwn[cs, k] = wbuf[cs, pl.ds(k, WINDOW, stride=n_ws), :].astype(BF16)

    for g in groups:
        o_cmp, o_slc, o_win = o_cmps[g], o_slcs[g], o_wins[g]
        for r in range(HEADS_PER_GROUP):
            h = g * HEADS_PER_GROUP + r
            rs = slice(r * SAMPLE_ROWS, (r + 1) * SAMPLE_ROWS)
            g_of = lambda br: jnp.broadcast_to(
                gates[:, g * LANES + br * HEADS_PER_GROUP + r:g * LANES + br * HEADS_PER_GROUP + r + 1],
                (SAMPLE_ROWS, HEAD_DIM))
            o = g_of(0) * o_cmp[rs] + g_of(1) * o_slc[rs] + g_of(2) * o_win[rs]
            o_ref[:, h * HEAD_DIM:(h + 1) * HEAD_DIM] = (
                o * _silu(z_ref[:, h * HEAD_DIM:(h + 1) * HEAD_DIM])).astype(BF16)


def _attn_s(page_table, cache3, wst3, q3, kvn3, rest3, gate3, w1cat, cconst, w2b, b2, ov, e_s,
            n_cmp, n_sel, pos0, t_new):
    nb, n_pages = page_table.shape
    page = cache3.shape[1] // _KV_STREAMS
    past = n_pages * page
    nch = past // CMP_STRIDE
    n_ws = N_WIN_SLOTS * N_KV_HEADS
    assert wst3.shape[1:] == (WINDOW * n_ws, HEAD_DIM) and past % LANES == 0
    ncp = ov.shape[0]
    prev = lambda i: jnp.maximum(i - 1, 0)
    grid_spec = pltpu.PrefetchScalarGridSpec(
        num_scalar_prefetch=1,
        grid=(nb + 1,),
        in_specs=[pl.BlockSpec(memory_space=pl.ANY),
                  pl.BlockSpec(memory_space=pl.ANY),
                  pl.BlockSpec((None, SAMPLE_ROWS, Q_COLS), lambda i, pt: (prev(i), 0, 0)),
                  pl.BlockSpec((None, SAMPLE_ROWS, KV_COLS), lambda i, pt: (prev(i), 0, 0)),
                  pl.BlockSpec((None, SAMPLE_ROWS, WIN_COLS), lambda i, pt: (prev(i), 0, _REST_WIN)),
                  pl.BlockSpec((None, SAMPLE_ROWS, 2 * LANES), lambda i, pt: (prev(i), 0, 0)),
                  pl.BlockSpec((None, SAMPLE_ROWS, NSA_WIDTH), lambda i, pt: (prev(i), 0, _REST_Z)),
                  pl.BlockSpec((None, SAMPLE_ROWS, WIN_COLS), lambda i, pt: (jnp.minimum(i, nb - 1), 0, _REST_WIN)),
                  pl.BlockSpec((2, 16 * HEAD_DIM, 2 * HEAD_DIM), lambda i, pt: (0, 0, 0)),
                  pl.BlockSpec((2, 1, HEAD_DIM), lambda i, pt: (0, 0, 0)),
                  pl.BlockSpec((2, HEAD_DIM, HEAD_DIM), lambda i, pt: (0, 0, 0)),
                  pl.BlockSpec((2, 1, HEAD_DIM), lambda i, pt: (0, 0, 0)),
                  pl.BlockSpec((ncp, LANES), lambda i, pt: (0, 0)),
                  pl.BlockSpec((LANES, past + LANES), lambda i, pt: (0, 0)),
                  pl.BlockSpec((LANES, LANES), lambda i, pt: (0, 0))],
        out_specs=[pl.BlockSpec((None, SAMPLE_ROWS, NSA_WIDTH), lambda i, pt: (prev(i), 0, 0)),
                   pl.BlockSpec(memory_space=pl.ANY)],
        scratch_shapes=[pltpu.VMEM((2, past * _KV_STREAMS, HEAD_DIM), F32),
                        pltpu.VMEM((2, WINDOW * n_ws, HEAD_DIM), F32),
                        pltpu.VMEM((2, t_new * n_ws, HEAD_DIM), F32),
                        pltpu.VMEM((2, 2 * N_KV_HEADS, nch, HEAD_DIM), BF16),
                        pltpu.VMEM((2, 2 * N_KV_HEADS, past, HEAD_DIM), BF16),
                        pltpu.VMEM((2, n_ws, WINDOW, HEAD_DIM), BF16),
                        pltpu.SemaphoreType.DMA((2,)),
                        pltpu.SemaphoreType.DMA((2,)),
                        pltpu.SemaphoreType.DMA((2,))],
    )
    return pl.pallas_call(
        functools.partial(_attn_s_kernel, n_pages=n_pages, page=page, n_cmp=n_cmp, n_sel=n_sel, pos0=pos0,
                          t_new=t_new),
        grid_spec=grid_spec,
        out_shape=[jax.ShapeDtypeStruct((nb, SAMPLE_ROWS, NSA_WIDTH), BF16),
                   jax.ShapeDtypeStruct((nb, WINDOW * n_ws, HEAD_DIM), F32)],
        compiler_params=_params("arbitrary"),
        name="attn_s",
    )(page_table, cache3, wst3, q3, kvn3, rest3, gate3, rest3, rest3, w1cat, cconst, w2b, b2, ov, e_s,
      _chunk_permutation())


_CONV_RC = 64
_CONV_LC = 256


def _depthwise_conv(uext, w_ref, db_ref, ybuf, tt, row0=0):
    off = CONV_HALO - (CONV_WIDTH - 1)
    uext[CONV_HALO + tt:CONV_HALO + tt + SUBLANES] = jnp.zeros((SUBLANES, CONV_DIM), F32)
    rc = min(_CONV_RC, tt)
    for r0 in range(0, tt, rc):
        for c0 in range(0, CONV_DIM, _CONV_LC):
            lanes = slice(c0, c0 + _CONV_LC)
            acc = jnp.broadcast_to(db_ref[:, lanes], (rc, _CONV_LC))
            for r in range(SUBLANES):
                z = None
                for a in range((CONV_WIDTH + off) // SUBLANES + 1):
                    k = SUBLANES * a + r - off
                    if 0 <= k < CONV_WIDTH:
                        term = w_ref[k:k + 1, lanes] * uext[r0 + SUBLANES * a:r0 + SUBLANES * a + rc + SUBLANES, lanes]
                        z = term if z is None else z + term
                acc = acc + z[r:r + rc]
            ybuf[row0 + r0:row0 + r0 + rc, lanes] = acc


def _conv_kernel(a_ref, gl_ref, ah_ref, glh_ref, init_ref, zc_ref, w_ref, db_ref, lg_ref, lb_ref,
                 cv_ref, st_ref, uext, ybuf, *, tt, t_valid):
    ti = pl.program_id(1)
    nt = pl.num_programs(1)
    halo = ah_ref[...] * _sigmoid(glh_ref[...])
    uext[0:CONV_HALO] = jnp.where(ti == 0, init_ref[...], halo)
    uext[CONV_HALO:CONV_HALO + tt] = a_ref[...] * _sigmoid(gl_ref[...])
    _depthwise_conv(uext, w_ref, db_ref, ybuf, tt)

    y = ybuf[...]
    mu = jnp.mean(y, axis=-1, keepdims=True)
    var = jnp.mean(jnp.square(y - mu), axis=-1, keepdims=True)
    ln = (y - mu) * lax.rsqrt(var + NORM_EPS) * lg_ref[...] + lb_ref[...]
    cv_ref[...] = (_silu(ln) * _silu(zc_ref[...])).astype(BF16)

    @pl.when(ti == nt - 1)
    def _():
        st_ref[...] = uext[pl.ds(t_valid, CONV_HALO), :]


def _conv(rest, u_init, w_pad, db, lg, lb, b, t, tt, t_valid):
    assert tt % CONV_HALO == 0 and t % tt == 0
    nt = t // tt
    hb = tt // CONV_HALO
    row = lambda bi, ti: bi * nt + ti
    halo_map = lambda col: (lambda bi, ti: (jnp.maximum(row(bi, ti) * hb - 1, 0), col))
    halo_rows = CONV_HALO
    vec = lambda: pl.BlockSpec((1, CONV_DIM), lambda bi, ti: (0, 0))
    return pl.pallas_call(
        functools.partial(_conv_kernel, tt=tt, t_valid=t_valid),
        grid=(b, nt),
        in_specs=[pl.BlockSpec((tt, CONV_DIM), lambda bi, ti: (row(bi, ti), _REST_A)),
                  pl.BlockSpec((tt, CONV_DIM), lambda bi, ti: (row(bi, ti), _REST_GL)),
                  pl.BlockSpec((halo_rows, CONV_DIM), halo_map(_REST_A)),
                  pl.BlockSpec((halo_rows, CONV_DIM), halo_map(_REST_GL)),
                  pl.BlockSpec((None, CONV_HALO, CONV_DIM), lambda bi, ti: (bi, 0, 0)),
                  pl.BlockSpec((tt, CONV_DIM), lambda bi, ti: (row(bi, ti), _REST_ZC)),
                  pl.BlockSpec((CONV_HALO, CONV_DIM), lambda bi, ti: (0, 0)),
                  vec(), vec(), vec()],
        out_specs=[pl.BlockSpec((tt, CONV_DIM), lambda bi, ti: (row(bi, ti), 0)),
                   pl.BlockSpec((None, CONV_HALO, CONV_DIM), lambda bi, ti: (bi, 0, 0))],
        out_shape=[jax.ShapeDtypeStruct((b * t, CONV_DIM), BF16),
                   jax.ShapeDtypeStruct((b, CONV_HALO, CONV_DIM), F32)],
        scratch_shapes=[pltpu.VMEM((CONV_HALO + tt + SUBLANES, CONV_DIM), F32),
                        pltpu.VMEM((tt, CONV_DIM), F32)],
        compiler_params=_params("parallel", "arbitrary"),
        name="conv",
    )(rest, rest, rest, rest, u_init, rest, w_pad, db, lg, lb)


def _conv_short_kernel(a_ref, gl_ref, init_ref, zc_ref, w_ref, db_ref, lg_ref, lb_ref,
                       cv_ref, st_ref, uext, ybuf, *, tt, t_valid, ns):
    u = a_ref[...] * _sigmoid(gl_ref[...])
    for s in range(ns):
        ue = uext.at[s]
        ue[0:CONV_HALO] = init_ref[s]
        ue[CONV_HALO:CONV_HALO + tt] = u[s * tt:(s + 1) * tt]
        _depthwise_conv(ue, w_ref, db_ref, ybuf, tt, row0=s * tt)
        st_ref[s] = ue[pl.ds(t_valid, CONV_HALO), :]
    y = ybuf[...]
    mu = jnp.mean(y, axis=-1, keepdims=True)
    var = jnp.mean(jnp.square(y - mu), axis=-1, keepdims=True)
    ln = (y - mu) * lax.rsqrt(var + NORM_EPS) * lg_ref[...] + lb_ref[...]
    cv_ref[...] = (_silu(ln) * _silu(zc_ref[...])).astype(BF16)


def _conv_short(rest, u_init, w_pad, db, lg, lb, b, tt, t_valid, ns):
    assert b % ns == 0 and tt % SUBLANES == 0
    rows = ns * tt
    vec = lambda: pl.BlockSpec((1, CONV_DIM), lambda i: (0, 0))
    return pl.pallas_call(
        functools.partial(_conv_short_kernel, tt=tt, t_valid=t_valid, ns=ns),
        grid=(b // ns,),
        in_specs=[pl.BlockSpec((rows, CONV_DIM), lambda i: (i, _REST_A)),
                  pl.BlockSpec((rows, CONV_DIM), lambda i: (i, _REST_GL)),
                  pl.BlockSpec((ns, CONV_HALO, CONV_DIM), lambda i: (i, 0, 0)),
                  pl.BlockSpec((rows, CONV_DIM), lambda i: (i, _REST_ZC)),
                  pl.BlockSpec((CONV_HALO, CONV_DIM), lambda i: (0, 0)),
                  vec(), vec(), vec()],
        out_specs=[pl.BlockSpec((rows, CONV_DIM), lambda i: (i, 0)),
                   pl.BlockSpec((ns, CONV_HALO, CONV_DIM), lambda i: (i, 0, 0))],
        out_shape=[jax.ShapeDtypeStruct((b * tt, CONV_DIM), BF16),
                   jax.ShapeDtypeStruct((b, CONV_HALO, CONV_DIM), F32)],
        scratch_shapes=[pltpu.VMEM((ns, CONV_HALO + tt + SUBLANES, CONV_DIM), F32),
                        pltpu.VMEM((rows, CONV_DIM), F32)],
        compiler_params=_params("parallel"),
        name="conv_short",
    )(rest, rest, u_init, rest, w_pad, db, lg, lb)


def _outproj_kernel(on_ref, cv_ref, w_ref, x_ref, gate_ref, g_ref, y_ref):
    mix = _dot(on_ref[...], w_ref[0:NSA_WIDTH, :]) + _dot(cv_ref[...], w_ref[NSA_WIDTH:, :])
    nrm = mix * lax.rsqrt(jnp.mean(mix * mix, axis=-1, keepdims=True) + NORM_EPS) * g_ref[...]
    y_ref[...] = x_ref[...] + gate_ref[...] * nrm


def _outproj(o_nsa, cv, w_out_b, x, gate3, g_post, tm, rows_per_mod):
    m, d = x.shape
    mod_rows = gate3.shape[1]
    tiles_per_mod = rows_per_mod // tm
    return pl.pallas_call(
        _outproj_kernel,
        grid=(m // tm,),
        in_specs=[pl.BlockSpec((tm, NSA_WIDTH), lambda i: (i, 0)),
                  pl.BlockSpec((tm, CONV_DIM), lambda i: (i, 0)),
                  pl.BlockSpec((d, d), lambda i: (0, 0)),
                  pl.BlockSpec((tm, d), lambda i: (i, 0)),
                  pl.BlockSpec((None, mod_rows, d), lambda i: (i // tiles_per_mod, 0, 0)),
                  pl.BlockSpec((1, d), lambda i: (0, 0))],
        out_specs=pl.BlockSpec((tm, d), lambda i: (i, 0)),
        out_shape=jax.ShapeDtypeStruct((m, d), F32),
        compiler_params=_params("parallel"),
        name="outproj",
    )(o_nsa, cv, w_out_b, x, gate3, g_post)


def _overlap_matrix(n_cmp, n_sel, ncp):
    ci = np.arange(ncp)[:, None] * CMP_STRIDE
    sj = np.arange(LANES)[None, :] * SEL_BLOCK
    ov = (ci < sj + SEL_BLOCK) & (ci + CMP_BLOCK > sj)
    ov &= (np.arange(ncp)[:, None] < n_cmp) & (np.arange(LANES)[None, :] < n_sel)
    return jnp.asarray(ov.astype(np.float32), dtype=BF16)


def _overlap_matrix_t(n_cmp, n_sel, ncp):
    sr = _round_up(n_sel, 2 * SUBLANES)
    return _overlap_matrix(n_cmp, n_sel, ncp).T[:sr]


def _chunk_permutation():
    p = np.zeros((LANES, LANES), np.float32)
    for j in range(LANES // CMP_STRIDE):
        for l in range(CMP_STRIDE):
            p[l * (LANES // CMP_STRIDE) + j, CMP_STRIDE * j + l] = 1.0
    return jnp.asarray(p, dtype=BF16)


def _expand_matrix(n_keys):
    e = (np.arange(n_keys)[None, :] // SEL_BLOCK) == np.arange(LANES)[:, None]
    return jnp.asarray(e.astype(np.float32), dtype=BF16)


def _round_up(x, m):
    return -(-x // m) * m


def kernel(x_prompt, x_sample, c_prompt, c_sample, cache_kv, page_table, state_win_kv, state_conv, w_ada, b_ada, norm_pre, norm_post, w_in, cmp_pe, cmp_w1, cmp_b1, cmp_w2, cmp_b2, conv_dw, conv_db, conv_ln_g, conv_ln_b, w_out):
    assert w_ada.shape[0] == 1, "single-layer trunk"
    bp, tp, d = x_prompt.shape
    bs, ts, _ = x_sample.shape
    n_phys, page = cache_kv.shape[1], cache_kv.shape[2]
    n_pages = page_table.shape[1]
    past = n_pages * page
    assert ts <= SAMPLE_ROWS and d == D_MODEL

    w = w_in[0]
    c_q, c_kv, c_win, c_gate = 0, Q_COLS, Q_COLS + KV_COLS, Q_COLS + KV_COLS + WIN_COLS
    c_rest = c_gate + GATE_COLS
    w_main = jnp.concatenate([w[:, c_q:c_win].astype(BF16), w[:, c_rest:].astype(BF16),
                              w[:, c_win:c_gate].astype(BF16)], axis=1)
    wg = w[:, c_gate:c_rest].reshape(d, N_BRANCH, N_KV_HEADS, HEADS_PER_GROUP).transpose(0, 2, 1, 3)
    wg = wg.reshape(d, N_KV_HEADS, N_BRANCH * HEADS_PER_GROUP)
    w_gate = jnp.pad(wg, ((0, 0), (0, 0), (0, LANES - N_BRANCH * HEADS_PER_GROUP))).reshape(d, 2 * LANES).astype(BF16)
    w_out_b = w_out[0].astype(BF16)
    w1 = cmp_w1[0]
    w1cat = jnp.concatenate([w1[:, :CMP_STRIDE].reshape(2, CMP_STRIDE * HEAD_DIM, HEAD_DIM),
                             w1[:, CMP_STRIDE:].reshape(2, CMP_STRIDE * HEAD_DIM, HEAD_DIM)], axis=2).astype(BF16)
    w2b = cmp_w2[0].astype(BF16)
    pe_rows = jnp.pad(cmp_pe[0].reshape(2, 2, CMP_STRIDE * HEAD_DIM), ((0, 0), (0, SUBLANES - 2), (0, 0)))
    b1 = cmp_b1[0].reshape(2, 1, HEAD_DIM)
    b2 = cmp_b2[0].reshape(2, 1, HEAD_DIM)
    conv_w_pad = jnp.pad(conv_dw[0], ((0, CONV_HALO - CONV_WIDTH), (0, 0)))
    db = conv_db[0].reshape(1, CONV_DIM)
    lg = conv_ln_g[0].reshape(1, CONV_DIM)
    lb = conv_ln_b[0].reshape(1, CONV_DIM)
    g_pre = norm_pre[0].reshape(1, d)
    g_post = norm_post[0].reshape(1, d)

    n_c = bp + bs
    c_all = jnp.pad(jnp.concatenate([c_prompt, c_sample], axis=0), ((0, _round_up(n_c, SUBLANES) - n_c), (0, 0)))
    mod = _ada(c_all, w_ada[0], b_ada[0])
    shift, scale, gate = mod[:, :d], mod[:, d:2 * d], mod[:, 2 * d:]
    cconst = _cmpconst(pe_rows, w1cat, b1)

    tm_p = 512
    xp = x_prompt.reshape(bp * tp, d)
    q_p, kv_p, kvb_p, rest_p, winb_p, gate_p = _inproj(
        xp, g_pre, scale[:bp].reshape(bp, 1, d), shift[:bp].reshape(bp, 1, d), w_main, w_gate, tm_p, tp)
    n_cmp_p = (tp - CMP_BLOCK) // CMP_STRIDE + 1
    n_sel_p = -(-tp // SEL_BLOCK)
    ncp_p = _round_up(tp // CMP_STRIDE, LANES)
    kcv_p = _compress(kv_p.reshape(bp, tp * _KV_STREAMS, HEAD_DIM), w1cat, cconst, w2b, b2, ncp_p)
    tk = 512
    e_p = _expand_matrix(tp).reshape(LANES, tp // tk, tk).transpose(1, 0, 2)
    o_nsa_p = _attn_p(q_p, kcv_p, kvb_p.reshape(bp, tp, KV_COLS), winb_p.reshape(bp, tp, WIN_COLS), gate_p, rest_p,
                      _overlap_matrix_t(n_cmp_p, n_sel_p, ncp_p), e_p, bp, tp, n_cmp_p, n_sel_p, tk=tk)
    cv_p, st_p = _conv(rest_p, jnp.zeros((bp, CONV_HALO, CONV_DIM), F32), conv_w_pad, db, lg, lb, bp, tp, 128, 128)
    y_p = _outproj(o_nsa_p, cv_p, w_out_b, xp, gate[:bp].reshape(bp, 1, d), g_post, 256, tp)

    r = SAMPLE_ROWS
    xs = jnp.pad(x_sample, ((0, 0), (0, r - ts), (0, 0))).reshape(bs * r, d)
    per_row = lambda v: jnp.repeat(v[bp:bp + bs], r, axis=0)
    tm_s = min(512, bs * r)
    n_mod = (bs * r) // tm_s
    q_s, kv_s, kvb_s, rest_s, _, gate_s = _inproj(
        xs, g_pre, per_row(scale).reshape(n_mod, tm_s, d), per_row(shift).reshape(n_mod, tm_s, d),
        w_main, w_gate, tm_s, tm_s)
    total = past + ts
    n_cmp_s = (total - CMP_BLOCK) // CMP_STRIDE + 1
    n_sel_s = -(-total // SEL_BLOCK)
    assert n_cmp_s <= past // CMP_STRIDE - 1 + 1 and (n_cmp_s - 1) * CMP_STRIDE + CMP_BLOCK <= past
    ncp_s = _round_up(past // CMP_STRIDE, LANES)
    assert ncp_s == past // CMP_STRIDE
    n_ws = N_WIN_SLOTS * N_KV_HEADS
    o_nsa_s, win_state_s = _attn_s(
        page_table, cache_kv.reshape(n_phys, page * _KV_STREAMS, HEAD_DIM),
        state_win_kv.reshape(bs, WINDOW * n_ws, HEAD_DIM),
        q_s.reshape(bs, r, Q_COLS), kvb_s.reshape(bs, r, KV_COLS), rest_s.reshape(bs, r, _REST_COLS),
        gate_s.reshape(bs, r, 2 * LANES), w1cat, cconst, w2b, b2,
        _overlap_matrix(n_cmp_s, n_sel_s, ncp_s), _expand_matrix(past + LANES),
        n_cmp_s, n_sel_s, past, ts)
    u_init_s = jnp.pad(state_conv[0], ((0, 0), (CONV_HALO - (CONV_WIDTH - 1), 0), (0, 0)))
    cv_s, st_s = _conv_short(rest_s, u_init_s, conv_w_pad, db, lg, lb, bs, r, ts, math.gcd(bs, SUBLANES))
    y_s = _outproj(o_nsa_s.reshape(bs * r, NSA_WIDTH), cv_s, w_out_b, xs, per_row(gate).reshape(n_mod, tm_s, d),
                   g_post, tm_s, tm_s)

    keep = CONV_WIDTH - 1
    y_prompt = y_p.reshape(bp, tp, d)
    y_sample = y_s.reshape(bs, r, d)[:, :ts]
    kv_rows_prompt = kv_p.reshape(1, bp, tp, N_CACHE_SLOTS, N_KV_HEADS, HEAD_DIM)
    kv_rows_sample = kv_s.reshape(1, bs, r, N_CACHE_SLOTS, N_KV_HEADS, HEAD_DIM)[:, :, :ts]
    win_p = rest_p.reshape(bp, tp, _REST_COLS)[:, tp - min(WINDOW, tp):, _REST_WIN * WIN_COLS:]
    win_prompt = win_p.reshape(1, bp, min(WINDOW, tp), N_WIN_SLOTS, N_KV_HEADS, HEAD_DIM)
    win_sample = win_state_s.reshape(1, bs, WINDOW, N_WIN_SLOTS, N_KV_HEADS, HEAD_DIM)
    conv_prompt = st_p[:, CONV_HALO - keep:].reshape(1, bp, keep, CONV_DIM)
    conv_sample = st_s[:, CONV_HALO - keep:].reshape(1, bs, keep, CONV_DIM)
    return (y_prompt, y_sample, kv_rows_prompt, kv_rows_sample, win_prompt, win_sample, conv_prompt, conv_sample)
```

```python
import functools
import math

import numpy as np
import jax
import jax.numpy as jnp
from jax import lax
from jax.experimental import pallas as pl
from jax.experimental.pallas import tpu as pltpu

D_MODEL = 2048
HEAD_DIM = 128
N_HEADS = 8
N_KV_HEADS = 2
HEADS_PER_GROUP = N_HEADS // N_KV_HEADS
NSA_WIDTH = N_HEADS * HEAD_DIM
CONV_DIM = D_MODEL - NSA_WIDTH
CMP_BLOCK = 32
CMP_STRIDE = 16
SEL_BLOCK = 64
N_SELECT = 16
WINDOW = 512
CONV_WIDTH = 31
N_CACHE_SLOTS = 4
N_WIN_SLOTS = 2
N_BRANCH = 3
Q_COLS = N_HEADS * HEAD_DIM
KV_COLS = N_CACHE_SLOTS * N_KV_HEADS * HEAD_DIM
WIN_COLS = N_WIN_SLOTS * N_KV_HEADS * HEAD_DIM
GATE_COLS = N_BRANCH * N_HEADS
_KV_STREAMS = N_CACHE_SLOTS * N_KV_HEADS
NORM_EPS = 1e-6
NEG_INF = -1e30
FORCE_BONUS = 1e6
SM_SCALE = HEAD_DIM ** -0.5

LANES = 128
SUBLANES = 8
CONV_HALO = 32
SAMPLE_ROWS = 8
VMEM_LIMIT = 56 * 1024 * 1024

F32 = jnp.float32
BF16 = jnp.bfloat16


def _sigmoid(x):
    return 1.0 / (1.0 + jnp.exp(-x))


def _silu(x):
    return x * _sigmoid(x)


def _dot(a, b):
    return jnp.dot(a, b, preferred_element_type=F32)


def _dot_nt(a, b):
    return lax.dot_general(a, b, (((1,), (1,)), ((), ())), preferred_element_type=F32)


def _masked_softmax(s, mask):
    s = jnp.where(mask, s, NEG_INF)
    e = jnp.where(mask, jnp.exp(s - jnp.max(s, axis=-1, keepdims=True)), 0.0)
    return e / jnp.maximum(jnp.sum(e, axis=-1, keepdims=True), 1e-30)


def _params(*sem):
    return pltpu.CompilerParams(dimension_semantics=sem, vmem_limit_bytes=VMEM_LIMIT)


def _ada_kernel(c_ref, w_ref, b_ref, o_ref):
    o_ref[...] = _dot(_silu(c_ref[...]).astype(BF16), w_ref[...].astype(BF16)) + b_ref[...]


def _ada(c_all, w_ada, b_ada, tn=512):
    rows, d = c_all.shape
    n = w_ada.shape[1]
    return pl.pallas_call(
        _ada_kernel,
        grid=(n // tn,),
        in_specs=[pl.BlockSpec((rows, d), lambda j: (0, 0)),
                  pl.BlockSpec((d, tn), lambda j: (0, j)),
                  pl.BlockSpec((1, tn), lambda j: (0, j))],
        out_specs=pl.BlockSpec((rows, tn), lambda j: (0, j)),
        out_shape=jax.ShapeDtypeStruct((rows, n), F32),
        compiler_params=_params("parallel"),
        name="ada",
    )(c_all, w_ada, b_ada.reshape(1, n))


def _cmpconst_kernel(pe_ref, w1_ref, b1_ref, o_ref):
    y = _dot(pe_ref[...].astype(BF16), w1_ref[...])
    o_ref[...] = y[0:1, :HEAD_DIM] + y[1:2, HEAD_DIM:] + b1_ref[...]


def _cmpconst(pe_rows, w1cat, b1):
    return pl.pallas_call(
        _cmpconst_kernel,
        grid=(2,),
        in_specs=[pl.BlockSpec((None, SUBLANES, 16 * HEAD_DIM), lambda s: (s, 0, 0)),
                  pl.BlockSpec((None, 16 * HEAD_DIM, 2 * HEAD_DIM), lambda s: (s, 0, 0)),
                  pl.BlockSpec((None, 1, HEAD_DIM), lambda s: (s, 0, 0))],
        out_specs=pl.BlockSpec((None, 1, HEAD_DIM), lambda s: (s, 0, 0)),
        out_shape=jax.ShapeDtypeStruct((2, 1, HEAD_DIM), F32),
        compiler_params=_params("parallel"),
        name="cmpconst",
    )(pe_rows, w1cat, b1)


_TN = 512
_J_KV = Q_COLS // _TN
_J_REST = (Q_COLS + KV_COLS) // _TN
_REST_COLS = NSA_WIDTH + 3 * CONV_DIM + WIN_COLS
_N_MAIN = Q_COLS + KV_COLS + _REST_COLS
_J_WIN = (_N_MAIN - WIN_COLS) // _TN
_NJ = _N_MAIN // _TN
_REST_Z = 0
_REST_A = 1
_REST_GL = 2
_REST_ZC = 3
_REST_WIN = (NSA_WIDTH + 3 * CONV_DIM) // WIN_COLS


def _inproj_kernel(x_ref, g_ref, sc_ref, sh_ref, w_ref, wg_ref,
                   q_ref, kv_ref, kvb_ref, rest_ref, winb_ref, gate_ref, h_ref):
    j = pl.program_id(1)

    @pl.when(j == 0)
    def _():
        x = x_ref[...]
        y = x * lax.rsqrt(jnp.mean(x * x, axis=-1, keepdims=True) + NORM_EPS) * g_ref[...]
        h = (y * (1.0 + sc_ref[...]) + sh_ref[...]).astype(BF16)
        h_ref[...] = h
        gate_ref[...] = _dot(h, wg_ref[...])

    tile = lambda: _dot(h_ref[...], w_ref[...])

    @pl.when(j < _J_KV)
    def _():
        q_ref[...] = tile().astype(BF16)

    tm = h_ref.shape[0]
    streams_per_tile = _TN // HEAD_DIM
    for jj in range(_J_KV, _J_REST):
        @pl.when(j == jj)
        def _(jj=jj):
            acc = tile()
            kvb_ref[...] = acc.astype(BF16)
            for cc in range(streams_per_tile):
                c = (jj - _J_KV) * streams_per_tile + cc
                kv_ref[pl.ds(c, tm, stride=_KV_STREAMS), :] = acc[:, cc * HEAD_DIM:(cc + 1) * HEAD_DIM]

    @pl.when((j >= _J_REST) & (j != _J_WIN))
    def _():
        rest_ref[...] = tile()

    @pl.when(j == _J_WIN)
    def _():
        acc = tile()
        rest_ref[...] = acc
        winb_ref[...] = acc.astype(BF16)


def _inproj(x, g, scale3, shift3, w_main, w_gate, tm, rows_per_mod):
    m, d = x.shape
    mod_rows = scale3.shape[1]
    tiles_per_mod = rows_per_mod // tm

    def mod_map(i, j):
        return (i // tiles_per_mod, 0, 0)

    clampj = lambda j, lo, n: jnp.clip(j - lo, 0, n - 1)
    return pl.pallas_call(
        _inproj_kernel,
        grid=(m // tm, _NJ),
        in_specs=[pl.BlockSpec((tm, d), lambda i, j: (i, 0)),
                  pl.BlockSpec((1, d), lambda i, j: (0, 0)),
                  pl.BlockSpec((None, mod_rows, d), mod_map),
                  pl.BlockSpec((None, mod_rows, d), mod_map),
                  pl.BlockSpec((d, _TN), lambda i, j: (0, j)),
                  pl.BlockSpec((d, 2 * LANES), lambda i, j: (0, 0))],
        out_specs=[pl.BlockSpec((tm, _TN), lambda i, j: (i, clampj(j, 0, _J_KV))),
                   pl.BlockSpec((tm * _KV_STREAMS, HEAD_DIM), lambda i, j: (i, 0)),
                   pl.BlockSpec((tm, _TN), lambda i, j: (i, clampj(j, _J_KV, _J_REST - _J_KV))),
                   pl.BlockSpec((tm, _TN), lambda i, j: (i, clampj(j, _J_REST, _NJ - _J_REST))),
                   pl.BlockSpec((tm, _TN), lambda i, j: (i, 0)),
                   pl.BlockSpec((tm, 2 * LANES), lambda i, j: (i, 0))],
        out_shape=[jax.ShapeDtypeStruct((m, Q_COLS), BF16),
                   jax.ShapeDtypeStruct((m * _KV_STREAMS, HEAD_DIM), F32),
                   jax.ShapeDtypeStruct((m, KV_COLS), BF16),
                   jax.ShapeDtypeStruct((m, _REST_COLS), F32),
                   jax.ShapeDtypeStruct((m, WIN_COLS), BF16),
                   jax.ShapeDtypeStruct((m, 2 * LANES), F32)],
        scratch_shapes=[pltpu.VMEM((tm, d), BF16)],
        compiler_params=_params("parallel", "arbitrary"),
        name="inproj",
    )(x, g, scale3, shift3, w_main, w_gate)


def _compress_rows(load_rows, nch, w1, cconst, w2, b2):
    lhs = jnp.concatenate([load_rows(l) for l in range(CMP_STRIDE)], axis=1).astype(BF16)
    y = _dot(lhs, w1)
    second = pltpu.roll(y[:, HEAD_DIM:], shift=nch - 1, axis=0)
    h = _silu(y[:, :HEAD_DIM] + second + cconst)
    out = _dot(h.astype(BF16), w2) + b2
    row = lax.broadcasted_iota(jnp.int32, out.shape, 0)
    return jnp.where(row < nch - 1, out, 0.0)


def _compress_kernel(x_ref, w1_ref, cc_ref, w2_ref, b2_ref, o_ref, *, nch, ncp):
    for c in range(2 * N_KV_HEADS):
        s = c // N_KV_HEADS

        def load_rows(l, c=c):
            return x_ref[pl.ds(l * _KV_STREAMS + c, nch, stride=CMP_STRIDE * _KV_STREAMS), :]

        out = _compress_rows(load_rows, nch, w1_ref[s], cc_ref[s], w2_ref[s], b2_ref[s])
        if ncp > nch:
            out = jnp.concatenate([out, jnp.zeros((ncp - nch, HEAD_DIM), F32)], axis=0)
        o_ref[c] = out.astype(BF16)


def _compress(kv_streams, w1cat, cconst, w2b, b2, ncp):
    b, rows, _ = kv_streams.shape
    t = rows // _KV_STREAMS
    nch = t // CMP_STRIDE
    return pl.pallas_call(
        functools.partial(_compress_kernel, nch=nch, ncp=ncp),
        grid=(b,),
        in_specs=[pl.BlockSpec((None, rows, HEAD_DIM), lambda i: (i, 0, 0)),
                  pl.BlockSpec((2, 16 * HEAD_DIM, 2 * HEAD_DIM), lambda i: (0, 0, 0)),
                  pl.BlockSpec((2, 1, HEAD_DIM), lambda i: (0, 0, 0)),
                  pl.BlockSpec((2, HEAD_DIM, HEAD_DIM), lambda i: (0, 0, 0)),
                  pl.BlockSpec((2, 1, HEAD_DIM), lambda i: (0, 0, 0))],
        out_specs=pl.BlockSpec((None, 4, ncp, HEAD_DIM), lambda i: (i, 0, 0, 0)),
        out_shape=jax.ShapeDtypeStruct((b, 4, ncp, HEAD_DIM), BF16),
        compiler_params=_params("parallel"),
        name="compress",
    )(kv_streams, w1cat, cconst, w2b, b2)


def _cmp_branch(q_heads, kc, vc, qpos, n_cmp):
    rows = q_heads[0].shape[0]
    nio = lax.broadcasted_iota(jnp.int32, (rows, kc.shape[0]), 1)
    mask = (nio < n_cmp) & (nio * CMP_STRIDE + (CMP_BLOCK - 1) <= qpos)
    outs, probs = [], []
    for q in q_heads:
        p = _masked_softmax(_dot_nt(q, kc) * SM_SCALE, mask)
        outs.append(_dot(p.astype(BF16), vc))
        probs.append(p)
    return outs, probs


def _select_mask(psum, ov, qpos, n_sel):
    hi = psum.astype(BF16)
    lo = (psum - hi.astype(F32)).astype(BF16)
    imp = _dot(hi, ov) + _dot(lo, ov)
    rows = imp.shape[0]
    jio = lax.broadcasted_iota(jnp.int32, (rows, LANES), 1)
    allowed = (jio * SEL_BLOCK <= qpos) & (jio < n_sel)
    cur = qpos // SEL_BLOCK
    forced = (jio == 0) | (jio == cur) | (jio == cur - 1)
    score = jnp.where(allowed, imp + jnp.where(forced, FORCE_BONUS, 0.0), NEG_INF)
    rank = jnp.zeros((rows, LANES), F32)
    for i in range(n_sel):
        si = jnp.broadcast_to(score[:, i:i + 1], (rows, LANES))
        tie = jnp.where(jio > i, 1.0, 0.0)
        rank = rank + jnp.where(si > score, 1.0, jnp.where(si == score, tie, 0.0))
    return (rank < float(min(N_SELECT, n_sel))) & allowed


def _select_mask_t(psum, ovt, qpos_row, n_sel):
    hi = psum.astype(BF16)
    lo = (psum - hi.astype(F32)).astype(BF16)
    imp = _dot_nt(ovt, hi) + _dot_nt(ovt, lo)
    sr, rows = imp.shape
    jio = lax.broadcasted_iota(jnp.int32, (sr, rows), 0)
    allowed = (jio * SEL_BLOCK <= qpos_row) & (jio < n_sel)
    cur = qpos_row // SEL_BLOCK
    forced = (jio == 0) | (jio == cur) | (jio == cur - 1)
    score = jnp.where(allowed, imp + jnp.where(forced, FORCE_BONUS, 0.0), NEG_INF)
    rank = jnp.zeros((sr, rows), F32)
    for i in range(n_sel):
        si = jnp.broadcast_to(score[i:i + 1, :], (sr, rows))
        tie = jnp.where(jio > i, 1.0, 0.0)
        rank = rank + jnp.where(si > score, 1.0, jnp.where(si == score, tie, 0.0))
    sel_t = jnp.where((rank < float(min(N_SELECT, n_sel))) & allowed, 1.0, 0.0)
    sel_t = jnp.concatenate([sel_t, jnp.zeros((LANES - sr, rows), F32)], axis=0)
    return sel_t.T


def _biased_softmax(s, bias):
    s = s + bias
    e = jnp.exp(s - jnp.max(s, axis=-1, keepdims=True))
    return e / jnp.maximum(jnp.sum(e, axis=-1, keepdims=True), 1e-30)


def _attn_p_kernel(q_ref, kc_ref, vc_ref, ks_ref, vs_ref, kw_ref, vw_ref, gate_ref, z_ref, ovt_ref, e_ref,
                   o_ref, m_s, l_s, acc_s, pre_s, *, tq, tk, n_cmp, n_sel):
    qi = pl.program_id(2)
    q0 = qi * tq
    qpos = q0 + lax.broadcasted_iota(jnp.int32, (tq, 1), 0)
    heads = range(HEADS_PER_GROUP)
    q_heads = [q_ref[:, r * HEAD_DIM:(r + 1) * HEAD_DIM] for r in heads]
    gates = _sigmoid(gate_ref[...])
    g_of = lambda br, r: jnp.broadcast_to(
        gates[:, br * HEADS_PER_GROUP + r:br * HEADS_PER_GROUP + r + 1], (tq, HEAD_DIM))

    kc = kc_ref[...]
    vc = vc_ref[...]
    w0 = pl.multiple_of(jnp.maximum(q0 - WINDOW, 0), LANES)
    k_w = kw_ref[pl.ds(w0, WINDOW + tq), :]
    v_w = vw_ref[pl.ds(w0, WINDOW + tq), :]
    dist = qpos - (w0 + lax.broadcasted_iota(jnp.int32, (tq, WINDOW + tq), 1))
    wbias = jnp.where((dist >= 0) & (dist <= WINDOW), 0.0, NEG_INF)
    nio = lax.broadcasted_iota(jnp.int32, (tq, kc.shape[0]), 1)
    cmask = (nio < n_cmp) & (nio * CMP_STRIDE + (CMP_BLOCK - 1) <= qpos)

    s_cmp = [_dot_nt(q_heads[r], kc) * SM_SCALE for r in heads]
    s_win = [_dot_nt(q_heads[r], k_w) * SM_SCALE for r in heads]
    p_cmp = [_masked_softmax(s_cmp[r], cmask) for r in heads]
    psum = (p_cmp[0] + p_cmp[1]) + (p_cmp[2] + p_cmp[3])
    p_win = [_biased_softmax(s_win[r], wbias) for r in heads]
    qpos_row = q0 + lax.broadcasted_iota(jnp.int32, (1, tq), 1)
    selb = _select_mask_t(psum, ovt_ref[...], qpos_row, n_sel).astype(BF16)
    for r in heads:
        pre_s[r] = (g_of(0, r) * _dot(p_cmp[r].astype(BF16), vc)
                    + g_of(2, r) * _dot(p_win[r].astype(BF16), v_w))

    m_s[...] = jnp.full(m_s.shape, NEG_INF, F32)
    l_s[...] = jnp.zeros(l_s.shape, F32)
    acc_s[...] = jnp.zeros(acc_s.shape, F32)
    n_kt = (q0 + tq - 1) // tk + 1

    def sweep(kt, carry):
        k0 = pl.multiple_of(kt * tk, tk)
        k_t = ks_ref[pl.ds(k0, tk), :]
        v_t = vs_ref[pl.ds(k0, tk), :]
        kpos = k0 + lax.broadcasted_iota(jnp.int32, (tq, tk), 1)
        bias = jnp.where((_dot(selb, e_ref[kt]) > 0.5) & (kpos <= qpos), 0.0, NEG_INF)
        s = [_dot_nt(q_heads[r], k_t) * SM_SCALE + bias for r in heads]
        m_prev = [m_s[r] for r in heads]
        m_new = [jnp.maximum(m_prev[r], jnp.max(s[r], axis=-1, keepdims=True)) for r in heads]
        p = [jnp.exp(s[r] - jnp.concatenate([m_new[r]] * (tk // LANES), axis=1)) for r in heads]
        alpha = [jnp.exp(m_prev[r] - m_new[r]) for r in heads]
        for r in heads:
            l_s[r] = alpha[r] * l_s[r] + jnp.sum(p[r], axis=-1, keepdims=True)
            acc_s[r] = alpha[r] * acc_s[r] + _dot(p[r].astype(BF16), v_t)
            m_s[r] = m_new[r]
        return carry

    lax.fori_loop(0, n_kt, sweep, 0)

    for r in heads:
        o_slc = acc_s[r] / jnp.maximum(l_s[r], 1e-30)
        o = pre_s[r] + g_of(1, r) * o_slc
        o_ref[:, r * HEAD_DIM:(r + 1) * HEAD_DIM] = (
            o * _silu(z_ref[:, r * HEAD_DIM:(r + 1) * HEAD_DIM])).astype(BF16)


def _attn_p(q_bf, kcv, kv_bf3, win_bf3, gate, rest, ovt, e_p, b, t, n_cmp, n_sel, tq=128, tk=256):
    assert t >= WINDOW + tq and t % tk == 0 and tq == LANES
    nq = t // tq
    ncp = kcv.shape[2]
    sr = ovt.shape[0]
    gw = HEADS_PER_GROUP * HEAD_DIM
    row = lambda bi, g, qi: bi * nq + qi
    return pl.pallas_call(
        functools.partial(_attn_p_kernel, tq=tq, tk=tk, n_cmp=n_cmp, n_sel=n_sel),
        grid=(b, N_KV_HEADS, nq),
        in_specs=[pl.BlockSpec((tq, gw), lambda bi, g, qi: (row(bi, g, qi), g)),
                  pl.BlockSpec((None, None, ncp, HEAD_DIM), lambda bi, g, qi: (bi, g, 0, 0)),
                  pl.BlockSpec((None, None, ncp, HEAD_DIM), lambda bi, g, qi: (bi, N_KV_HEADS + g, 0, 0)),
                  pl.BlockSpec((None, t, HEAD_DIM), lambda bi, g, qi: (bi, 0, 2 * N_KV_HEADS + g)),
                  pl.BlockSpec((None, t, HEAD_DIM), lambda bi, g, qi: (bi, 0, 3 * N_KV_HEADS + g)),
                  pl.BlockSpec((None, t, HEAD_DIM), lambda bi, g, qi: (bi, 0, g)),
                  pl.BlockSpec((None, t, HEAD_DIM), lambda bi, g, qi: (bi, 0, N_KV_HEADS + g)),
                  pl.BlockSpec((tq, LANES), lambda bi, g, qi: (row(bi, g, qi), g)),
                  pl.BlockSpec((tq, gw), lambda bi, g, qi: (row(bi, g, qi), g)),
                  pl.BlockSpec((sr, ncp), lambda bi, g, qi: (0, 0)),
                  pl.BlockSpec((t // tk, LANES, tk), lambda bi, g, qi: (0, 0, 0))],
        out_specs=pl.BlockSpec((tq, gw), lambda bi, g, qi: (row(bi, g, qi), g)),
        out_shape=jax.ShapeDtypeStruct((b * t, NSA_WIDTH), BF16),
        scratch_shapes=[pltpu.VMEM((HEADS_PER_GROUP, tq, LANES), F32),
                        pltpu.VMEM((HEADS_PER_GROUP, tq, LANES), F32),
                        pltpu.VMEM((HEADS_PER_GROUP, tq, HEAD_DIM), F32),
                        pltpu.VMEM((HEADS_PER_GROUP, tq, HEAD_DIM), F32)],
        compiler_params=_params("parallel", "parallel", "arbitrary"),
        name="attn_p",
    )(q_bf, kcv, kcv, kv_bf3, kv_bf3, win_bf3, win_bf3, gate, rest, ovt, e_p)


def _attn_s_kernel(pt_ref, cache_ref, wst_ref, q_ref, kvn_ref, winn_ref, gate_ref, z_ref, winc_ref,
                   w1_ref, cc_ref, w2_ref, b2_ref, ov_ref, e_ref, perm_ref, o_ref, wout_ref,
                   kvbuf, wbuf, wnew, kcs, ksl, kwn, sem, wsem, osem,
                   *, n_pages, page, n_cmp, n_sel, pos0, t_new):
    i = pl.program_id(0)
    nb = pl.num_programs(0) - 1
    past = n_pages * page
    nch = past // CMP_STRIDE
    rows = HEADS_PER_GROUP * SAMPLE_ROWS
    n_ws = N_WIN_SLOTS * N_KV_HEADS
    state_rows = WINDOW * n_ws
    new_rows = t_new * n_ws
    prow = page * _KV_STREAMS
    c_seq = jnp.minimum(i, nb - 1)
    a_seq = jnp.maximum(i - 1, 0)
    cs = c_seq % 2
    asl = a_seq % 2

    def page_copy(seq, p, slot):
        return pltpu.make_async_copy(cache_ref.at[pt_ref[seq, p]], kvbuf.at[slot, pl.ds(p * prow, prow)],
                                     sem.at[slot])

    def win_copy(seq, slot):
        return pltpu.make_async_copy(wst_ref.at[seq], wbuf.at[slot], wsem.at[slot])

    def fetch(seq, slot):
        for p in range(n_pages):
            page_copy(seq, p, slot).start()
        win_copy(seq, slot).start()

    def state_copies(seq, slot):
        kept = pltpu.make_async_copy(wbuf.at[slot, pl.ds(new_rows, state_rows - new_rows)],
                                     wout_ref.at[seq, pl.ds(0, state_rows - new_rows)], osem.at[slot])
        fresh = pltpu.make_async_copy(wnew.at[slot], wout_ref.at[seq, pl.ds(state_rows - new_rows, new_rows)],
                                      osem.at[slot])
        return kept, fresh

    @pl.when(i == 0)
    def _():
        fetch(0, 0)
        kcs[0] = jnp.zeros(kcs.shape[1:], BF16)
        ksl[0] = jnp.zeros(ksl.shape[1:], BF16)
        kwn[0] = jnp.zeros(kwn.shape[1:], BF16)

    @pl.when(i >= 1)
    def _():
        for cp in state_copies(i - 1, (i - 1) % 2):
            cp.wait()

    @pl.when(i + 1 < nb)
    def _():
        fetch(i + 1, (i + 1) % 2)

    @pl.when(i < nb)
    def _():
        for p in range(n_pages):
            page_copy(i, p, i % 2).wait()
        win_copy(i, i % 2).wait()
        for t in range(t_new):
            for c in range(n_ws):
                wnew[i % 2, pl.ds(t * n_ws + c, 1), :] = winc_ref[pl.ds(t, 1), c * HEAD_DIM:(c + 1) * HEAD_DIM]
        for cp in state_copies(i, i % 2):
            cp.start()

    qpos = pos0 + lax.broadcasted_iota(jnp.int32, (rows, 1), 0) % SAMPLE_ROWS
    kvn = kvn_ref[...].astype(F32)
    qf = q_ref[...].astype(F32)
    gates = _sigmoid(gate_ref[...])
    zero_tail = jnp.zeros((LANES - 2 * SUBLANES, HEAD_DIM), BF16)
    pad8 = jnp.zeros((SUBLANES, HEAD_DIM), F32)

    def with_new_rows(old_bf, new_f32):
        new16 = jnp.concatenate([new_f32, pad8], axis=0).astype(BF16)
        return jnp.concatenate([old_bf, new16, zero_tail], axis=0)

    def stack_heads(x8):
        return jnp.concatenate([x8] * HEADS_PER_GROUP, axis=0)

    groups = range(N_KV_HEADS)
    nk = past + LANES
    kpos = lax.broadcasted_iota(jnp.int32, (rows, nk), 1)
    wpos = (pos0 - WINDOW) + lax.broadcasted_iota(jnp.int32, (rows, WINDOW + LANES), 1)
    dist = qpos - wpos
    wmask = (dist >= 0) & (dist <= WINDOW) & (wpos >= 0)
    nio = lax.broadcasted_iota(jnp.int32, (rows, nch), 1)
    cmask = (nio < n_cmp) & (nio * CMP_STRIDE + (CMP_BLOCK - 1) <= qpos)

    qgs, s_cmp, s_slc, s_win, v_alls, vw_alls = [], [], [], [], [], []
    for g in groups:
        heads = [qf[:, (g * HEADS_PER_GROUP + r) * HEAD_DIM:(g * HEADS_PER_GROUP + r + 1) * HEAD_DIM]
                 for r in range(HEADS_PER_GROUP)]
        qgs.append(jnp.concatenate(heads, axis=0).astype(BF16))
    for g in groups:
        s_cmp.append(_dot_nt(qgs[g], kcs[asl, g]) * SM_SCALE)
    for g in groups:
        c_k = (2 * N_KV_HEADS + g) * HEAD_DIM
        c_v = (3 * N_KV_HEADS + g) * HEAD_DIM
        k_all = with_new_rows(ksl[asl, g], kvn[:, c_k:c_k + HEAD_DIM])
        v_alls.append(with_new_rows(ksl[asl, N_KV_HEADS + g], kvn[:, c_v:c_v + HEAD_DIM]))
        s_slc.append(_dot_nt(qgs[g], k_all) * SM_SCALE)
    for g in groups:
        wk = g * HEAD_DIM
        wv = (N_KV_HEADS + g) * HEAD_DIM
        kw_all = with_new_rows(kwn[asl, g], winn_ref[:, wk:wk + HEAD_DIM])
        vw_alls.append(with_new_rows(kwn[asl, N_KV_HEADS + g], winn_ref[:, wv:wv + HEAD_DIM]))
        s_win.append(_dot_nt(qgs[g], kw_all) * SM_SCALE)

    vcs = [kcs[asl, N_KV_HEADS + g] for g in groups]

    perm = perm_ref[...]
    blk = LANES
    cpb = blk // CMP_STRIDE
    stream = lambda c: kvbuf[cs, pl.ds(c, past, stride=_KV_STREAMS), :].astype(BF16)

    def slot_chunks(s):
        xs = jnp.concatenate([stream(s * N_KV_HEADS + g) for g in groups], axis=1)
        out = [[] for _ in groups]
        for k in range(past // blk):
            y = _dot(perm, xs[k * blk:(k + 1) * blk])
            for g in groups:
                out[g].append(jnp.concatenate(
                    [y[l * cpb:(l + 1) * cpb, g * HEAD_DIM:(g + 1) * HEAD_DIM] for l in range(CMP_STRIDE)], axis=1))
        return jnp.concatenate([jnp.concatenate(o, axis=0) for o in out], axis=0).astype(BF16)

    def compress_slot(s, lhs):
        y = _dot(lhs, w1_ref[s])
        second = pltpu.roll(y[:, HEAD_DIM:], shift=lhs.shape[0] - 1, axis=0)
        h = _silu(y[:, :HEAD_DIM] + second + cc_ref[s])
        out = _dot(h.astype(BF16), w2_ref[s]) + b2_ref[s]
        row = lax.broadcasted_iota(jnp.int32, out.shape, 0) % nch
        out = jnp.where(row < nch - 1, out, 0.0).astype(BF16)
        for g in groups:
            kcs[cs, s * N_KV_HEADS + g] = out[g * nch:(g + 1) * nch]

    lhs0 = slot_chunks(0)

    p_cmps = [_masked_softmax(s_cmp[g], cmask) for g in groups]
    psums = [(p[0:8] + p[8:16]) + (p[16:24] + p[24:32]) for p in p_cmps]
    sels = [_select_mask(psums[g], ov_ref[...], qpos[0:SAMPLE_ROWS], n_sel) for g in groups]
    selbs = [jnp.where(s, 1.0, 0.0).astype(BF16) for s in sels]

    compress_slot(0, lhs0)
    lhs1 = slot_chunks(1)

    smasks = [(stack_heads(_dot(selbs[g], e_ref[...])) > 0.5) & (kpos <= qpos) for g in groups]
    o_cmps = [_dot(p_cmps[g].astype(BF16), vcs[g]) for g in groups]
    p_wins = [_masked_softmax(s_win[g], wmask) for g in groups]
    o_wins = [_dot(p_wins[g].astype(BF16), vw_alls[g]) for g in groups]
    p_slcs = [_masked_softmax(s_slc[g], smasks[g]) for g in groups]
    o_slcs = [_dot(p_slcs[g].astype(BF16), v_alls[g]) for g in groups]

    compress_slot(1, lhs1)
    for k in range(2 * N_KV_HEADS):
        ksl[cs, k] = stream(2 * N_KV_HEADS + k)
    for k in range(n_ws):
        kwn[cs, k] = wbuf[cs, pl.ds(k, WINDOW, stride=n_ws), :].astype(BF16)

    for g in groups:
        o_cmp, o_slc, o_win = o_cmps[g], o_slcs[g], o_wins[g]
        for r in range(HEADS_PER_GROUP):
            h = g * HEADS_PER_GROUP + r
            rs = slice(r * SAMPLE_ROWS, (r + 1) * SAMPLE_ROWS)
            g_of = lambda br: jnp.broadcast_to(
                gates[:, g * LANES + br * HEADS_PER_GROUP + r:g * LANES + br * HEADS_PER_GROUP + r + 1],
                (SAMPLE_ROWS, HEAD_DIM))
            o = g_of(0) * o_cmp[rs] + g_of(1) * o_slc[rs] + g_of(2) * o_win[rs]
            o_ref[:, h * HEAD_DIM:(h + 1) * HEAD_DIM] = (
                o * _silu(z_ref[:, h * HEAD_DIM:(h + 1) * HEAD_DIM])).astype(BF16)


def _attn_s(page_table, cache3, wst3, q3, kvn3, rest3, gate3, w1cat, cconst, w2b, b2, ov, e_s,
            n_cmp, n_sel, pos0, t_new):
    nb, n_pages = page_table.shape
    page = cache3.shape[1] // _KV_STREAMS
    past = n_pages * page
    nch = past // CMP_STRIDE
    n_ws = N_WIN_SLOTS * N_KV_HEADS
    assert wst3.shape[1:] == (WINDOW * n_ws, HEAD_DIM) and past % LANES == 0
    ncp = ov.shape[0]
    prev = lambda i: jnp.maximum(i - 1, 0)
    grid_spec = pltpu.PrefetchScalarGridSpec(
        num_scalar_prefetch=1,
        grid=(nb + 1,),
        in_specs=[pl.BlockSpec(memory_space=pl.ANY),
                  pl.BlockSpec(memory_space=pl.ANY),
                  pl.BlockSpec((None, SAMPLE_ROWS, Q_COLS), lambda i, pt: (prev(i), 0, 0)),
                  pl.BlockSpec((None, SAMPLE_ROWS, KV_COLS), lambda i, pt: (prev(i), 0, 0)),
                  pl.BlockSpec((None, SAMPLE_ROWS, WIN_COLS), lambda i, pt: (prev(i), 0, _REST_WIN)),
                  pl.BlockSpec((None, SAMPLE_ROWS, 2 * LANES), lambda i, pt: (prev(i), 0, 0)),
                  pl.BlockSpec((None, SAMPLE_ROWS, NSA_WIDTH), lambda i, pt: (prev(i), 0, _REST_Z)),
                  pl.BlockSpec((None, SAMPLE_ROWS, WIN_COLS), lambda i, pt: (jnp.minimum(i, nb - 1), 0, _REST_WIN)),
                  pl.BlockSpec((2, 16 * HEAD_DIM, 2 * HEAD_DIM), lambda i, pt: (0, 0, 0)),
                  pl.BlockSpec((2, 1, HEAD_DIM), lambda i, pt: (0, 0, 0)),
                  pl.BlockSpec((2, HEAD_DIM, HEAD_DIM), lambda i, pt: (0, 0, 0)),
                  pl.BlockSpec((2, 1, HEAD_DIM), lambda i, pt: (0, 0, 0)),
                  pl.BlockSpec((ncp, LANES), lambda i, pt: (0, 0)),
                  pl.BlockSpec((LANES, past + LANES), lambda i, pt: (0, 0)),
                  pl.BlockSpec((LANES, LANES), lambda i, pt: (0, 0))],
        out_specs=[pl.BlockSpec((None, SAMPLE_ROWS, NSA_WIDTH), lambda i, pt: (prev(i), 0, 0)),
                   pl.BlockSpec(memory_space=pl.ANY)],
        scratch_shapes=[pltpu.VMEM((2, past * _KV_STREAMS, HEAD_DIM), F32),
                        pltpu.VMEM((2, WINDOW * n_ws, HEAD_DIM), F32),
                        pltpu.VMEM((2, t_new * n_ws, HEAD_DIM), F32),
                        pltpu.VMEM((2, 2 * N_KV_HEADS, nch, HEAD_DIM), BF16),
                        pltpu.VMEM((2, 2 * N_KV_HEADS, past, HEAD_DIM), BF16),
                        pltpu.VMEM((2, n_ws, WINDOW, HEAD_DIM), BF16),
                        pltpu.SemaphoreType.DMA((2,)),
                        pltpu.SemaphoreType.DMA((2,)),
                        pltpu.SemaphoreType.DMA((2,))],
    )
    return pl.pallas_call(
        functools.partial(_attn_s_kernel, n_pages=n_pages, page=page, n_cmp=n_cmp, n_sel=n_sel, pos0=pos0,
                          t_new=t_new),
        grid_spec=grid_spec,
        out_shape=[jax.ShapeDtypeStruct((nb, SAMPLE_ROWS, NSA_WIDTH), BF16),
                   jax.ShapeDtypeStruct((nb, WINDOW * n_ws, HEAD_DIM), F32)],
        compiler_params=_params("arbitrary"),
        name="attn_s",
    )(page_table, cache3, wst3, q3, kvn3, rest3, gate3, rest3, rest3, w1cat, cconst, w2b, b2, ov, e_s,
      _chunk_permutation())


_CONV_RC = 64
_CONV_LC = 256


def _depthwise_conv(uext, w_ref, db_ref, ybuf, tt, row0=0):
    off = CONV_HALO - (CONV_WIDTH - 1)
    uext[CONV_HALO + tt:CONV_HALO + tt + SUBLANES] = jnp.zeros((SUBLANES, CONV_DIM), F32)
    rc = min(_CONV_RC, tt)
    for r0 in range(0, tt, rc):
        for c0 in range(0, CONV_DIM, _CONV_LC):
            lanes = slice(c0, c0 + _CONV_LC)
            acc = jnp.broadcast_to(db_ref[:, lanes], (rc, _CONV_LC))
            for r in range(SUBLANES):
                z = None
                for a in range((CONV_WIDTH + off) // SUBLANES + 1):
                    k = SUBLANES * a + r - off
                    if 0 <= k < CONV_WIDTH:
                        term = w_ref[k:k + 1, lanes] * uext[r0 + SUBLANES * a:r0 + SUBLANES * a + rc + SUBLANES, lanes]
                        z = term if z is None else z + term
                acc = acc + z[r:r + rc]
            ybuf[row0 + r0:row0 + r0 + rc, lanes] = acc


def _conv_kernel(a_ref, gl_ref, ah_ref, glh_ref, init_ref, zc_ref, w_ref, db_ref, lg_ref, lb_ref,
                 cv_ref, st_ref, uext, ybuf, *, tt, t_valid):
    ti = pl.program_id(1)
    nt = pl.num_programs(1)
    halo = ah_ref[...] * _sigmoid(glh_ref[...])
    uext[0:CONV_HALO] = jnp.where(ti == 0, init_ref[...], halo)
    uext[CONV_HALO:CONV_HALO + tt] = a_ref[...] * _sigmoid(gl_ref[...])
    _depthwise_conv(uext, w_ref, db_ref, ybuf, tt)

    y = ybuf[...]
    mu = jnp.mean(y, axis=-1, keepdims=True)
    var = jnp.mean(jnp.square(y - mu), axis=-1, keepdims=True)
    ln = (y - mu) * lax.rsqrt(var + NORM_EPS) * lg_ref[...] + lb_ref[...]
    cv_ref[...] = (_silu(ln) * _silu(zc_ref[...])).astype(BF16)

    @pl.when(ti == nt - 1)
    def _():
        st_ref[...] = uext[pl.ds(t_valid, CONV_HALO), :]


def _conv(rest, u_init, w_pad, db, lg, lb, b, t, tt, t_valid):
    assert tt % CONV_HALO == 0 and t % tt == 0
    nt = t // tt
    hb = tt // CONV_HALO
    row = lambda bi, ti: bi * nt + ti
    halo_map = lambda col: (lambda bi, ti: (jnp.maximum(row(bi, ti) * hb - 1, 0), col))
    halo_rows = CONV_HALO
    vec = lambda: pl.BlockSpec((1, CONV_DIM), lambda bi, ti: (0, 0))
    return pl.pallas_call(
        functools.partial(_conv_kernel, tt=tt, t_valid=t_valid),
        grid=(b, nt),
        in_specs=[pl.BlockSpec((tt, CONV_DIM), lambda bi, ti: (row(bi, ti), _REST_A)),
                  pl.BlockSpec((tt, CONV_DIM), lambda bi, ti: (row(bi, ti), _REST_GL)),
                  pl.BlockSpec((halo_rows, CONV_DIM), halo_map(_REST_A)),
                  pl.BlockSpec((halo_rows, CONV_DIM), halo_map(_REST_GL)),
                  pl.BlockSpec((None, CONV_HALO, CONV_DIM), lambda bi, ti: (bi, 0, 0)),
                  pl.BlockSpec((tt, CONV_DIM), lambda bi, ti: (row(bi, ti), _REST_ZC)),
                  pl.BlockSpec((CONV_HALO, CONV_DIM), lambda bi, ti: (0, 0)),
                  vec(), vec(), vec()],
        out_specs=[pl.BlockSpec((tt, CONV_DIM), lambda bi, ti: (row(bi, ti), 0)),
                   pl.BlockSpec((None, CONV_HALO, CONV_DIM), lambda bi, ti: (bi, 0, 0))],
        out_shape=[jax.ShapeDtypeStruct((b * t, CONV_DIM), BF16),
                   jax.ShapeDtypeStruct((b, CONV_HALO, CONV_DIM), F32)],
        scratch_shapes=[pltpu.VMEM((CONV_HALO + tt + SUBLANES, CONV_DIM), F32),
                        pltpu.VMEM((tt, CONV_DIM), F32)],
        compiler_params=_params("parallel", "arbitrary"),
        name="conv",
    )(rest, rest, rest, rest, u_init, rest, w_pad, db, lg, lb)


def _conv_short_kernel(a_ref, gl_ref, init_ref, zc_ref, w_ref, db_ref, lg_ref, lb_ref,
                       cv_ref, st_ref, uext, ybuf, *, tt, t_valid, ns):
    u = a_ref[...] * _sigmoid(gl_ref[...])
    for s in range(ns):
        ue = uext.at[s]
        ue[0:CONV_HALO] = init_ref[s]
        ue[CONV_HALO:CONV_HALO + tt] = u[s * tt:(s + 1) * tt]
        _depthwise_conv(ue, w_ref, db_ref, ybuf, tt, row0=s * tt)
        st_ref[s] = ue[pl.ds(t_valid, CONV_HALO), :]
    y = ybuf[...]
    mu = jnp.mean(y, axis=-1, keepdims=True)
    var = jnp.mean(jnp.square(y - mu), axis=-1, keepdims=True)
    ln = (y - mu) * lax.rsqrt(var + NORM_EPS) * lg_ref[...] + lb_ref[...]
    cv_ref[...] = (_silu(ln) * _silu(zc_ref[...])).astype(BF16)


def _conv_short(rest, u_init, w_pad, db, lg, lb, b, tt, t_valid, ns):
    assert b % ns == 0 and tt % SUBLANES == 0
    rows = ns * tt
    vec = lambda: pl.BlockSpec((1, CONV_DIM), lambda i: (0, 0))
    return pl.pallas_call(
        functools.partial(_conv_short_kernel, tt=tt, t_valid=t_valid, ns=ns),
        grid=(b // ns,),
        in_specs=[pl.BlockSpec((rows, CONV_DIM), lambda i: (i, _REST_A)),
                  pl.BlockSpec((rows, CONV_DIM), lambda i: (i, _REST_GL)),
                  pl.BlockSpec((ns, CONV_HALO, CONV_DIM), lambda i: (i, 0, 0)),
                  pl.BlockSpec((rows, CONV_DIM), lambda i: (i, _REST_ZC)),
                  pl.BlockSpec((CONV_HALO, CONV_DIM), lambda i: (0, 0)),
                  vec(), vec(), vec()],
        out_specs=[pl.BlockSpec((rows, CONV_DIM), lambda i: (i, 0)),
                   pl.BlockSpec((ns, CONV_HALO, CONV_DIM), lambda i: (i, 0, 0))],
        out_shape=[jax.ShapeDtypeStruct((b * tt, CONV_DIM), BF16),
                   jax.ShapeDtypeStruct((b, CONV_HALO, CONV_DIM), F32)],
        scratch_shapes=[pltpu.VMEM((ns, CONV_HALO + tt + SUBLANES, CONV_DIM), F32),
                        pltpu.VMEM((rows, CONV_DIM), F32)],
        compiler_params=_params("parallel"),
        name="conv_short",
    )(rest, rest, u_init, rest, w_pad, db, lg, lb)


def _outproj_kernel(on_ref, cv_ref, w_ref, x_ref, gate_ref, g_ref, y_ref):
    mix = _dot(on_ref[...], w_ref[0:NSA_WIDTH, :]) + _dot(cv_ref[...], w_ref[NSA_WIDTH:, :])
    nrm = mix * lax.rsqrt(jnp.mean(mix * mix, axis=-1, keepdims=True) + NORM_EPS) * g_ref[...]
    y_ref[...] = x_ref[...] + gate_ref[...] * nrm


def _outproj(o_nsa, cv, w_out_b, x, gate3, g_post, tm, rows_per_mod):
    m, d = x.shape
    mod_rows = gate3.shape[1]
    tiles_per_mod = rows_per_mod // tm
    return pl.pallas_call(
        _outproj_kernel,
        grid=(m // tm,),
        in_specs=[pl.BlockSpec((tm, NSA_WIDTH), lambda i: (i, 0)),
                  pl.BlockSpec((tm, CONV_DIM), lambda i: (i, 0)),
                  pl.BlockSpec((d, d), lambda i: (0, 0)),
                  pl.BlockSpec((tm, d), lambda i: (i, 0)),
                  pl.BlockSpec((None, mod_rows, d), lambda i: (i // tiles_per_mod, 0, 0)),
                  pl.BlockSpec((1, d), lambda i: (0, 0))],
        out_specs=pl.BlockSpec((tm, d), lambda i: (i, 0)),
        out_shape=jax.ShapeDtypeStruct((m, d), F32),
        compiler_params=_params("parallel"),
        name="outproj",
    )(o_nsa, cv, w_out_b, x, gate3, g_post)


def _overlap_matrix(n_cmp, n_sel, ncp):
    ci = np.arange(ncp)[:, None] * CMP_STRIDE
    sj = np.arange(LANES)[None, :] * SEL_BLOCK
    ov = (ci < sj + SEL_BLOCK) & (ci + CMP_BLOCK > sj)
    ov &= (np.arange(ncp)[:, None] < n_cmp) & (np.arange(LANES)[None, :] < n_sel)
    return jnp.asarray(ov.astype(np.float32), dtype=BF16)


def _overlap_matrix_t(n_cmp, n_sel, ncp):
    sr = _round_up(n_sel, 2 * SUBLANES)
    return _overlap_matrix(n_cmp, n_sel, ncp).T[:sr]


def _chunk_permutation():
    p = np.zeros((LANES, LANES), np.float32)
    for j in range(LANES // CMP_STRIDE):
        for l in range(CMP_STRIDE):
            p[l * (LANES // CMP_STRIDE) + j, CMP_STRIDE * j + l] = 1.0
    return jnp.asarray(p, dtype=BF16)


def _expand_matrix(n_keys):
    e = (np.arange(n_keys)[None, :] // SEL_BLOCK) == np.arange(LANES)[:, None]
    return jnp.asarray(e.astype(np.float32), dtype=BF16)


def _round_up(x, m):
    return -(-x // m) * m


def kernel(x_prompt, x_sample, c_prompt, c_sample, cache_kv, page_table, state_win_kv, state_conv, w_ada, b_ada, norm_pre, norm_post, w_in, cmp_pe, cmp_w1, cmp_b1, cmp_w2, cmp_b2, conv_dw, conv_db, conv_ln_g, conv_ln_b, w_out):
    assert w_ada.shape[0] == 1, "single-layer trunk"
    bp, tp, d = x_prompt.shape
    bs, ts, _ = x_sample.shape
    n_phys, page = cache_kv.shape[1], cache_kv.shape[2]
    n_pages = page_table.shape[1]
    past = n_pages * page
    assert ts <= SAMPLE_ROWS and d == D_MODEL

    w = w_in[0]
    c_q, c_kv, c_win, c_gate = 0, Q_COLS, Q_COLS + KV_COLS, Q_COLS + KV_COLS + WIN_COLS
    c_rest = c_gate + GATE_COLS
    w_main = jnp.concatenate([w[:, c_q:c_win].astype(BF16), w[:, c_rest:].astype(BF16),
                              w[:, c_win:c_gate].astype(BF16)], axis=1)
    wg = w[:, c_gate:c_rest].reshape(d, N_BRANCH, N_KV_HEADS, HEADS_PER_GROUP).transpose(0, 2, 1, 3)
    wg = wg.reshape(d, N_KV_HEADS, N_BRANCH * HEADS_PER_GROUP)
    w_gate = jnp.pad(wg, ((0, 0), (0, 0), (0, LANES - N_BRANCH * HEADS_PER_GROUP))).reshape(d, 2 * LANES).astype(BF16)
    w_out_b = w_out[0].astype(BF16)
    w1 = cmp_w1[0]
    w1cat = jnp.concatenate([w1[:, :CMP_STRIDE].reshape(2, CMP_STRIDE * HEAD_DIM, HEAD_DIM),
                             w1[:, CMP_STRIDE:].reshape(2, CMP_STRIDE * HEAD_DIM, HEAD_DIM)], axis=2).astype(BF16)
    w2b = cmp_w2[0].astype(BF16)
    pe_rows = jnp.pad(cmp_pe[0].reshape(2, 2, CMP_STRIDE * HEAD_DIM), ((0, 0), (0, SUBLANES - 2), (0, 0)))
    b1 = cmp_b1[0].reshape(2, 1, HEAD_DIM)
    b2 = cmp_b2[0].reshape(2, 1, HEAD_DIM)
    conv_w_pad = jnp.pad(conv_dw[0], ((0, CONV_HALO - CONV_WIDTH), (0, 0)))
    db = conv_db[0].reshape(1, CONV_DIM)
    lg = conv_ln_g[0].reshape(1, CONV_DIM)
    lb = conv_ln_b[0].reshape(1, CONV_DIM)
    g_pre = norm_pre[0].reshape(1, d)
    g_post = norm_post[0].reshape(1, d)

    n_c = bp + bs
    c_all = jnp.pad(jnp.concatenate([c_prompt, c_sample], axis=0), ((0, _round_up(n_c, SUBLANES) - n_c), (0, 0)))
    mod = _ada(c_all, w_ada[0], b_ada[0])
    shift, scale, gate = mod[:, :d], mod[:, d:2 * d], mod[:, 2 * d:]
    cconst = _cmpconst(pe_rows, w1cat, b1)

    tm_p = 512
    xp = x_prompt.reshape(bp * tp, d)
    q_p, kv_p, kvb_p, rest_p, winb_p, gate_p = _inproj(
        xp, g_pre, scale[:bp].reshape(bp, 1, d), shift[:bp].reshape(bp, 1, d), w_main, w_gate, tm_p, tp)
    n_cmp_p = (tp - CMP_BLOCK) // CMP_STRIDE + 1
    n_sel_p = -(-tp // SEL_BLOCK)
    ncp_p = _round_up(tp // CMP_STRIDE, LANES)
    kcv_p = _compress(kv_p.reshape(bp, tp * _KV_STREAMS, HEAD_DIM), w1cat, cconst, w2b, b2, ncp_p)
    tk = 512
    e_p = _expand_matrix(tp).reshape(LANES, tp // tk, tk).transpose(1, 0, 2)
    o_nsa_p = _attn_p(q_p, kcv_p, kvb_p.reshape(bp, tp, KV_COLS), winb_p.reshape(bp, tp, WIN_COLS), gate_p, rest_p,
                      _overlap_matrix_t(n_cmp_p, n_sel_p, ncp_p), e_p, bp, tp, n_cmp_p, n_sel_p, tk=tk)
    cv_p, st_p = _conv(rest_p, jnp.zeros((bp, CONV_HALO, CONV_DIM), F32), conv_w_pad, db, lg, lb, bp, tp, 128, 128)
    y_p = _outproj(o_nsa_p, cv_p, w_out_b, xp, gate[:bp].reshape(bp, 1, d), g_post, 256, tp)

    r = SAMPLE_ROWS
    xs = jnp.pad(x_sample, ((0, 0), (0, r - ts), (0, 0))).reshape(bs * r, d)
    per_row = lambda v: jnp.repeat(v[bp:bp + bs], r, axis=0)
    tm_s = min(512, bs * r)
    n_mod = (bs * r) // tm_s
    q_s, kv_s, kvb_s, rest_s, _, gate_s = _inproj(
        xs, g_pre, per_row(scale).reshape(n_mod, tm_s, d), per_row(shift).reshape(n_mod, tm_s, d),
        w_main, w_gate, tm_s, tm_s)
    total = past + ts
    n_cmp_s = (total - CMP_BLOCK) // CMP_STRIDE + 1
    n_sel_s = -(-total // SEL_BLOCK)
    assert n_cmp_s <= past // CMP_STRIDE - 1 + 1 and (n_cmp_s - 1) * CMP_STRIDE + CMP_BLOCK <= past
    ncp_s = _round_up(past // CMP_STRIDE, LANES)
    assert ncp_s == past // CMP_STRIDE
    n_ws = N_WIN_SLOTS * N_KV_HEADS
    o_nsa_s, win_state_s = _attn_s(
        page_table, cache_kv.reshape(n_phys, page * _KV_STREAMS, HEAD_DIM),
        state_win_kv.reshape(bs, WINDOW * n_ws, HEAD_DIM),
        q_s.reshape(bs, r, Q_COLS), kvb_s.reshape(bs, r, KV_COLS), rest_s.reshape(bs, r, _REST_COLS),
        gate_s.reshape(bs, r, 2 * LANES), w1cat, cconst, w2b, b2,
        _overlap_matrix(n_cmp_s, n_sel_s, ncp_s), _expand_matrix(past + LANES),
        n_cmp_s, n_sel_s, past, ts)
    u_init_s = jnp.pad(state_conv[0], ((0, 0), (CONV_HALO - (CONV_WIDTH - 1), 0), (0, 0)))
    cv_s, st_s = _conv_short(rest_s, u_init_s, conv_w_pad, db, lg, lb, bs, r, ts, math.gcd(bs, SUBLANES))
    y_s = _outproj(o_nsa_s.reshape(bs * r, NSA_WIDTH), cv_s, w_out_b, xs, per_row(gate).reshape(n_mod, tm_s, d),
                   g_post, tm_s, tm_s)

    keep = CONV_WIDTH - 1
    y_prompt = y_p.reshape(bp, tp, d)
    y_sample = y_s.reshape(bs, r, d)[:, :ts]
    kv_rows_prompt = kv_p.reshape(1, bp, tp, N_CACHE_SLOTS, N_KV_HEADS, HEAD_DIM)
    kv_rows_sample = kv_s.reshape(1, bs, r, N_CACHE_SLOTS, N_KV_HEADS, HEAD_DIM)[:, :, :ts]
    win_p = rest_p.reshape(bp, tp, _REST_COLS)[:, tp - min(WINDOW, tp):, _REST_WIN * WIN_COLS:]
    win_prompt = win_p.reshape(1, bp, min(WINDOW, tp), N_WIN_SLOTS, N_KV_HEADS, HEAD_DIM)
    win_sample = win_state_s.reshape(1, bs, WINDOW, N_WIN_SLOTS, N_KV_HEADS, HEAD_DIM)
    conv_prompt = st_p[:, CONV_HALO - keep:].reshape(1, bp, keep, CONV_DIM)
    conv_sample = st_s[:, CONV_HALO - keep:].reshape(1, bs, keep, CONV_DIM)
    return (y_prompt, y_sample, kv_rows_prompt, kv_rows_sample, win_prompt, win_sample, conv_prompt, conv_sample)
```

```python
import functools
import math

import numpy as np
import jax
import jax.numpy as jnp
from jax import lax
from jax.experimental import pallas as pl
from jax.experimental.pallas import tpu as pltpu

D_MODEL = 2048
HEAD_DIM = 128
N_HEADS = 8
N_KV_HEADS = 2
HEADS_PER_GROUP = N_HEADS // N_KV_HEADS
NSA_WIDTH = N_HEADS * HEAD_DIM
CONV_DIM = D_MODEL - NSA_WIDTH
CMP_BLOCK = 32
CMP_STRIDE = 16
SEL_BLOCK = 64
N_SELECT = 16
WINDOW = 512
CONV_WIDTH = 31
N_CACHE_SLOTS = 4
N_WIN_SLOTS = 2
N_BRANCH = 3
Q_COLS = N_HEADS * HEAD_DIM
KV_COLS = N_CACHE_SLOTS * N_KV_HEADS * HEAD_DIM
WIN_COLS = N_WIN_SLOTS * N_KV_HEADS * HEAD_DIM
GATE_COLS = N_BRANCH * N_HEADS
_KV_STREAMS = N_CACHE_SLOTS * N_KV_HEADS
NORM_EPS = 1e-6
NEG_INF = -1e30
FORCE_BONUS = 1e6
SM_SCALE = HEAD_DIM ** -0.5

LANES = 128
SUBLANES = 8
CONV_HALO = 32
SAMPLE_ROWS = 8
VMEM_LIMIT = 56 * 1024 * 1024

F32 = jnp.float32
BF16 = jnp.bfloat16


def _sigmoid(x):
    return 1.0 / (1.0 + jnp.exp(-x))


def _silu(x):
    return x * _sigmoid(x)


def _dot(a, b):
    return jnp.dot(a, b, preferred_element_type=F32)


def _dot_nt(a, b):
    return lax.dot_general(a, b, (((1,), (1,)), ((), ())), preferred_element_type=F32)


def _masked_softmax(s, mask):
    s = jnp.where(mask, s, NEG_INF)
    e = jnp.where(mask, jnp.exp(s - jnp.max(s, axis=-1, keepdims=True)), 0.0)
    return e / jnp.maximum(jnp.sum(e, axis=-1, keepdims=True), 1e-30)


def _params(*sem):
    return pltpu.CompilerParams(dimension_semantics=sem, vmem_limit_bytes=VMEM_LIMIT)


def _ada_kernel(c_ref, w_ref, b_ref, o_ref):
    o_ref[...] = _dot(_silu(c_ref[...]).astype(BF16), w_ref[...].astype(BF16)) + b_ref[...]


def _ada(c_all, w_ada, b_ada, tn=512):
    rows, d = c_all.shape
    n = w_ada.shape[1]
    return pl.pallas_call(
        _ada_kernel,
        grid=(n // tn,),
        in_specs=[pl.BlockSpec((rows, d), lambda j: (0, 0)),
                  pl.BlockSpec((d, tn), lambda j: (0, j)),
                  pl.BlockSpec((1, tn), lambda j: (0, j))],
        out_specs=pl.BlockSpec((rows, tn), lambda j: (0, j)),
        out_shape=jax.ShapeDtypeStruct((rows, n), F32),
        compiler_params=_params("parallel"),
        name="ada",
    )(c_all, w_ada, b_ada.reshape(1, n))


def _cmpconst_kernel(pe_ref, w1_ref, b1_ref, o_ref):
    y = _dot(pe_ref[...].astype(BF16), w1_ref[...])
    o_ref[...] = y[0:1, :HEAD_DIM] + y[1:2, HEAD_DIM:] + b1_ref[...]


def _cmpconst(pe_rows, w1cat, b1):
    return pl.pallas_call(
        _cmpconst_kernel,
        grid=(2,),
        in_specs=[pl.BlockSpec((None, SUBLANES, 16 * HEAD_DIM), lambda s: (s, 0, 0)),
                  pl.BlockSpec((None, 16 * HEAD_DIM, 2 * HEAD_DIM), lambda s: (s, 0, 0)),
                  pl.BlockSpec((None, 1, HEAD_DIM), lambda s: (s, 0, 0))],
        out_specs=pl.BlockSpec((None, 1, HEAD_DIM), lambda s: (s, 0, 0)),
        out_shape=jax.ShapeDtypeStruct((2, 1, HEAD_DIM), F32),
        compiler_params=_params("parallel"),
        name="cmpconst",
    )(pe_rows, w1cat, b1)


_TN = 512
_J_KV = Q_COLS // _TN
_J_REST = (Q_COLS + KV_COLS) // _TN
_REST_COLS = NSA_WIDTH + 3 * CONV_DIM + WIN_COLS
_N_MAIN = Q_COLS + KV_COLS + _REST_COLS
_J_WIN = (_N_MAIN - WIN_COLS) // _TN
_NJ = _N_MAIN // _TN
_REST_Z = 0
_REST_A = 1
_REST_GL = 2
_REST_ZC = 3
_REST_WIN = (NSA_WIDTH + 3 * CONV_DIM) // WIN_COLS


def _inproj_kernel(x_ref, g_ref, sc_ref, sh_ref, w_ref, wg_ref,
                   q_ref, kv_ref, kvb_ref, rest_ref, winb_ref, gate_ref, h_ref):
    j = pl.program_id(1)

    @pl.when(j == 0)
    def _():
        x = x_ref[...]
        y = x * lax.rsqrt(jnp.mean(x * x, axis=-1, keepdims=True) + NORM_EPS) * g_ref[...]
        h = (y * (1.0 + sc_ref[...]) + sh_ref[...]).astype(BF16)
        h_ref[...] = h
        gate_ref[...] = _dot(h, wg_ref[...])

    tile = lambda: _dot(h_ref[...], w_ref[...])

    @pl.when(j < _J_KV)
    def _():
        q_ref[...] = tile().astype(BF16)

    tm = h_ref.shape[0]
    streams_per_tile = _TN // HEAD_DIM
    for jj in range(_J_KV, _J_REST):
        @pl.when(j == jj)
        def _(jj=jj):
            acc = tile()
            kvb_ref[...] = acc.astype(BF16)
            for cc in range(streams_per_tile):
                c = (jj - _J_KV) * streams_per_tile + cc
                kv_ref[pl.ds(c, tm, stride=_KV_STREAMS), :] = acc[:, cc * HEAD_DIM:(cc + 1) * HEAD_DIM]

    @pl.when((j >= _J_REST) & (j != _J_WIN))
    def _():
        rest_ref[...] = tile()

    @pl.when(j == _J_WIN)
    def _():
        acc = tile()
        rest_ref[...] = acc
        winb_ref[...] = acc.astype(BF16)


def _inproj(x, g, scale3, shift3, w_main, w_gate, tm, rows_per_mod):
    m, d = x.shape
    mod_rows = scale3.shape[1]
    tiles_per_mod = rows_per_mod // tm

    def mod_map(i, j):
        return (i // tiles_per_mod, 0, 0)

    clampj = lambda j, lo, n: jnp.clip(j - lo, 0, n - 1)
    return pl.pallas_call(
        _inproj_kernel,
        grid=(m // tm, _NJ),
        in_specs=[pl.BlockSpec((tm, d), lambda i, j: (i, 0)),
                  pl.BlockSpec((1, d), lambda i, j: (0, 0)),
                  pl.BlockSpec((None, mod_rows, d), mod_map),
                  pl.BlockSpec((None, mod_rows, d), mod_map),
                  pl.BlockSpec((d, _TN), lambda i, j: (0, j)),
                  pl.BlockSpec((d, 2 * LANES), lambda i, j: (0, 0))],
        out_specs=[pl.BlockSpec((tm, _TN), lambda i, j: (i, clampj(j, 0, _J_KV))),
                   pl.BlockSpec((tm * _KV_STREAMS, HEAD_DIM), lambda i, j: (i, 0)),
                   pl.BlockSpec((tm, _TN), lambda i, j: (i, clampj(j, _J_KV, _J_REST - _J_KV))),
                   pl.BlockSpec((tm, _TN), lambda i, j: (i, clampj(j, _J_REST, _NJ - _J_REST))),
                   pl.BlockSpec((tm, _TN), lambda i, j: (i, 0)),
                   pl.BlockSpec((tm, 2 * LANES), lambda i, j: (i, 0))],
        out_shape=[jax.ShapeDtypeStruct((m, Q_COLS), BF16),
                   jax.ShapeDtypeStruct((m * _KV_STREAMS, HEAD_DIM), F32),
                   jax.ShapeDtypeStruct((m, KV_COLS), BF16),
                   jax.ShapeDtypeStruct((m, _REST_COLS), F32),
                   jax.ShapeDtypeStruct((m, WIN_COLS), BF16),
                   jax.ShapeDtypeStruct((m, 2 * LANES), F32)],
        scratch_shapes=[pltpu.VMEM((tm, d), BF16)],
        compiler_params=_params("parallel", "arbitrary"),
        name="inproj",
    )(x, g, scale3, shift3, w_main, w_gate)


def _compress_rows(load_rows, nch, w1, cconst, w2, b2):
    lhs = jnp.concatenate([load_rows(l) for l in range(CMP_STRIDE)], axis=1).astype(BF16)
    y = _dot(lhs, w1)
    second = pltpu.roll(y[:, HEAD_DIM:], shift=nch - 1, axis=0)
    h = _silu(y[:, :HEAD_DIM] + second + cconst)
    out = _dot(h.astype(BF16), w2) + b2
    row = lax.broadcasted_iota(jnp.int32, out.shape, 0)
    return jnp.where(row < nch - 1, out, 0.0)


def _compress_kernel(x_ref, w1_ref, cc_ref, w2_ref, b2_ref, o_ref, *, nch, ncp):
    for c in range(2 * N_KV_HEADS):
        s = c // N_KV_HEADS

        def load_rows(l, c=c):
            return x_ref[pl.ds(l * _KV_STREAMS + c, nch, stride=CMP_STRIDE * _KV_STREAMS), :]

        out = _compress_rows(load_rows, nch, w1_ref[s], cc_ref[s], w2_ref[s], b2_ref[s])
        if ncp > nch:
            out = jnp.concatenate([out, jnp.zeros((ncp - nch, HEAD_DIM), F32)], axis=0)
        o_ref[c] = out.astype(BF16)


def _compress(kv_streams, w1cat, cconst, w2b, b2, ncp):
    b, rows, _ = kv_streams.shape
    t = rows // _KV_STREAMS
    nch = t // CMP_STRIDE
    return pl.pallas_call(
        functools.partial(_compress_kernel, nch=nch, ncp=ncp),
        grid=(b,),
        in_specs=[pl.BlockSpec((None, rows, HEAD_DIM), lambda i: (i, 0, 0)),
                  pl.BlockSpec((2, 16 * HEAD_DIM, 2 * HEAD_DIM), lambda i: (0, 0, 0)),
                  pl.BlockSpec((2, 1, HEAD_DIM), lambda i: (0, 0, 0)),
                  pl.BlockSpec((2, HEAD_DIM, HEAD_DIM), lambda i: (0, 0, 0)),
                  pl.BlockSpec((2, 1, HEAD_DIM), lambda i: (0, 0, 0))],
        out_specs=pl.BlockSpec((None, 4, ncp, HEAD_DIM), lambda i: (i, 0, 0, 0)),
        out_shape=jax.ShapeDtypeStruct((b, 4, ncp, HEAD_DIM), BF16),
        compiler_params=_params("parallel"),
        name="compress",
    )(kv_streams, w1cat, cconst, w2b, b2)


def _cmp_branch(q_heads, kc, vc, qpos, n_cmp):
    rows = q_heads[0].shape[0]
    nio = lax.broadcasted_iota(jnp.int32, (rows, kc.shape[0]), 1)
    mask = (nio < n_cmp) & (nio * CMP_STRIDE + (CMP_BLOCK - 1) <= qpos)
    outs, probs = [], []
    for q in q_heads:
        p = _masked_softmax(_dot_nt(q, kc) * SM_SCALE, mask)
        outs.append(_dot(p.astype(BF16), vc))
        probs.append(p)
    return outs, probs


def _select_mask(psum, ov, qpos, n_sel):
    hi = psum.astype(BF16)
    lo = (psum - hi.astype(F32)).astype(BF16)
    imp = _dot(hi, ov) + _dot(lo, ov)
    rows = imp.shape[0]
    jio = lax.broadcasted_iota(jnp.int32, (rows, LANES), 1)
    allowed = (jio * SEL_BLOCK <= qpos) & (jio < n_sel)
    cur = qpos // SEL_BLOCK
    forced = (jio == 0) | (jio == cur) | (jio == cur - 1)
    score = jnp.where(allowed, imp + jnp.where(forced, FORCE_BONUS, 0.0), NEG_INF)
    rank = jnp.zeros((rows, LANES), F32)
    for i in range(n_sel):
        si = jnp.broadcast_to(score[:, i:i + 1], (rows, LANES))
        tie = jnp.where(jio > i, 1.0, 0.0)
        rank = rank + jnp.where(si > score, 1.0, jnp.where(si == score, tie, 0.0))
    return (rank < float(min(N_SELECT, n_sel))) & allowed


def _select_mask_t(psum, ovt, qpos_row, n_sel):
    hi = psum.astype(BF16)
    lo = (psum - hi.astype(F32)).astype(BF16)
    imp = _dot_nt(ovt, hi) + _dot_nt(ovt, lo)
    sr, rows = imp.shape
    jio = lax.broadcasted_iota(jnp.int32, (sr, rows), 0)
    allowed = (jio * SEL_BLOCK <= qpos_row) & (jio < n_sel)
    cur = qpos_row // SEL_BLOCK
    forced = (jio == 0) | (jio == cur) | (jio == cur - 1)
    score = jnp.where(allowed, imp + jnp.where(forced, FORCE_BONUS, 0.0), NEG_INF)
    rank = jnp.zeros((sr, rows), F32)
    for i in range(n_sel):
        si = jnp.broadcast_to(score[i:i + 1, :], (sr, rows))
        tie = jnp.where(jio > i, 1.0, 0.0)
        rank = rank + jnp.where(si > score, 1.0, jnp.where(si == score, tie, 0.0))
    sel_t = jnp.where((rank < float(min(N_SELECT, n_sel))) & allowed, 1.0, 0.0)
    sel_t = jnp.concatenate([sel_t, jnp.zeros((LANES - sr, rows), F32)], axis=0)
    return sel_t.T


def _biased_softmax(s, bias):
    s = s + bias
    e = jnp.exp(s - jnp.max(s, axis=-1, keepdims=True))
    return e / jnp.maximum(jnp.sum(e, axis=-1, keepdims=True), 1e-30)


def _attn_p_kernel(q_ref, kc_ref, vc_ref, ks_ref, vs_ref, kw_ref, vw_ref, gate_ref, z_ref, ovt_ref, e_ref,
                   o_ref, m_s, l_s, acc_s, pre_s, *, tq, tk, n_cmp, n_sel):
    qi = pl.program_id(2)
    q0 = qi * tq
    qpos = q0 + lax.broadcasted_iota(jnp.int32, (tq, 1), 0)
    heads = range(HEADS_PER_GROUP)
    q_heads = [q_ref[:, r * HEAD_DIM:(r + 1) * HEAD_DIM] for r in heads]
    gates = _sigmoid(gate_ref[...])
    g_of = lambda br, r: jnp.broadcast_to(
        gates[:, br * HEADS_PER_GROUP + r:br * HEADS_PER_GROUP + r + 1], (tq, HEAD_DIM))

    kc = kc_ref[...]
    vc = vc_ref[...]
    w0 = pl.multiple_of(jnp.maximum(q0 - WINDOW, 0), LANES)
    k_w = kw_ref[pl.ds(w0, WINDOW + tq), :]
    v_w = vw_ref[pl.ds(w0, WINDOW + tq), :]
    dist = qpos - (w0 + lax.broadcasted_iota(jnp.int32, (tq, WINDOW + tq), 1))
    wbias = jnp.where((dist >= 0) & (dist <= WINDOW), 0.0, NEG_INF)
    nio = lax.broadcasted_iota(jnp.int32, (tq, kc.shape[0]), 1)
    cmask = (nio < n_cmp) & (nio * CMP_STRIDE + (CMP_BLOCK - 1) <= qpos)

    s_cmp = [_dot_nt(q_heads[r], kc) * SM_SCALE for r in heads]
    s_win = [_dot_nt(q_heads[r], k_w) * SM_SCALE for r in heads]
    p_cmp = [_masked_softmax(s_cmp[r], cmask) for r in heads]
    psum = (p_cmp[0] + p_cmp[1]) + (p_cmp[2] + p_cmp[3])
    p_win = [_biased_softmax(s_win[r], wbias) for r in heads]
    qpos_row = q0 + lax.broadcasted_iota(jnp.int32, (1, tq), 1)
    selb = _select_mask_t(psum, ovt_ref[...], qpos_row, n_sel).astype(BF16)
    for r in heads:
        pre_s[r] = (g_of(0, r) * _dot(p_cmp[r].astype(BF16), vc)
                    + g_of(2, r) * _dot(p_win[r].astype(BF16), v_w))

    m_s[...] = jnp.full(m_s.shape, NEG_INF, F32)
    l_s[...] = jnp.zeros(l_s.shape, F32)
    acc_s[...] = jnp.zeros(acc_s.shape, F32)
    n_kt = (q0 + tq - 1) // tk + 1

    def sweep(kt, carry):
        k0 = pl.multiple_of(kt * tk, tk)
        k_t = ks_ref[pl.ds(k0, tk), :]
        v_t = vs_ref[pl.ds(k0, tk), :]
        kpos = k0 + lax.broadcasted_iota(jnp.int32, (tq, tk), 1)
        bias = jnp.where((_dot(selb, e_ref[kt]) > 0.5) & (kpos <= qpos), 0.0, NEG_INF)
        s = [_dot_nt(q_heads[r], k_t) * SM_SCALE + bias for r in heads]
        m_prev = [m_s[r] for r in heads]
        m_new = [jnp.maximum(m_prev[r], jnp.max(s[r], axis=-1, keepdims=True)) for r in heads]
        p = [jnp.exp(s[r] - jnp.concatenate([m_new[r]] * (tk // LANES), axis=1)) for r in heads]
        alpha = [jnp.exp(m_prev[r] - m_new[r]) for r in heads]
        for r in heads:
            l_s[r] = alpha[r] * l_s[r] + jnp.sum(p[r], axis=-1, keepdims=True)
            acc_s[r] = alpha[r] * acc_s[r] + _dot(p[r].astype(BF16), v_t)
            m_s[r] = m_new[r]
        return carry

    lax.fori_loop(0, n_kt, sweep, 0)

    for r in heads:
        o_slc = acc_s[r] / jnp.maximum(l_s[r], 1e-30)
        o = pre_s[r] + g_of(1, r) * o_slc
        o_ref[:, r * HEAD_DIM:(r + 1) * HEAD_DIM] = (
            o * _silu(z_ref[:, r * HEAD_DIM:(r + 1) * HEAD_DIM])).astype(BF16)


def _attn_p(q_bf, kcv, kv_bf3, win_bf3, gate, rest, ovt, e_p, b, t, n_cmp, n_sel, tq=128, tk=256):
    assert t >= WINDOW + tq and t % tk == 0 and tq == LANES
    nq = t // tq
    ncp = kcv.shape[2]
    sr = ovt.shape[0]
    gw = HEADS_PER_GROUP * HEAD_DIM
    row = lambda bi, g, qi: bi * nq + qi
    return pl.pallas_call(
        functools.partial(_attn_p_kernel, tq=tq, tk=tk, n_cmp=n_cmp, n_sel=n_sel),
        grid=(b, N_KV_HEADS, nq),
        in_specs=[pl.BlockSpec((tq, gw), lambda bi, g, qi: (row(bi, g, qi), g)),
                  pl.BlockSpec((None, None, ncp, HEAD_DIM), lambda bi, g, qi: (bi, g, 0, 0)),
                  pl.BlockSpec((None, None, ncp, HEAD_DIM), lambda bi, g, qi: (bi, N_KV_HEADS + g, 0, 0)),
                  pl.BlockSpec((None, t, HEAD_DIM), lambda bi, g, qi: (bi, 0, 2 * N_KV_HEADS + g)),
                  pl.BlockSpec((None, t, HEAD_DIM), lambda bi, g, qi: (bi, 0, 3 * N_KV_HEADS + g)),
                  pl.BlockSpec((None, t, HEAD_DIM), lambda bi, g, qi: (bi, 0, g)),
                  pl.BlockSpec((None, t, HEAD_DIM), lambda bi, g, qi: (bi, 0, N_KV_HEADS + g)),
                  pl.BlockSpec((tq, LANES), lambda bi, g, qi: (row(bi, g, qi), g)),
                  pl.BlockSpec((tq, gw), lambda bi, g, qi: (row(bi, g, qi), g)),
                  pl.BlockSpec((sr, ncp), lambda bi, g, qi: (0, 0)),
                  pl.BlockSpec((t // tk, LANES, tk), lambda bi, g, qi: (0, 0, 0))],
        out_specs=pl.BlockSpec((tq, gw), lambda bi, g, qi: (row(bi, g, qi), g)),
        out_shape=jax.ShapeDtypeStruct((b * t, NSA_WIDTH), BF16),
        scratch_shapes=[pltpu.VMEM((HEADS_PER_GROUP, tq, LANES), F32),
                        pltpu.VMEM((HEADS_PER_GROUP, tq, LANES), F32),
                        pltpu.VMEM((HEADS_PER_GROUP, tq, HEAD_DIM), F32),
                        pltpu.VMEM((HEADS_PER_GROUP, tq, HEAD_DIM), F32)],
        compiler_params=_params("parallel", "parallel", "arbitrary"),
        name="attn_p",
    )(q_bf, kcv, kcv, kv_bf3, kv_bf3, win_bf3, win_bf3, gate, rest, ovt, e_p)


def _attn_s_kernel(pt_ref, cache_ref, wst_ref, q_ref, kvn_ref, winn_ref, gate_ref, z_ref, winc_ref,
                   w1_ref, cc_ref, w2_ref, b2_ref, ov_ref, e_ref, perm_ref, o_ref, wout_ref,
                   kvbuf, wbuf, wnew, kcs, ksl, kwn, sem, wsem, osem,
                   *, n_pages, page, n_cmp, n_sel, pos0, t_new):
    i = pl.program_id(0)
    nb = pl.num_programs(0) - 1
    past = n_pages * page
    nch = past // CMP_STRIDE
    rows = HEADS_PER_GROUP * SAMPLE_ROWS
    n_ws = N_WIN_SLOTS * N_KV_HEADS
    state_rows = WINDOW * n_ws
    new_rows = t_new * n_ws
    prow = page * _KV_STREAMS
    c_seq = jnp.minimum(i, nb - 1)
    a_seq = jnp.maximum(i - 1, 0)
    cs = c_seq % 2
    asl = a_seq % 2

    def page_copy(seq, p, slot):
        return pltpu.make_async_copy(cache_ref.at[pt_ref[seq, p]], kvbuf.at[slot, pl.ds(p * prow, prow)],
                                     sem.at[slot])

    def win_copy(seq, slot):
        return pltpu.make_async_copy(wst_ref.at[seq], wbuf.at[slot], wsem.at[slot])

    def fetch(seq, slot):
        for p in range(n_pages):
            page_copy(seq, p, slot).start()
        win_copy(seq, slot).start()

    def state_copies(seq, slot):
        kept = pltpu.make_async_copy(wbuf.at[slot, pl.ds(new_rows, state_rows - new_rows)],
                                     wout_ref.at[seq, pl.ds(0, state_rows - new_rows)], osem.at[slot])
        fresh = pltpu.make_async_copy(wnew.at[slot], wout_ref.at[seq, pl.ds(state_rows - new_rows, new_rows)],
                                      osem.at[slot])
        return kept, fresh

    @pl.when(i == 0)
    def _():
        fetch(0, 0)
        kcs[0] = jnp.zeros(kcs.shape[1:], BF16)
        ksl[0] = jnp.zeros(ksl.shape[1:], BF16)
        kwn[0] = jnp.zeros(kwn.shape[1:], BF16)

    @pl.when(i >= 1)
    def _():
        for cp in state_copies(i - 1, (i - 1) % 2):
            cp.wait()

    @pl.when(i + 1 < nb)
    def _():
        fetch(i + 1, (i + 1) % 2)

    @pl.when(i < nb)
    def _():
        for p in range(n_pages):
            page_copy(i, p, i % 2).wait()
        win_copy(i, i % 2).wait()
        for t in range(t_new):
            for c in range(n_ws):
                wnew[i % 2, pl.ds(t * n_ws + c, 1), :] = winc_ref[pl.ds(t, 1), c * HEAD_DIM:(c + 1) * HEAD_DIM]
        for cp in state_copies(i, i % 2):
            cp.start()

    qpos = pos0 + lax.broadcasted_iota(jnp.int32, (rows, 1), 0) % SAMPLE_ROWS
    kvn = kvn_ref[...].astype(F32)
    qf = q_ref[...].astype(F32)
    gates = _sigmoid(gate_ref[...])
    zero_tail = jnp.zeros((LANES - SUBLANES, HEAD_DIM), F32)

    def tail_block(new_f32):
        return jnp.concatenate([new_f32, zero_tail], axis=0).astype(BF16)

    def scores(q, parked, tail):
        return jnp.concatenate([_dot_nt(q, parked), _dot_nt(q, tail)], axis=1) * SM_SCALE

    def weighted(p, parked, tail):
        n_old = parked.shape[0]
        return _dot(p[:, :n_old].astype(BF16), parked) + _dot(p[:, n_old:].astype(BF16), tail)

    def stack_heads(x8):
        return jnp.concatenate([x8] * HEADS_PER_GROUP, axis=0)

    groups = range(N_KV_HEADS)
    nk = past + LANES
    kpos = lax.broadcasted_iota(jnp.int32, (rows, nk), 1)
    wpos = (pos0 - WINDOW) + lax.broadcasted_iota(jnp.int32, (rows, WINDOW + LANES), 1)
    dist = qpos - wpos
    wmask = (dist >= 0) & (dist <= WINDOW) & (wpos >= 0)
    nio = lax.broadcasted_iota(jnp.int32, (rows, nch), 1)
    cmask = (nio < n_cmp) & (nio * CMP_STRIDE + (CMP_BLOCK - 1) <= qpos)

    qgs, s_cmp, s_slc, s_win, v_tails, vw_tails = [], [], [], [], [], []
    for g in groups:
        heads = [qf[:, (g * HEADS_PER_GROUP + r) * HEAD_DIM:(g * HEADS_PER_GROUP + r + 1) * HEAD_DIM]
                 for r in range(HEADS_PER_GROUP)]
        qgs.append(jnp.concatenate(heads, axis=0).astype(BF16))
    for g in groups:
        s_cmp.append(_dot_nt(qgs[g], kcs[asl, g]) * SM_SCALE)
    for g in groups:
        c_k = (2 * N_KV_HEADS + g) * HEAD_DIM
        c_v = (3 * N_KV_HEADS + g) * HEAD_DIM
        v_tails.append(tail_block(kvn[:, c_v:c_v + HEAD_DIM]))
        s_slc.append(scores(qgs[g], ksl[asl, g], tail_block(kvn[:, c_k:c_k + HEAD_DIM])))
    for g in groups:
        wk = g * HEAD_DIM
        wv = (N_KV_HEADS + g) * HEAD_DIM
        vw_tails.append(tail_block(winn_ref[:, wv:wv + HEAD_DIM]))
        s_win.append(scores(qgs[g], kwn[asl, g], tail_block(winn_ref[:, wk:wk + HEAD_DIM])))

    vcs = [kcs[asl, N_KV_HEADS + g] for g in groups]

    perm = perm_ref[...]
    blk = LANES
    cpb = blk // CMP_STRIDE
    stream = lambda c: kvbuf[cs, pl.ds(c, past, stride=_KV_STREAMS), :].astype(BF16)

    def slot_chunks(s):
        xs = jnp.concatenate([stream(s * N_KV_HEADS + g) for g in groups], axis=1)
        out = [[] for _ in groups]
        for k in range(past // blk):
            y = _dot(perm, xs[k * blk:(k + 1) * blk])
            for g in groups:
                out[g].append(jnp.concatenate(
                    [y[l * cpb:(l + 1) * cpb, g * HEAD_DIM:(g + 1) * HEAD_DIM] for l in range(CMP_STRIDE)], axis=1))
        return jnp.concatenate([jnp.concatenate(o, axis=0) for o in out], axis=0).astype(BF16)

    def compress_slot(s, lhs):
        y = _dot(lhs, w1_ref[s])
        second = pltpu.roll(y[:, HEAD_DIM:], shift=lhs.shape[0] - 1, axis=0)
        h = _silu(y[:, :HEAD_DIM] + second + cc_ref[s])
        out = _dot(h.astype(BF16), w2_ref[s]) + b2_ref[s]
        row = lax.broadcasted_iota(jnp.int32, out.shape, 0) % nch
        out = jnp.where(row < nch - 1, out, 0.0).astype(BF16)
        for g in groups:
            kcs[cs, s * N_KV_HEADS + g] = out[g * nch:(g + 1) * nch]

    lhs0 = slot_chunks(0)

    p_cmps = [_masked_softmax(s_cmp[g], cmask) for g in groups]
    psums = [(p[0:8] + p[8:16]) + (p[16:24] + p[24:32]) for p in p_cmps]
    sels = [_select_mask(psums[g], ov_ref[...], qpos[0:SAMPLE_ROWS], n_sel) for g in groups]
    selbs = [jnp.where(s, 1.0, 0.0).astype(BF16) for s in sels]

    compress_slot(0, lhs0)
    lhs1 = slot_chunks(1)

    smasks = [(stack_heads(_dot(selbs[g], e_ref[...])) > 0.5) & (kpos <= qpos) for g in groups]
    o_cmps = [_dot(p_cmps[g].astype(BF16), vcs[g]) for g in groups]
    p_wins = [_masked_softmax(s_win[g], wmask) for g in groups]
    o_wins = [weighted(p_wins[g], kwn[asl, N_KV_HEADS + g], vw_tails[g]) for g in groups]
    p_slcs = [_masked_softmax(s_slc[g], smasks[g]) for g in groups]
    o_slcs = [weighted(p_slcs[g], ksl[asl, N_KV_HEADS + g], v_tails[g]) for g in groups]

    compress_slot(1, lhs1)
    for k in range(2 * N_KV_HEADS):
        ksl[cs, k] = stream(2 * N_KV_HEADS + k)
    for k in range(n_ws):
        kwn[cs, k] = wbuf[cs, pl.ds(k, WINDOW, stride=n_ws), :].astype(BF16)

    for g in groups:
        o_cmp, o_slc, o_win = o_cmps[g], o_slcs[g], o_wins[g]
        for r in range(HEADS_PER_GROUP):
            h = g * HEADS_PER_GROUP + r
            rs = slice(r * SAMPLE_ROWS, (r + 1) * SAMPLE_ROWS)
            g_of = lambda br: jnp.broadcast_to(
                gates[:, g * LANES + br * HEADS_PER_GROUP + r:g * LANES + br * HEADS_PER_GROUP + r + 1],
                (SAMPLE_ROWS, HEAD_DIM))
            o = g_of(0) * o_cmp[rs] + g_of(1) * o_slc[rs] + g_of(2) * o_win[rs]
            o_ref[:, h * HEAD_DIM:(h + 1) * HEAD_DIM] = (
                o * _silu(z_ref[:, h * HEAD_DIM:(h + 1) * HEAD_DIM])).astype(BF16)


def _attn_s(page_table, cache3, wst3, q3, kvn3, rest3, gate3, w1cat, cconst, w2b, b2, ov, e_s,
            n_cmp, n_sel, pos0, t_new):
    nb, n_pages = page_table.shape
    page = cache3.shape[1] // _KV_STREAMS
    past = n_pages * page
    nch = past // CMP_STRIDE
    n_ws = N_WIN_SLOTS * N_KV_HEADS
    assert wst3.shape[1:] == (WINDOW * n_ws, HEAD_DIM) and past % LANES == 0
    ncp = ov.shape[0]
    prev = lambda i: jnp.maximum(i - 1, 0)
    grid_spec = pltpu.PrefetchScalarGridSpec(
        num_scalar_prefetch=1,
        grid=(nb + 1,),
        in_specs=[pl.BlockSpec(memory_space=pl.ANY),
                  pl.BlockSpec(memory_space=pl.ANY),
                  pl.BlockSpec((None, SAMPLE_ROWS, Q_COLS), lambda i, pt: (prev(i), 0, 0)),
                  pl.BlockSpec((None, SAMPLE_ROWS, KV_COLS), lambda i, pt: (prev(i), 0, 0)),
                  pl.BlockSpec((None, SAMPLE_ROWS, WIN_COLS), lambda i, pt: (prev(i), 0, _REST_WIN)),
                  pl.BlockSpec((None, SAMPLE_ROWS, 2 * LANES), lambda i, pt: (prev(i), 0, 0)),
                  pl.BlockSpec((None, SAMPLE_ROWS, NSA_WIDTH), lambda i, pt: (prev(i), 0, _REST_Z)),
                  pl.BlockSpec((None, SAMPLE_ROWS, WIN_COLS), lambda i, pt: (jnp.minimum(i, nb - 1), 0, _REST_WIN)),
                  pl.BlockSpec((2, 16 * HEAD_DIM, 2 * HEAD_DIM), lambda i, pt: (0, 0, 0)),
                  pl.BlockSpec((2, 1, HEAD_DIM), lambda i, pt: (0, 0, 0)),
                  pl.BlockSpec((2, HEAD_DIM, HEAD_DIM), lambda i, pt: (0, 0, 0)),
                  pl.BlockSpec((2, 1, HEAD_DIM), lambda i, pt: (0, 0, 0)),
                  pl.BlockSpec((ncp, LANES), lambda i, pt: (0, 0)),
                  pl.BlockSpec((LANES, past + LANES), lambda i, pt: (0, 0)),
                  pl.BlockSpec((LANES, LANES), lambda i, pt: (0, 0))],
        out_specs=[pl.BlockSpec((None, SAMPLE_ROWS, NSA_WIDTH), lambda i, pt: (prev(i), 0, 0)),
                   pl.BlockSpec(memory_space=pl.ANY)],
        scratch_shapes=[pltpu.VMEM((2, past * _KV_STREAMS, HEAD_DIM), F32),
                        pltpu.VMEM((2, WINDOW * n_ws, HEAD_DIM), F32),
                        pltpu.VMEM((2, t_new * n_ws, HEAD_DIM), F32),
                        pltpu.VMEM((2, 2 * N_KV_HEADS, nch, HEAD_DIM), BF16),
                        pltpu.VMEM((2, 2 * N_KV_HEADS, past, HEAD_DIM), BF16),
                        pltpu.VMEM((2, n_ws, WINDOW, HEAD_DIM), BF16),
                        pltpu.SemaphoreType.DMA((2,)),
                        pltpu.SemaphoreType.DMA((2,)),
                        pltpu.SemaphoreType.DMA((2,))],
    )
    return pl.pallas_call(
        functools.partial(_attn_s_kernel, n_pages=n_pages, page=page, n_cmp=n_cmp, n_sel=n_sel, pos0=pos0,
                          t_new=t_new),
        grid_spec=grid_spec,
        out_shape=[jax.ShapeDtypeStruct((nb, SAMPLE_ROWS, NSA_WIDTH), BF16),
                   jax.ShapeDtypeStruct((nb, WINDOW * n_ws, HEAD_DIM), F32)],
        compiler_params=_params("arbitrary"),
        name="attn_s",
    )(page_table, cache3, wst3, q3, kvn3, rest3, gate3, rest3, rest3, w1cat, cconst, w2b, b2, ov, e_s,
      _chunk_permutation())


_CONV_RC = 64
_CONV_LC = 256


def _depthwise_conv(uext, w_ref, db_ref, ybuf, tt, row0=0):
    off = CONV_HALO - (CONV_WIDTH - 1)
    uext[CONV_HALO + tt:CONV_HALO + tt + SUBLANES] = jnp.zeros((SUBLANES, CONV_DIM), F32)
    rc = min(_CONV_RC, tt)
    for r0 in range(0, tt, rc):
        for c0 in range(0, CONV_DIM, _CONV_LC):
            lanes = slice(c0, c0 + _CONV_LC)
            acc = jnp.broadcast_to(db_ref[:, lanes], (rc, _CONV_LC))
            for r in range(SUBLANES):
                z = None
                for a in range((CONV_WIDTH + off) // SUBLANES + 1):
                    k = SUBLANES * a + r - off
                    if 0 <= k < CONV_WIDTH:
                        term = w_ref[k:k + 1, lanes] * uext[r0 + SUBLANES * a:r0 + SUBLANES * a + rc + SUBLANES, lanes]
                        z = term if z is None else z + term
                acc = acc + z[r:r + rc]
            ybuf[row0 + r0:row0 + r0 + rc, lanes] = acc


def _conv_kernel(a_ref, gl_ref, ah_ref, glh_ref, init_ref, zc_ref, w_ref, db_ref, lg_ref, lb_ref,
                 cv_ref, st_ref, uext, ybuf, *, tt, t_valid):
    ti = pl.program_id(1)
    nt = pl.num_programs(1)
    halo = ah_ref[...] * _sigmoid(glh_ref[...])
    uext[0:CONV_HALO] = jnp.where(ti == 0, init_ref[...], halo)
    uext[CONV_HALO:CONV_HALO + tt] = a_ref[...] * _sigmoid(gl_ref[...])
    _depthwise_conv(uext, w_ref, db_ref, ybuf, tt)

    y = ybuf[...]
    mu = jnp.mean(y, axis=-1, keepdims=True)
    var = jnp.mean(jnp.square(y - mu), axis=-1, keepdims=True)
    ln = (y - mu) * lax.rsqrt(var + NORM_EPS) * lg_ref[...] + lb_ref[...]
    cv_ref[...] = (_silu(ln) * _silu(zc_ref[...])).astype(BF16)

    @pl.when(ti == nt - 1)
    def _():
        st_ref[...] = uext[pl.ds(t_valid, CONV_HALO), :]


def _conv(rest, u_init, w_pad, db, lg, lb, b, t, tt, t_valid):
    assert tt % CONV_HALO == 0 and t % tt == 0
    nt = t // tt
    hb = tt // CONV_HALO
    row = lambda bi, ti: bi * nt + ti
    halo_map = lambda col: (lambda bi, ti: (jnp.maximum(row(bi, ti) * hb - 1, 0), col))
    halo_rows = CONV_HALO
    vec = lambda: pl.BlockSpec((1, CONV_DIM), lambda bi, ti: (0, 0))
    return pl.pallas_call(
        functools.partial(_conv_kernel, tt=tt, t_valid=t_valid),
        grid=(b, nt),
        in_specs=[pl.BlockSpec((tt, CONV_DIM), lambda bi, ti: (row(bi, ti), _REST_A)),
                  pl.BlockSpec((tt, CONV_DIM), lambda bi, ti: (row(bi, ti), _REST_GL)),
                  pl.BlockSpec((halo_rows, CONV_DIM), halo_map(_REST_A)),
                  pl.BlockSpec((halo_rows, CONV_DIM), halo_map(_REST_GL)),
                  pl.BlockSpec((None, CONV_HALO, CONV_DIM), lambda bi, ti: (bi, 0, 0)),
                  pl.BlockSpec((tt, CONV_DIM), lambda bi, ti: (row(bi, ti), _REST_ZC)),
                  pl.BlockSpec((CONV_HALO, CONV_DIM), lambda bi, ti: (0, 0)),
                  vec(), vec(), vec()],
        out_specs=[pl.BlockSpec((tt, CONV_DIM), lambda bi, ti: (row(bi, ti), 0)),
                   pl.BlockSpec((None, CONV_HALO, CONV_DIM), lambda bi, ti: (bi, 0, 0))],
        out_shape=[jax.ShapeDtypeStruct((b * t, CONV_DIM), BF16),
                   jax.ShapeDtypeStruct((b, CONV_HALO, CONV_DIM), F32)],
        scratch_shapes=[pltpu.VMEM((CONV_HALO + tt + SUBLANES, CONV_DIM), F32),
                        pltpu.VMEM((tt, CONV_DIM), F32)],
        compiler_params=_params("parallel", "arbitrary"),
        name="conv",
    )(rest, rest, rest, rest, u_init, rest, w_pad, db, lg, lb)


def _conv_short_kernel(a_ref, gl_ref, init_ref, zc_ref, w_ref, db_ref, lg_ref, lb_ref,
                       cv_ref, st_ref, uext, ybuf, *, tt, t_valid, ns):
    u = a_ref[...] * _sigmoid(gl_ref[...])
    for s in range(ns):
        ue = uext.at[s]
        ue[0:CONV_HALO] = init_ref[s]
        ue[CONV_HALO:CONV_HALO + tt] = u[s * tt:(s + 1) * tt]
        _depthwise_conv(ue, w_ref, db_ref, ybuf, tt, row0=s * tt)
        st_ref[s] = ue[pl.ds(t_valid, CONV_HALO), :]
    y = ybuf[...]
    mu = jnp.mean(y, axis=-1, keepdims=True)
    var = jnp.mean(jnp.square(y - mu), axis=-1, keepdims=True)
    ln = (y - mu) * lax.rsqrt(var + NORM_EPS) * lg_ref[...] + lb_ref[...]
    cv_ref[...] = (_silu(ln) * _silu(zc_ref[...])).astype(BF16)


def _conv_short(rest, u_init, w_pad, db, lg, lb, b, tt, t_valid, ns):
    assert b % ns == 0 and tt % SUBLANES == 0
    rows = ns * tt
    vec = lambda: pl.BlockSpec((1, CONV_DIM), lambda i: (0, 0))
    return pl.pallas_call(
        functools.partial(_conv_short_kernel, tt=tt, t_valid=t_valid, ns=ns),
        grid=(b // ns,),
        in_specs=[pl.BlockSpec((rows, CONV_DIM), lambda i: (i, _REST_A)),
                  pl.BlockSpec((rows, CONV_DIM), lambda i: (i, _REST_GL)),
                  pl.BlockSpec((ns, CONV_HALO, CONV_DIM), lambda i: (i, 0, 0)),
                  pl.BlockSpec((rows, CONV_DIM), lambda i: (i, _REST_ZC)),
                  pl.BlockSpec((CONV_HALO, CONV_DIM), lambda i: (0, 0)),
                  vec(), vec(), vec()],
        out_specs=[pl.BlockSpec((rows, CONV_DIM), lambda i: (i, 0)),
                   pl.BlockSpec((ns, CONV_HALO, CONV_DIM), lambda i: (i, 0, 0))],
        out_shape=[jax.ShapeDtypeStruct((b * tt, CONV_DIM), BF16),
                   jax.ShapeDtypeStruct((b, CONV_HALO, CONV_DIM), F32)],
        scratch_shapes=[pltpu.VMEM((ns, CONV_HALO + tt + SUBLANES, CONV_DIM), F32),
                        pltpu.VMEM((rows, CONV_DIM), F32)],
        compiler_params=_params("parallel"),
        name="conv_short",
    )(rest, rest, u_init, rest, w_pad, db, lg, lb)


def _outproj_kernel(on_ref, cv_ref, w_ref, x_ref, gate_ref, g_ref, y_ref):
    mix = _dot(on_ref[...], w_ref[0:NSA_WIDTH, :]) + _dot(cv_ref[...], w_ref[NSA_WIDTH:, :])
    nrm = mix * lax.rsqrt(jnp.mean(mix * mix, axis=-1, keepdims=True) + NORM_EPS) * g_ref[...]
    y_ref[...] = x_ref[...] + gate_ref[...] * nrm


def _outproj(o_nsa, cv, w_out_b, x, gate3, g_post, tm, rows_per_mod):
    m, d = x.shape
    mod_rows = gate3.shape[1]
    tiles_per_mod = rows_per_mod // tm
    return pl.pallas_call(
        _outproj_kernel,
        grid=(m // tm,),
        in_specs=[pl.BlockSpec((tm, NSA_WIDTH), lambda i: (i, 0)),
                  pl.BlockSpec((tm, CONV_DIM), lambda i: (i, 0)),
                  pl.BlockSpec((d, d), lambda i: (0, 0)),
                  pl.BlockSpec((tm, d), lambda i: (i, 0)),
                  pl.BlockSpec((None, mod_rows, d), lambda i: (i // tiles_per_mod, 0, 0)),
                  pl.BlockSpec((1, d), lambda i: (0, 0))],
        out_specs=pl.BlockSpec((tm, d), lambda i: (i, 0)),
        out_shape=jax.ShapeDtypeStruct((m, d), F32),
        compiler_params=_params("parallel"),
        name="outproj",
    )(o_nsa, cv, w_out_b, x, gate3, g_post)


def _overlap_matrix(n_cmp, n_sel, ncp):
    ci = np.arange(ncp)[:, None] * CMP_STRIDE
    sj = np.arange(LANES)[None, :] * SEL_BLOCK
    ov = (ci < sj + SEL_BLOCK) & (ci + CMP_BLOCK > sj)
    ov &= (np.arange(ncp)[:, None] < n_cmp) & (np.arange(LANES)[None, :] < n_sel)
    return jnp.asarray(ov.astype(np.float32), dtype=BF16)


def _overlap_matrix_t(n_cmp, n_sel, ncp):
    sr = _round_up(n_sel, 2 * SUBLANES)
    return _overlap_matrix(n_cmp, n_sel, ncp).T[:sr]


def _chunk_permutation():
    p = np.zeros((LANES, LANES), np.float32)
    for j in range(LANES // CMP_STRIDE):
        for l in range(CMP_STRIDE):
            p[l * (LANES // CMP_STRIDE) + j, CMP_STRIDE * j + l] = 1.0
    return jnp.asarray(p, dtype=BF16)


def _expand_matrix(n_keys):
    e = (np.arange(n_keys)[None, :] // SEL_BLOCK) == np.arange(LANES)[:, None]
    return jnp.asarray(e.astype(np.float32), dtype=BF16)


def _round_up(x, m):
    return -(-x // m) * m


def kernel(x_prompt, x_sample, c_prompt, c_sample, cache_kv, page_table, state_win_kv, state_conv, w_ada, b_ada, norm_pre, norm_post, w_in, cmp_pe, cmp_w1, cmp_b1, cmp_w2, cmp_b2, conv_dw, conv_db, conv_ln_g, conv_ln_b, w_out):
    assert w_ada.shape[0] == 1, "single-layer trunk"
    bp, tp, d = x_prompt.shape
    bs, ts, _ = x_sample.shape
    n_phys, page = cache_kv.shape[1], cache_kv.shape[2]
    n_pages = page_table.shape[1]
    past = n_pages * page
    assert ts <= SAMPLE_ROWS and d == D_MODEL

    w = w_in[0]
    c_q, c_kv, c_win, c_gate = 0, Q_COLS, Q_COLS + KV_COLS, Q_COLS + KV_COLS + WIN_COLS
    c_rest = c_gate + GATE_COLS
    w_main = jnp.concatenate([w[:, c_q:c_win].astype(BF16), w[:, c_rest:].astype(BF16),
                              w[:, c_win:c_gate].astype(BF16)], axis=1)
    wg = w[:, c_gate:c_rest].reshape(d, N_BRANCH, N_KV_HEADS, HEADS_PER_GROUP).transpose(0, 2, 1, 3)
    wg = wg.reshape(d, N_KV_HEADS, N_BRANCH * HEADS_PER_GROUP)
    w_gate = jnp.pad(wg, ((0, 0), (0, 0), (0, LANES - N_BRANCH * HEADS_PER_GROUP))).reshape(d, 2 * LANES).astype(BF16)
    w_out_b = w_out[0].astype(BF16)
    w1 = cmp_w1[0]
    w1cat = jnp.concatenate([w1[:, :CMP_STRIDE].reshape(2, CMP_STRIDE * HEAD_DIM, HEAD_DIM),
                             w1[:, CMP_STRIDE:].reshape(2, CMP_STRIDE * HEAD_DIM, HEAD_DIM)], axis=2).astype(BF16)
    w2b = cmp_w2[0].astype(BF16)
    pe_rows = jnp.pad(cmp_pe[0].reshape(2, 2, CMP_STRIDE * HEAD_DIM), ((0, 0), (0, SUBLANES - 2), (0, 0)))
    b1 = cmp_b1[0].reshape(2, 1, HEAD_DIM)
    b2 = cmp_b2[0].reshape(2, 1, HEAD_DIM)
    conv_w_pad = jnp.pad(conv_dw[0], ((0, CONV_HALO - CONV_WIDTH), (0, 0)))
    db = conv_db[0].reshape(1, CONV_DIM)
    lg = conv_ln_g[0].reshape(1, CONV_DIM)
    lb = conv_ln_b[0].reshape(1, CONV_DIM)
    g_pre = norm_pre[0].reshape(1, d)
    g_post = norm_post[0].reshape(1, d)

    n_c = bp + bs
    c_all = jnp.pad(jnp.concatenate([c_prompt, c_sample], axis=0), ((0, _round_up(n_c, SUBLANES) - n_c), (0, 0)))
    mod = _ada(c_all, w_ada[0], b_ada[0])
    shift, scale, gate = mod[:, :d], mod[:, d:2 * d], mod[:, 2 * d:]
    cconst = _cmpconst(pe_rows, w1cat, b1)

    tm_p = 1024
    xp = x_prompt.reshape(bp * tp, d)
    q_p, kv_p, kvb_p, rest_p, winb_p, gate_p = _inproj(
        xp, g_pre, scale[:bp].reshape(bp, 1, d), shift[:bp].reshape(bp, 1, d), w_main, w_gate, tm_p, tp)
    n_cmp_p = (tp - CMP_BLOCK) // CMP_STRIDE + 1
    n_sel_p = -(-tp // SEL_BLOCK)
    ncp_p = _round_up(tp // CMP_STRIDE, LANES)
    kcv_p = _compress(kv_p.reshape(bp, tp * _KV_STREAMS, HEAD_DIM), w1cat, cconst, w2b, b2, ncp_p)
    tk = 512
    e_p = _expand_matrix(tp).reshape(LANES, tp // tk, tk).transpose(1, 0, 2)
    o_nsa_p = _attn_p(q_p, kcv_p, kvb_p.reshape(bp, tp, KV_COLS), winb_p.reshape(bp, tp, WIN_COLS), gate_p, rest_p,
                      _overlap_matrix_t(n_cmp_p, n_sel_p, ncp_p), e_p, bp, tp, n_cmp_p, n_sel_p, tk=tk)
    cv_p, st_p = _conv(rest_p, jnp.zeros((bp, CONV_HALO, CONV_DIM), F32), conv_w_pad, db, lg, lb, bp, tp, 128, 128)
    y_p = _outproj(o_nsa_p, cv_p, w_out_b, xp, gate[:bp].reshape(bp, 1, d), g_post, 256, tp)

    r = SAMPLE_ROWS
    xs = jnp.pad(x_sample, ((0, 0), (0, r - ts), (0, 0))).reshape(bs * r, d)
    per_row = lambda v: jnp.repeat(v[bp:bp + bs], r, axis=0)
    tm_s = min(512, bs * r)
    n_mod = (bs * r) // tm_s
    q_s, kv_s, kvb_s, rest_s, _, gate_s = _inproj(
        xs, g_pre, per_row(scale).reshape(n_mod, tm_s, d), per_row(shift).reshape(n_mod, tm_s, d),
        w_main, w_gate, tm_s, tm_s)
    total = past + ts
    n_cmp_s = (total - CMP_BLOCK) // CMP_STRIDE + 1
    n_sel_s = -(-total // SEL_BLOCK)
    assert n_cmp_s <= past // CMP_STRIDE - 1 + 1 and (n_cmp_s - 1) * CMP_STRIDE + CMP_BLOCK <= past
    ncp_s = _round_up(past // CMP_STRIDE, LANES)
    assert ncp_s == past // CMP_STRIDE
    n_ws = N_WIN_SLOTS * N_KV_HEADS
    o_nsa_s, win_state_s = _attn_s(
        page_table, cache_kv.reshape(n_phys, page * _KV_STREAMS, HEAD_DIM),
        state_win_kv.reshape(bs, WINDOW * n_ws, HEAD_DIM),
        q_s.reshape(bs, r, Q_COLS), kvb_s.reshape(bs, r, KV_COLS), rest_s.reshape(bs, r, _REST_COLS),
        gate_s.reshape(bs, r, 2 * LANES), w1cat, cconst, w2b, b2,
        _overlap_matrix(n_cmp_s, n_sel_s, ncp_s), _expand_matrix(past + LANES),
        n_cmp_s, n_sel_s, past, ts)
    u_init_s = jnp.pad(state_conv[0], ((0, 0), (CONV_HALO - (CONV_WIDTH - 1), 0), (0, 0)))
    cv_s, st_s = _conv_short(rest_s, u_init_s, conv_w_pad, db, lg, lb, bs, r, ts, math.gcd(bs, SUBLANES))
    y_s = _outproj(o_nsa_s.reshape(bs * r, NSA_WIDTH), cv_s, w_out_b, xs, per_row(gate).reshape(n_mod, tm_s, d),
                   g_post, tm_s, tm_s)

    keep = CONV_WIDTH - 1
    y_prompt = y_p.reshape(bp, tp, d)
    y_sample = y_s.reshape(bs, r, d)[:, :ts]
    kv_rows_prompt = kv_p.reshape(1, bp, tp, N_CACHE_SLOTS, N_KV_HEADS, HEAD_DIM)
    kv_rows_sample = kv_s.reshape(1, bs, r, N_CACHE_SLOTS, N_KV_HEADS, HEAD_DIM)[:, :, :ts]
    win_p = rest_p.reshape(bp, tp, _REST_COLS)[:, tp - min(WINDOW, tp):, _REST_WIN * WIN_COLS:]
    win_prompt = win_p.reshape(1, bp, min(WINDOW, tp), N_WIN_SLOTS, N_KV_HEADS, HEAD_DIM)
    win_sample = win_state_s.reshape(1, bs, WINDOW, N_WIN_SLOTS, N_KV_HEADS, HEAD_DIM)
    conv_prompt = st_p[:, CONV_HALO - keep:].reshape(1, bp, keep, CONV_DIM)
    conv_sample = st_s[:, CONV_HALO - keep:].reshape(1, bs, keep, CONV_DIM)
    return (y_prompt, y_sample, kv_rows_prompt, kv_rows_sample, win_prompt, win_sample, conv_prompt, conv_sample)
```

```python
import functools
import math

import numpy as np
import jax
import jax.numpy as jnp
from jax import lax
from jax.experimental import pallas as pl
from jax.experimental.pallas import tpu as pltpu

D_MODEL = 2048
HEAD_DIM = 128
N_HEADS = 8
N_KV_HEADS = 2
HEADS_PER_GROUP = N_HEADS // N_KV_HEADS
NSA_WIDTH = N_HEADS * HEAD_DIM
CONV_DIM = D_MODEL - NSA_WIDTH
CMP_BLOCK = 32
CMP_STRIDE = 16
SEL_BLOCK = 64
N_SELECT = 16
WINDOW = 512
CONV_WIDTH = 31
N_CACHE_SLOTS = 4
N_WIN_SLOTS = 2
N_BRANCH = 3
Q_COLS = N_HEADS * HEAD_DIM
KV_COLS = N_CACHE_SLOTS * N_KV_HEADS * HEAD_DIM
WIN_COLS = N_WIN_SLOTS * N_KV_HEADS * HEAD_DIM
GATE_COLS = N_BRANCH * N_HEADS
_KV_STREAMS = N_CACHE_SLOTS * N_KV_HEADS
NORM_EPS = 1e-6
NEG_INF = -1e30
FORCE_BONUS = 1e6
SM_SCALE_LOG2 = HEAD_DIM ** -0.5 * math.log2(math.e)

LANES = 128
SUBLANES = 8
CONV_HALO = 32
SAMPLE_ROWS = 8
VMEM_LIMIT = 56 * 1024 * 1024

F32 = jnp.float32
BF16 = jnp.bfloat16


def _sigmoid(x):
    return 1.0 / (1.0 + jnp.exp(-x))


def _silu(x):
    return x * _sigmoid(x)


def _dot(a, b):
    return jnp.dot(a, b, preferred_element_type=F32)


def _dot_nt(a, b):
    return lax.dot_general(a, b, (((1,), (1,)), ((), ())), preferred_element_type=F32)


def _masked_softmax(s, mask):
    s = jnp.where(mask, s, NEG_INF)
    e = jnp.where(mask, jnp.exp2(s - jnp.max(s, axis=-1, keepdims=True)), 0.0)
    return e / jnp.maximum(jnp.sum(e, axis=-1, keepdims=True), 1e-30)


def _params(*sem):
    return pltpu.CompilerParams(dimension_semantics=sem, vmem_limit_bytes=VMEM_LIMIT)


def _ada_kernel(c_ref, w_ref, b_ref, o_ref):
    o_ref[...] = _dot(_silu(c_ref[...]).astype(BF16), w_ref[...].astype(BF16)) + b_ref[...]


def _ada(c_all, w_ada, b_ada, tn=512):
    rows, d = c_all.shape
    n = w_ada.shape[1]
    return pl.pallas_call(
        _ada_kernel,
        grid=(n // tn,),
        in_specs=[pl.BlockSpec((rows, d), lambda j: (0, 0)),
                  pl.BlockSpec((d, tn), lambda j: (0, j)),
                  pl.BlockSpec((1, tn), lambda j: (0, j))],
        out_specs=pl.BlockSpec((rows, tn), lambda j: (0, j)),
        out_shape=jax.ShapeDtypeStruct((rows, n), F32),
        compiler_params=_params("parallel"),
        name="ada",
    )(c_all, w_ada, b_ada.reshape(1, n))


def _cmpconst_kernel(pe_ref, w1_ref, b1_ref, o_ref):
    y = _dot(pe_ref[...].astype(BF16), w1_ref[...])
    o_ref[...] = y[0:1, :HEAD_DIM] + y[1:2, HEAD_DIM:] + b1_ref[...]


def _cmpconst(pe_rows, w1cat, b1):
    return pl.pallas_call(
        _cmpconst_kernel,
        grid=(2,),
        in_specs=[pl.BlockSpec((None, SUBLANES, 16 * HEAD_DIM), lambda s: (s, 0, 0)),
                  pl.BlockSpec((None, 16 * HEAD_DIM, 2 * HEAD_DIM), lambda s: (s, 0, 0)),
                  pl.BlockSpec((None, 1, HEAD_DIM), lambda s: (s, 0, 0))],
        out_specs=pl.BlockSpec((None, 1, HEAD_DIM), lambda s: (s, 0, 0)),
        out_shape=jax.ShapeDtypeStruct((2, 1, HEAD_DIM), F32),
        compiler_params=_params("parallel"),
        name="cmpconst",
    )(pe_rows, w1cat, b1)


_TN = 512
_J_KV = Q_COLS // _TN
_J_REST = (Q_COLS + KV_COLS) // _TN
_REST_COLS = NSA_WIDTH + 3 * CONV_DIM + WIN_COLS
_N_MAIN = Q_COLS + KV_COLS + _REST_COLS
_J_WIN = (_N_MAIN - WIN_COLS) // _TN
_NJ = _N_MAIN // _TN
_REST_Z = 0
_REST_A = 1
_REST_GL = 2
_REST_ZC = 3
_REST_WIN = (NSA_WIDTH + 3 * CONV_DIM) // WIN_COLS


def _inproj_kernel(x_ref, g_ref, sc_ref, sh_ref, w_ref, wg_ref,
                   q_ref, kv_ref, kvb_ref, rest_ref, winb_ref, gate_ref, h_ref):
    j = pl.program_id(1)

    @pl.when(j == 0)
    def _():
        x = x_ref[...]
        y = x * lax.rsqrt(jnp.mean(x * x, axis=-1, keepdims=True) + NORM_EPS) * g_ref[...]
        h = (y * (1.0 + sc_ref[...]) + sh_ref[...]).astype(BF16)
        h_ref[...] = h
        gate_ref[...] = _dot(h, wg_ref[...])

    tile = lambda: _dot(h_ref[...], w_ref[...])

    @pl.when(j < _J_KV)
    def _():
        q_ref[...] = tile().astype(BF16)

    tm = h_ref.shape[0]
    streams_per_tile = _TN // HEAD_DIM
    for jj in range(_J_KV, _J_REST):
        @pl.when(j == jj)
        def _(jj=jj):
            acc = tile()
            kvb_ref[...] = acc.astype(BF16)
            for cc in range(streams_per_tile):
                c = (jj - _J_KV) * streams_per_tile + cc
                kv_ref[pl.ds(c, tm, stride=_KV_STREAMS), :] = acc[:, cc * HEAD_DIM:(cc + 1) * HEAD_DIM]

    @pl.when((j >= _J_REST) & (j != _J_WIN))
    def _():
        rest_ref[...] = tile()

    @pl.when(j == _J_WIN)
    def _():
        acc = tile()
        rest_ref[...] = acc
        winb_ref[...] = acc.astype(BF16)


def _inproj(x, g, scale3, shift3, w_main, w_gate, tm, rows_per_mod):
    m, d = x.shape
    mod_rows = scale3.shape[1]
    tiles_per_mod = rows_per_mod // tm

    def mod_map(i, j):
        return (i // tiles_per_mod, 0, 0)

    clampj = lambda j, lo, n: jnp.clip(j - lo, 0, n - 1)
    return pl.pallas_call(
        _inproj_kernel,
        grid=(m // tm, _NJ),
        in_specs=[pl.BlockSpec((tm, d), lambda i, j: (i, 0)),
                  pl.BlockSpec((1, d), lambda i, j: (0, 0)),
                  pl.BlockSpec((None, mod_rows, d), mod_map),
                  pl.BlockSpec((None, mod_rows, d), mod_map),
                  pl.BlockSpec((d, _TN), lambda i, j: (0, j)),
                  pl.BlockSpec((d, 2 * LANES), lambda i, j: (0, 0))],
        out_specs=[pl.BlockSpec((tm, _TN), lambda i, j: (i, clampj(j, 0, _J_KV))),
                   pl.BlockSpec((tm * _KV_STREAMS, HEAD_DIM), lambda i, j: (i, 0)),
                   pl.BlockSpec((tm, _TN), lambda i, j: (i, clampj(j, _J_KV, _J_REST - _J_KV))),
                   pl.BlockSpec((tm, _TN), lambda i, j: (i, clampj(j, _J_REST, _NJ - _J_REST))),
                   pl.BlockSpec((tm, _TN), lambda i, j: (i, 0)),
                   pl.BlockSpec((tm, 2 * LANES), lambda i, j: (i, 0))],
        out_shape=[jax.ShapeDtypeStruct((m, Q_COLS), BF16),
                   jax.ShapeDtypeStruct((m * _KV_STREAMS, HEAD_DIM), F32),
                   jax.ShapeDtypeStruct((m, KV_COLS), BF16),
                   jax.ShapeDtypeStruct((m, _REST_COLS), F32),
                   jax.ShapeDtypeStruct((m, WIN_COLS), BF16),
                   jax.ShapeDtypeStruct((m, 2 * LANES), F32)],
        scratch_shapes=[pltpu.VMEM((tm, d), BF16)],
        compiler_params=_params("parallel", "arbitrary"),
        name="inproj",
    )(x, g, scale3, shift3, w_main, w_gate)


def _compress_rows(load_rows, nch, w1, cconst, w2, b2):
    lhs = jnp.concatenate([load_rows(l) for l in range(CMP_STRIDE)], axis=1).astype(BF16)
    y = _dot(lhs, w1)
    second = pltpu.roll(y[:, HEAD_DIM:], shift=nch - 1, axis=0)
    h = _silu(y[:, :HEAD_DIM] + second + cconst)
    out = _dot(h.astype(BF16), w2) + b2
    row = lax.broadcasted_iota(jnp.int32, out.shape, 0)
    return jnp.where(row < nch - 1, out, 0.0)


def _compress_kernel(x_ref, w1_ref, cc_ref, w2_ref, b2_ref, o_ref, *, nch, ncp):
    for c in range(2 * N_KV_HEADS):
        s = c // N_KV_HEADS

        def load_rows(l, c=c):
            return x_ref[pl.ds(l * _KV_STREAMS + c, nch, stride=CMP_STRIDE * _KV_STREAMS), :]

        out = _compress_rows(load_rows, nch, w1_ref[s], cc_ref[s], w2_ref[s], b2_ref[s])
        if ncp > nch:
            out = jnp.concatenate([out, jnp.zeros((ncp - nch, HEAD_DIM), F32)], axis=0)
        o_ref[c] = out.astype(BF16)


def _compress(kv_streams, w1cat, cconst, w2b, b2, ncp):
    b, rows, _ = kv_streams.shape
    t = rows // _KV_STREAMS
    nch = t // CMP_STRIDE
    return pl.pallas_call(
        functools.partial(_compress_kernel, nch=nch, ncp=ncp),
        grid=(b,),
        in_specs=[pl.BlockSpec((None, rows, HEAD_DIM), lambda i: (i, 0, 0)),
                  pl.BlockSpec((2, 16 * HEAD_DIM, 2 * HEAD_DIM), lambda i: (0, 0, 0)),
                  pl.BlockSpec((2, 1, HEAD_DIM), lambda i: (0, 0, 0)),
                  pl.BlockSpec((2, HEAD_DIM, HEAD_DIM), lambda i: (0, 0, 0)),
                  pl.BlockSpec((2, 1, HEAD_DIM), lambda i: (0, 0, 0))],
        out_specs=pl.BlockSpec((None, 4, ncp, HEAD_DIM), lambda i: (i, 0, 0, 0)),
        out_shape=jax.ShapeDtypeStruct((b, 4, ncp, HEAD_DIM), BF16),
        compiler_params=_params("parallel"),
        name="compress",
    )(kv_streams, w1cat, cconst, w2b, b2)


def _select_mask(psum, ov, qpos, n_sel):
    hi = psum.astype(BF16)
    lo = (psum - hi.astype(F32)).astype(BF16)
    imp = _dot(hi, ov) + _dot(lo, ov)
    rows = imp.shape[0]
    jio = lax.broadcasted_iota(jnp.int32, (rows, LANES), 1)
    allowed = (jio * SEL_BLOCK <= qpos) & (jio < n_sel)
    cur = qpos // SEL_BLOCK
    forced = (jio == 0) | (jio == cur) | (jio == cur - 1)
    score = jnp.where(allowed, imp + jnp.where(forced, FORCE_BONUS, 0.0), NEG_INF)
    rank = jnp.zeros((rows, LANES), F32)
    for i in range(n_sel):
        si = jnp.broadcast_to(score[:, i:i + 1], (rows, LANES))
        tie = jnp.where(jio > i, 1.0, 0.0)
        rank = rank + jnp.where(si > score, 1.0, jnp.where(si == score, tie, 0.0))
    return (rank < float(min(N_SELECT, n_sel))) & allowed


def _select_mask_t(psum, ovt, qpos_row, n_sel):
    hi = psum.astype(BF16)
    lo = (psum - hi.astype(F32)).astype(BF16)
    imp = _dot_nt(ovt, hi) + _dot_nt(ovt, lo)
    sr, rows = imp.shape
    jio = lax.broadcasted_iota(jnp.int32, (sr, rows), 0)
    allowed = (jio * SEL_BLOCK <= qpos_row) & (jio < n_sel)
    cur = qpos_row // SEL_BLOCK
    forced = (jio == 0) | (jio == cur) | (jio == cur - 1)
    score = jnp.where(allowed, imp + jnp.where(forced, FORCE_BONUS, 0.0), NEG_INF)
    rank = jnp.zeros((sr, rows), F32)
    for i in range(n_sel):
        si = jnp.broadcast_to(score[i:i + 1, :], (sr, rows))
        tie = jnp.where(jio > i, 1.0, 0.0)
        rank = rank + jnp.where(si > score, 1.0, jnp.where(si == score, tie, 0.0))
    sel_t = jnp.where((rank < float(min(N_SELECT, n_sel))) & allowed, 1.0, 0.0)
    sel_t = jnp.concatenate([sel_t, jnp.zeros((LANES - sr, rows), F32)], axis=0)
    return sel_t.T


def _biased_softmax(s, bias):
    s = s + bias
    e = jnp.exp2(s - jnp.max(s, axis=-1, keepdims=True))
    return e / jnp.maximum(jnp.sum(e, axis=-1, keepdims=True), 1e-30)


def _attn_p_kernel(q_ref, kc_ref, vc_ref, ks_ref, vs_ref, kw_ref, vw_ref, gate_ref, z_ref, ovt_ref, e_ref,
                   o_ref, m_s, l_s, acc_s, pre_s, *, tq, tk, n_cmp, n_sel):
    qi = pl.program_id(2)
    st = LANES
    subs = range(tq // st)
    chains = [(u, r) for u in subs for r in range(HEADS_PER_GROUP)]
    rows = lambda u: slice(u * st, (u + 1) * st)
    cid = lambda u, r: u * HEADS_PER_GROUP + r
    q0s = [qi * tq + u * st for u in subs]
    qpos = [q0s[u] + lax.broadcasted_iota(jnp.int32, (st, 1), 0) for u in subs]
    q_of = lambda u, r: q_ref[rows(u), r * HEAD_DIM:(r + 1) * HEAD_DIM]
    gates = [_sigmoid(gate_ref[rows(u), :]) for u in subs]
    g_of = lambda br, u, r: jnp.broadcast_to(
        gates[u][:, br * HEADS_PER_GROUP + r:br * HEADS_PER_GROUP + r + 1], (st, HEAD_DIM))

    kc = kc_ref[...]
    vc = vc_ref[...]
    w0 = [pl.multiple_of(jnp.maximum(q0s[u] - WINDOW, 0), LANES) for u in subs]
    k_w = [kw_ref[pl.ds(w0[u], WINDOW + st), :] for u in subs]
    v_w = [vw_ref[pl.ds(w0[u], WINDOW + st), :] for u in subs]
    wbias = []
    for u in subs:
        dist = qpos[u] - (w0[u] + lax.broadcasted_iota(jnp.int32, (st, WINDOW + st), 1))
        wbias.append(jnp.where((dist >= 0) & (dist <= WINDOW), 0.0, NEG_INF))
    nio = lax.broadcasted_iota(jnp.int32, (st, kc.shape[0]), 1)
    cmask = [(nio < n_cmp) & (nio * CMP_STRIDE + (CMP_BLOCK - 1) <= qpos[u]) for u in subs]

    s_cmp = {c: _dot_nt(q_of(*c), kc) * SM_SCALE_LOG2 for c in chains}
    s_win = {c: _dot_nt(q_of(*c), k_w[c[0]]) * SM_SCALE_LOG2 for c in chains}
    p_cmp = {c: _masked_softmax(s_cmp[c], cmask[c[0]]) for c in chains}
    psum = [(p_cmp[(u, 0)] + p_cmp[(u, 1)]) + (p_cmp[(u, 2)] + p_cmp[(u, 3)]) for u in subs]
    p_win = {c: _biased_softmax(s_win[c], wbias[c[0]]) for c in chains}
    selb = []
    for u in subs:
        qpos_row = q0s[u] + lax.broadcasted_iota(jnp.int32, (1, st), 1)
        selb.append(_select_mask_t(psum[u], ovt_ref[...], qpos_row, n_sel).astype(BF16))
    for (u, r) in chains:
        pre_s[cid(u, r)] = (g_of(0, u, r) * _dot(p_cmp[(u, r)].astype(BF16), vc)
                            + g_of(2, u, r) * _dot(p_win[(u, r)].astype(BF16), v_w[u]))

    m_s[...] = jnp.full(m_s.shape, NEG_INF, F32)
    l_s[...] = jnp.zeros(l_s.shape, F32)
    acc_s[...] = jnp.zeros(acc_s.shape, F32)
    n_kt = (qi * tq + tq - 1) // tk + 1

    def sweep(kt, carry):
        k0 = pl.multiple_of(kt * tk, tk)
        k_t = ks_ref[pl.ds(k0, tk), :]
        v_t = vs_ref[pl.ds(k0, tk), :]
        kpos = k0 + lax.broadcasted_iota(jnp.int32, (st, tk), 1)
        bias = [jnp.where((_dot(selb[u], e_ref[kt]) > 0.5) & (kpos <= qpos[u]), 0.0, NEG_INF) for u in subs]
        s = {c: _dot_nt(q_of(*c), k_t) * SM_SCALE_LOG2 + bias[c[0]] for c in chains}
        m_prev = {c: m_s[cid(*c)] for c in chains}
        m_new = {c: jnp.maximum(m_prev[c], jnp.max(s[c], axis=-1, keepdims=True)) for c in chains}
        p = {c: jnp.exp2(s[c] - jnp.concatenate([m_new[c]] * (tk // LANES), axis=1)) for c in chains}
        alpha = {c: jnp.exp2(m_prev[c] - m_new[c]) for c in chains}
        for c in chains:
            k = cid(*c)
            l_s[k] = alpha[c] * l_s[k] + jnp.sum(p[c], axis=-1, keepdims=True)
            acc_s[k] = alpha[c] * acc_s[k] + _dot(p[c].astype(BF16), v_t)
            m_s[k] = m_new[c]
        return carry

    lax.fori_loop(0, n_kt, sweep, 0)

    for (u, r) in chains:
        k = cid(u, r)
        o_slc = acc_s[k] / jnp.maximum(l_s[k], 1e-30)
        o = pre_s[k] + g_of(1, u, r) * o_slc
        o_ref[rows(u), r * HEAD_DIM:(r + 1) * HEAD_DIM] = (
            o * _silu(z_ref[rows(u), r * HEAD_DIM:(r + 1) * HEAD_DIM])).astype(BF16)


def _attn_p(q_bf, kcv, kv_bf3, win_bf3, gate, rest, ovt, e_p, b, t, n_cmp, n_sel, tq=256, tk=256):
    assert t >= WINDOW + tq and t % tk == 0 and tq % LANES == 0 and t % tq == 0
    nq = t // tq
    n_chain = HEADS_PER_GROUP * (tq // LANES)
    ncp = kcv.shape[2]
    sr = ovt.shape[0]
    gw = HEADS_PER_GROUP * HEAD_DIM
    row = lambda bi, g, qi: bi * nq + qi
    return pl.pallas_call(
        functools.partial(_attn_p_kernel, tq=tq, tk=tk, n_cmp=n_cmp, n_sel=n_sel),
        grid=(b, N_KV_HEADS, nq),
        in_specs=[pl.BlockSpec((tq, gw), lambda bi, g, qi: (row(bi, g, qi), g)),
                  pl.BlockSpec((None, None, ncp, HEAD_DIM), lambda bi, g, qi: (bi, g, 0, 0)),
                  pl.BlockSpec((None, None, ncp, HEAD_DIM), lambda bi, g, qi: (bi, N_KV_HEADS + g, 0, 0)),
                  pl.BlockSpec((None, t, HEAD_DIM), lambda bi, g, qi: (bi, 0, 2 * N_KV_HEADS + g)),
                  pl.BlockSpec((None, t, HEAD_DIM), lambda bi, g, qi: (bi, 0, 3 * N_KV_HEADS + g)),
                  pl.BlockSpec((None, t, HEAD_DIM), lambda bi, g, qi: (bi, 0, g)),
                  pl.BlockSpec((None, t, HEAD_DIM), lambda bi, g, qi: (bi, 0, N_KV_HEADS + g)),
                  pl.BlockSpec((tq, LANES), lambda bi, g, qi: (row(bi, g, qi), g)),
                  pl.BlockSpec((tq, gw), lambda bi, g, qi: (row(bi, g, qi), g)),
                  pl.BlockSpec((sr, ncp), lambda bi, g, qi: (0, 0)),
                  pl.BlockSpec((t // tk, LANES, tk), lambda bi, g, qi: (0, 0, 0))],
        out_specs=pl.BlockSpec((tq, gw), lambda bi, g, qi: (row(bi, g, qi), g)),
        out_shape=jax.ShapeDtypeStruct((b * t, NSA_WIDTH), BF16),
        scratch_shapes=[pltpu.VMEM((n_chain, LANES, LANES), F32),
                        pltpu.VMEM((n_chain, LANES, LANES), F32),
                        pltpu.VMEM((n_chain, LANES, HEAD_DIM), F32),
                        pltpu.VMEM((n_chain, LANES, HEAD_DIM), F32)],
        compiler_params=_params("parallel", "parallel", "arbitrary"),
        name="attn_p",
    )(q_bf, kcv, kcv, kv_bf3, kv_bf3, win_bf3, win_bf3, gate, rest, ovt, e_p)


def _attn_s_kernel(pt_ref, cache_ref, wst_ref, q_ref, kvn_ref, winn_ref, gate_ref, z_ref, winc_ref,
                   w1_ref, cc_ref, w2_ref, b2_ref, ov_ref, e_ref, perm_ref, o_ref, wout_ref,
                   kvbuf, wbuf, wnew, kcs, ksl, kwn, sem, wsem, osem,
                   *, n_pages, page, n_cmp, n_sel, pos0, t_new):
    i = pl.program_id(0)
    nb = pl.num_programs(0) - 1
    past = n_pages * page
    nch = past // CMP_STRIDE
    rows = HEADS_PER_GROUP * SAMPLE_ROWS
    n_ws = N_WIN_SLOTS * N_KV_HEADS
    state_rows = WINDOW * n_ws
    new_rows = t_new * n_ws
    prow = page * _KV_STREAMS
    c_seq = jnp.minimum(i, nb - 1)
    a_seq = jnp.maximum(i - 1, 0)
    cs = c_seq % 2
    asl = a_seq % 2

    def page_copy(seq, p, slot):
        return pltpu.make_async_copy(cache_ref.at[pt_ref[seq, p]], kvbuf.at[slot, pl.ds(p * prow, prow)],
                                     sem.at[slot])

    def win_copy(seq, slot):
        return pltpu.make_async_copy(wst_ref.at[seq], wbuf.at[slot], wsem.at[slot])

    def fetch(seq, slot):
        for p in range(n_pages):
            page_copy(seq, p, slot).start()
        win_copy(seq, slot).start()

    def state_copies(seq, slot):
        kept = pltpu.make_async_copy(wbuf.at[slot, pl.ds(new_rows, state_rows - new_rows)],
                                     wout_ref.at[seq, pl.ds(0, state_rows - new_rows)], osem.at[slot])
        fresh = pltpu.make_async_copy(wnew.at[slot], wout_ref.at[seq, pl.ds(state_rows - new_rows, new_rows)],
                                      osem.at[slot])
        return kept, fresh

    @pl.when(i == 0)
    def _():
        fetch(0, 0)
        kcs[0] = jnp.zeros(kcs.shape[1:], BF16)
        ksl[0] = jnp.zeros(ksl.shape[1:], BF16)
        kwn[0] = jnp.zeros(kwn.shape[1:], BF16)

    @pl.when(i >= 1)
    def _():
        for cp in state_copies(i - 1, (i - 1) % 2):
            cp.wait()

    @pl.when(i + 1 < nb)
    def _():
        fetch(i + 1, (i + 1) % 2)

    @pl.when(i < nb)
    def _():
        for p in range(n_pages):
            page_copy(i, p, i % 2).wait()
        win_copy(i, i % 2).wait()
        for t in range(t_new):
            for c in range(n_ws):
                wnew[i % 2, pl.ds(t * n_ws + c, 1), :] = winc_ref[pl.ds(t, 1), c * HEAD_DIM:(c + 1) * HEAD_DIM]
        for cp in state_copies(i, i % 2):
            cp.start()

    qpos = pos0 + lax.broadcasted_iota(jnp.int32, (rows, 1), 0) % SAMPLE_ROWS
    kvn = kvn_ref[...].astype(F32)
    qf = q_ref[...].astype(F32)
    gates = _sigmoid(gate_ref[...])
    zero_tail = jnp.zeros((LANES - SUBLANES, HEAD_DIM), F32)

    def tail_block(new_f32):
        return jnp.concatenate([new_f32, zero_tail], axis=0).astype(BF16)

    def scores(q, parked, tail):
        return jnp.concatenate([_dot_nt(q, parked), _dot_nt(q, tail)], axis=1) * SM_SCALE_LOG2

    def weighted(p, parked, tail):
        n_old = parked.shape[0]
        return _dot(p[:, :n_old].astype(BF16), parked) + _dot(p[:, n_old:].astype(BF16), tail)

    def stack_heads(x8):
        return jnp.concatenate([x8] * HEADS_PER_GROUP, axis=0)

    groups = range(N_KV_HEADS)
    nk = past + LANES
    kpos = lax.broadcasted_iota(jnp.int32, (rows, nk), 1)
    wpos = (pos0 - WINDOW) + lax.broadcasted_iota(jnp.int32, (rows, WINDOW + LANES), 1)
    dist = qpos - wpos
    wmask = (dist >= 0) & (dist <= WINDOW) & (wpos >= 0)
    nio = lax.broadcasted_iota(jnp.int32, (rows, nch), 1)
    cmask = (nio < n_cmp) & (nio * CMP_STRIDE + (CMP_BLOCK - 1) <= qpos)

    stream = lambda c: kvbuf[cs, pl.ds(c, past, stride=_KV_STREAMS), :].astype(BF16)
    xs0 = jnp.concatenate([stream(g) for g in groups], axis=1)

    qgs, s_cmp, s_slc, s_win, v_tails, vw_tails = [], [], [], [], [], []
    for g in groups:
        heads = [qf[:, (g * HEADS_PER_GROUP + r) * HEAD_DIM:(g * HEADS_PER_GROUP + r + 1) * HEAD_DIM]
                 for r in range(HEADS_PER_GROUP)]
        qgs.append(jnp.concatenate(heads, axis=0).astype(BF16))
    for g in groups:
        s_cmp.append(_dot_nt(qgs[g], kcs[asl, g]) * SM_SCALE_LOG2)
    for g in groups:
        c_k = (2 * N_KV_HEADS + g) * HEAD_DIM
        c_v = (3 * N_KV_HEADS + g) * HEAD_DIM
        v_tails.append(tail_block(kvn[:, c_v:c_v + HEAD_DIM]))
        s_slc.append(scores(qgs[g], ksl[asl, g], tail_block(kvn[:, c_k:c_k + HEAD_DIM])))
    for g in groups:
        wk = g * HEAD_DIM
        wv = (N_KV_HEADS + g) * HEAD_DIM
        vw_tails.append(tail_block(winn_ref[:, wv:wv + HEAD_DIM]))
        s_win.append(scores(qgs[g], kwn[asl, g], tail_block(winn_ref[:, wk:wk + HEAD_DIM])))

    vcs = [kcs[asl, N_KV_HEADS + g] for g in groups]

    perm = perm_ref[...]
    blk = LANES
    cpb = blk // CMP_STRIDE

    def slot_streams(s):
        return jnp.concatenate([stream(s * N_KV_HEADS + g) for g in groups], axis=1)

    def slot_chunks(xs):
        out = [[] for _ in groups]
        for k in range(past // blk):
            y = _dot(perm, xs[k * blk:(k + 1) * blk])
            for g in groups:
                out[g].append(jnp.concatenate(
                    [y[l * cpb:(l + 1) * cpb, g * HEAD_DIM:(g + 1) * HEAD_DIM] for l in range(CMP_STRIDE)], axis=1))
        return jnp.concatenate([jnp.concatenate(o, axis=0) for o in out], axis=0).astype(BF16)

    def compress_slot(s, lhs):
        y = _dot(lhs, w1_ref[s])
        second = pltpu.roll(y[:, HEAD_DIM:], shift=lhs.shape[0] - 1, axis=0)
        h = _silu(y[:, :HEAD_DIM] + second + cc_ref[s])
        out = _dot(h.astype(BF16), w2_ref[s]) + b2_ref[s]
        row = lax.broadcasted_iota(jnp.int32, out.shape, 0) % nch
        out = jnp.where(row < nch - 1, out, 0.0).astype(BF16)
        for g in groups:
            kcs[cs, s * N_KV_HEADS + g] = out[g * nch:(g + 1) * nch]

    xs1 = slot_streams(1)
    lhs0 = slot_chunks(xs0)

    p_cmps = [_masked_softmax(s_cmp[g], cmask) for g in groups]
    psums = [(p[0:8] + p[8:16]) + (p[16:24] + p[24:32]) for p in p_cmps]
    sels = [_select_mask(psums[g], ov_ref[...], qpos[0:SAMPLE_ROWS], n_sel) for g in groups]
    selbs = [jnp.where(s, 1.0, 0.0).astype(BF16) for s in sels]

    for k in range(N_KV_HEADS):
        ksl[cs, k] = stream(2 * N_KV_HEADS + k)
    compress_slot(0, lhs0)
    lhs1 = slot_chunks(xs1)

    smasks = [(stack_heads(_dot(selbs[g], e_ref[...])) > 0.5) & (kpos <= qpos) for g in groups]
    o_cmps = [_dot(p_cmps[g].astype(BF16), vcs[g]) for g in groups]
    p_wins = [_masked_softmax(s_win[g], wmask) for g in groups]
    o_wins = [weighted(p_wins[g], kwn[asl, N_KV_HEADS + g], vw_tails[g]) for g in groups]
    p_slcs = [_masked_softmax(s_slc[g], smasks[g]) for g in groups]
    o_slcs = [weighted(p_slcs[g], ksl[asl, N_KV_HEADS + g], v_tails[g]) for g in groups]

    for k in range(N_KV_HEADS, 2 * N_KV_HEADS):
        ksl[cs, k] = stream(2 * N_KV_HEADS + k)
    compress_slot(1, lhs1)
    for k in range(n_ws):
        kwn[cs, k] = wbuf[cs, pl.ds(k, WINDOW, stride=n_ws), :].astype(BF16)

    for g in groups:
        o_cmp, o_slc, o_win = o_cmps[g], o_slcs[g], o_wins[g]
        for r in range(HEADS_PER_GROUP):
            h = g * HEADS_PER_GROUP + r
            rs = slice(r * SAMPLE_ROWS, (r + 1) * SAMPLE_ROWS)
            g_of = lambda br: jnp.broadcast_to(
                gates[:, g * LANES + br * HEADS_PER_GROUP + r:g * LANES + br * HEADS_PER_GROUP + r + 1],
                (SAMPLE_ROWS, HEAD_DIM))
            o = g_of(0) * o_cmp[rs] + g_of(1) * o_slc[rs] + g_of(2) * o_win[rs]
            o_ref[:, h * HEAD_DIM:(h + 1) * HEAD_DIM] = (
                o * _silu(z_ref[:, h * HEAD_DIM:(h + 1) * HEAD_DIM])).astype(BF16)


def _attn_s(page_table, cache3, wst3, q3, kvn3, rest3, gate3, w1cat, cconst, w2b, b2, ov, e_s,
            n_cmp, n_sel, pos0, t_new):
    nb, n_pages = page_table.shape
    page = cache3.shape[1] // _KV_STREAMS
    past = n_pages * page
    nch = past // CMP_STRIDE
    n_ws = N_WIN_SLOTS * N_KV_HEADS
    assert wst3.shape[1:] == (WINDOW * n_ws, HEAD_DIM) and past % LANES == 0
    ncp = ov.shape[0]
    prev = lambda i: jnp.maximum(i - 1, 0)
    grid_spec = pltpu.PrefetchScalarGridSpec(
        num_scalar_prefetch=1,
        grid=(nb + 1,),
        in_specs=[pl.BlockSpec(memory_space=pl.ANY),
                  pl.BlockSpec(memory_space=pl.ANY),
                  pl.BlockSpec((None, SAMPLE_ROWS, Q_COLS), lambda i, pt: (prev(i), 0, 0)),
                  pl.BlockSpec((None, SAMPLE_ROWS, KV_COLS), lambda i, pt: (prev(i), 0, 0)),
                  pl.BlockSpec((None, SAMPLE_ROWS, WIN_COLS), lambda i, pt: (prev(i), 0, _REST_WIN)),
                  pl.BlockSpec((None, SAMPLE_ROWS, 2 * LANES), lambda i, pt: (prev(i), 0, 0)),
                  pl.BlockSpec((None, SAMPLE_ROWS, NSA_WIDTH), lambda i, pt: (prev(i), 0, _REST_Z)),
                  pl.BlockSpec((None, SAMPLE_ROWS, WIN_COLS), lambda i, pt: (jnp.minimum(i, nb - 1), 0, _REST_WIN)),
                  pl.BlockSpec((2, 16 * HEAD_DIM, 2 * HEAD_DIM), lambda i, pt: (0, 0, 0)),
                  pl.BlockSpec((2, 1, HEAD_DIM), lambda i, pt: (0, 0, 0)),
                  pl.BlockSpec((2, HEAD_DIM, HEAD_DIM), lambda i, pt: (0, 0, 0)),
                  pl.BlockSpec((2, 1, HEAD_DIM), lambda i, pt: (0, 0, 0)),
                  pl.BlockSpec((ncp, LANES), lambda i, pt: (0, 0)),
                  pl.BlockSpec((LANES, past + LANES), lambda i, pt: (0, 0)),
                  pl.BlockSpec((LANES, LANES), lambda i, pt: (0, 0))],
        out_specs=[pl.BlockSpec((None, SAMPLE_ROWS, NSA_WIDTH), lambda i, pt: (prev(i), 0, 0)),
                   pl.BlockSpec(memory_space=pl.ANY)],
        scratch_shapes=[pltpu.VMEM((2, past * _KV_STREAMS, HEAD_DIM), F32),
                        pltpu.VMEM((2, WINDOW * n_ws, HEAD_DIM), F32),
                        pltpu.VMEM((2, t_new * n_ws, HEAD_DIM), F32),
                        pltpu.VMEM((2, 2 * N_KV_HEADS, nch, HEAD_DIM), BF16),
                        pltpu.VMEM((2, 2 * N_KV_HEADS, past, HEAD_DIM), BF16),
                        pltpu.VMEM((2, n_ws, WINDOW, HEAD_DIM), BF16),
                        pltpu.SemaphoreType.DMA((2,)),
                        pltpu.SemaphoreType.DMA((2,)),
                        pltpu.SemaphoreType.DMA((2,))],
    )
    return pl.pallas_call(
        functools.partial(_attn_s_kernel, n_pages=n_pages, page=page, n_cmp=n_cmp, n_sel=n_sel, pos0=pos0,
                          t_new=t_new),
        grid_spec=grid_spec,
        out_shape=[jax.ShapeDtypeStruct((nb, SAMPLE_ROWS, NSA_WIDTH), BF16),
                   jax.ShapeDtypeStruct((nb, WINDOW * n_ws, HEAD_DIM), F32)],
        compiler_params=_params("arbitrary"),
        name="attn_s",
    )(page_table, cache3, wst3, q3, kvn3, rest3, gate3, rest3, rest3, w1cat, cconst, w2b, b2, ov, e_s,
      _chunk_permutation())


_CONV_RC = 64
_CONV_LC = 256


def _depthwise_conv(uext, w_ref, db_ref, ybuf, tt, row0=0):
    off = CONV_HALO - (CONV_WIDTH - 1)
    uext[CONV_HALO + tt:CONV_HALO + tt + SUBLANES] = jnp.zeros((SUBLANES, CONV_DIM), F32)
    rc = min(_CONV_RC, tt)
    for r0 in range(0, tt, rc):
        for c0 in range(0, CONV_DIM, _CONV_LC):
            lanes = slice(c0, c0 + _CONV_LC)
            acc = jnp.broadcast_to(db_ref[:, lanes], (rc, _CONV_LC))
            for r in range(SUBLANES):
                z = None
                for a in range((CONV_WIDTH + off) // SUBLANES + 1):
                    k = SUBLANES * a + r - off
                    if 0 <= k < CONV_WIDTH:
                        term = w_ref[k:k + 1, lanes] * uext[r0 + SUBLANES * a:r0 + SUBLANES * a + rc + SUBLANES, lanes]
                        z = term if z is None else z + term
                acc = acc + z[r:r + rc]
            ybuf[row0 + r0:row0 + r0 + rc, lanes] = acc


def _conv_kernel(a_ref, gl_ref, ah_ref, glh_ref, init_ref, zc_ref, w_ref, db_ref, lg_ref, lb_ref,
                 cv_ref, st_ref, uext, ybuf, *, tt, t_valid):
    ti = pl.program_id(1)
    nt = pl.num_programs(1)
    halo = ah_ref[...] * _sigmoid(glh_ref[...])
    uext[0:CONV_HALO] = jnp.where(ti == 0, init_ref[...], halo)
    uext[CONV_HALO:CONV_HALO + tt] = a_ref[...] * _sigmoid(gl_ref[...])
    _depthwise_conv(uext, w_ref, db_ref, ybuf, tt)

    y = ybuf[...]
    mu = jnp.mean(y, axis=-1, keepdims=True)
    var = jnp.mean(jnp.square(y - mu), axis=-1, keepdims=True)
    ln = (y - mu) * lax.rsqrt(var + NORM_EPS) * lg_ref[...] + lb_ref[...]
    cv_ref[...] = (_silu(ln) * _silu(zc_ref[...])).astype(BF16)

    @pl.when(ti == nt - 1)
    def _():
        st_ref[...] = uext[pl.ds(t_valid, CONV_HALO), :]


def _conv(rest, u_init, w_pad, db, lg, lb, b, t, tt, t_valid):
    assert tt % CONV_HALO == 0 and t % tt == 0
    nt = t // tt
    hb = tt // CONV_HALO
    row = lambda bi, ti: bi * nt + ti
    halo_map = lambda col: (lambda bi, ti: (jnp.maximum(row(bi, ti) * hb - 1, 0), col))
    halo_rows = CONV_HALO
    vec = lambda: pl.BlockSpec((1, CONV_DIM), lambda bi, ti: (0, 0))
    return pl.pallas_call(
        functools.partial(_conv_kernel, tt=tt, t_valid=t_valid),
        grid=(b, nt),
        in_specs=[pl.BlockSpec((tt, CONV_DIM), lambda bi, ti: (row(bi, ti), _REST_A)),
                  pl.BlockSpec((tt, CONV_DIM), lambda bi, ti: (row(bi, ti), _REST_GL)),
                  pl.BlockSpec((halo_rows, CONV_DIM), halo_map(_REST_A)),
                  pl.BlockSpec((halo_rows, CONV_DIM), halo_map(_REST_GL)),
                  pl.BlockSpec((None, CONV_HALO, CONV_DIM), lambda bi, ti: (bi, 0, 0)),
                  pl.BlockSpec((tt, CONV_DIM), lambda bi, ti: (row(bi, ti), _REST_ZC)),
                  pl.BlockSpec((CONV_HALO, CONV_DIM), lambda bi, ti: (0, 0)),
                  vec(), vec(), vec()],
        out_specs=[pl.BlockSpec((tt, CONV_DIM), lambda bi, ti: (row(bi, ti), 0)),
                   pl.BlockSpec((None, CONV_HALO, CONV_DIM), lambda bi, ti: (bi, 0, 0))],
        out_shape=[jax.ShapeDtypeStruct((b * t, CONV_DIM), BF16),
                   jax.ShapeDtypeStruct((b, CONV_HALO, CONV_DIM), F32)],
        scratch_shapes=[pltpu.VMEM((CONV_HALO + tt + SUBLANES, CONV_DIM), F32),
                        pltpu.VMEM((tt, CONV_DIM), F32)],
        compiler_params=_params("parallel", "arbitrary"),
        name="conv",
    )(rest, rest, rest, rest, u_init, rest, w_pad, db, lg, lb)


def _conv_short_kernel(a_ref, gl_ref, init_ref, zc_ref, w_ref, db_ref, lg_ref, lb_ref,
                       cv_ref, st_ref, uext, ybuf, *, tt, t_valid, ns):
    u = a_ref[...] * _sigmoid(gl_ref[...])
    for s in range(ns):
        ue = uext.at[s]
        ue[0:CONV_HALO] = init_ref[s]
        ue[CONV_HALO:CONV_HALO + tt] = u[s * tt:(s + 1) * tt]
        _depthwise_conv(ue, w_ref, db_ref, ybuf, tt, row0=s * tt)
        st_ref[s] = ue[pl.ds(t_valid, CONV_HALO), :]
    y = ybuf[...]
    mu = jnp.mean(y, axis=-1, keepdims=True)
    var = jnp.mean(jnp.square(y - mu), axis=-1, keepdims=True)
    ln = (y - mu) * lax.rsqrt(var + NORM_EPS) * lg_ref[...] + lb_ref[...]
    cv_ref[...] = (_silu(ln) * _silu(zc_ref[...])).astype(BF16)


def _conv_short(rest, u_init, w_pad, db, lg, lb, b, tt, t_valid, ns):
    assert b % ns == 0 and tt % SUBLANES == 0
    rows = ns * tt
    vec = lambda: pl.BlockSpec((1, CONV_DIM), lambda i: (0, 0))
    return pl.pallas_call(
        functools.partial(_conv_short_kernel, tt=tt, t_valid=t_valid, ns=ns),
        grid=(b // ns,),
        in_specs=[pl.BlockSpec((rows, CONV_DIM), lambda i: (i, _REST_A)),
                  pl.BlockSpec((rows, CONV_DIM), lambda i: (i, _REST_GL)),
                  pl.BlockSpec((ns, CONV_HALO, CONV_DIM), lambda i: (i, 0, 0)),
                  pl.BlockSpec((rows, CONV_DIM), lambda i: (i, _REST_ZC)),
                  pl.BlockSpec((CONV_HALO, CONV_DIM), lambda i: (0, 0)),
                  vec(), vec(), vec()],
        out_specs=[pl.BlockSpec((rows, CONV_DIM), lambda i: (i, 0)),
                   pl.BlockSpec((ns, CONV_HALO, CONV_DIM), lambda i: (i, 0, 0))],
        out_shape=[jax.ShapeDtypeStruct((b * tt, CONV_DIM), BF16),
                   jax.ShapeDtypeStruct((b, CONV_HALO, CONV_DIM), F32)],
        scratch_shapes=[pltpu.VMEM((ns, CONV_HALO + tt + SUBLANES, CONV_DIM), F32),
                        pltpu.VMEM((rows, CONV_DIM), F32)],
        compiler_params=_params("parallel"),
        name="conv_short",
    )(rest, rest, u_init, rest, w_pad, db, lg, lb)


def _outproj_kernel(on_ref, cv_ref, w_ref, x_ref, gate_ref, g_ref, y_ref):
    mix = _dot(on_ref[...], w_ref[0:NSA_WIDTH, :]) + _dot(cv_ref[...], w_ref[NSA_WIDTH:, :])
    nrm = mix * lax.rsqrt(jnp.mean(mix * mix, axis=-1, keepdims=True) + NORM_EPS) * g_ref[...]
    y_ref[...] = x_ref[...] + gate_ref[...] * nrm


def _outproj(o_nsa, cv, w_out_b, x, gate3, g_post, tm, rows_per_mod):
    m, d = x.shape
    mod_rows = gate3.shape[1]
    tiles_per_mod = rows_per_mod // tm
    return pl.pallas_call(
        _outproj_kernel,
        grid=(m // tm,),
        in_specs=[pl.BlockSpec((tm, NSA_WIDTH), lambda i: (i, 0)),
                  pl.BlockSpec((tm, CONV_DIM), lambda i: (i, 0)),
                  pl.BlockSpec((d, d), lambda i: (0, 0)),
                  pl.BlockSpec((tm, d), lambda i: (i, 0)),
                  pl.BlockSpec((None, mod_rows, d), lambda i: (i // tiles_per_mod, 0, 0)),
                  pl.BlockSpec((1, d), lambda i: (0, 0))],
        out_specs=pl.BlockSpec((tm, d), lambda i: (i, 0)),
        out_shape=jax.ShapeDtypeStruct((m, d), F32),
        compiler_params=_params("parallel"),
        name="outproj",
    )(o_nsa, cv, w_out_b, x, gate3, g_post)


def _overlap_matrix(n_cmp, n_sel, ncp):
    ci = np.arange(ncp)[:, None] * CMP_STRIDE
    sj = np.arange(LANES)[None, :] * SEL_BLOCK
    ov = (ci < sj + SEL_BLOCK) & (ci + CMP_BLOCK > sj)
    ov &= (np.arange(ncp)[:, None] < n_cmp) & (np.arange(LANES)[None, :] < n_sel)
    return jnp.asarray(ov.astype(np.float32), dtype=BF16)


def _overlap_matrix_t(n_cmp, n_sel, ncp):
    sr = _round_up(n_sel, 2 * SUBLANES)
    return _overlap_matrix(n_cmp, n_sel, ncp).T[:sr]


def _chunk_permutation():
    p = np.zeros((LANES, LANES), np.float32)
    for j in range(LANES // CMP_STRIDE):
        for l in range(CMP_STRIDE):
            p[l * (LANES // CMP_STRIDE) + j, CMP_STRIDE * j + l] = 1.0
    return jnp.asarray(p, dtype=BF16)


def _expand_matrix(n_keys):
    e = (np.arange(n_keys)[None, :] // SEL_BLOCK) == np.arange(LANES)[:, None]
    return jnp.asarray(e.astype(np.float32), dtype=BF16)


def _round_up(x, m):
    return -(-x // m) * m


def kernel(x_prompt, x_sample, c_prompt, c_sample, cache_kv, page_table, state_win_kv, state_conv, w_ada, b_ada, norm_pre, norm_post, w_in, cmp_pe, cmp_w1, cmp_b1, cmp_w2, cmp_b2, conv_dw, conv_db, conv_ln_g, conv_ln_b, w_out):
    assert w_ada.shape[0] == 1, "single-layer trunk"
    bp, tp, d = x_prompt.shape
    bs, ts, _ = x_sample.shape
    n_phys, page = cache_kv.shape[1], cache_kv.shape[2]
    n_pages = page_table.shape[1]
    past = n_pages * page
    assert ts <= SAMPLE_ROWS and d == D_MODEL

    w = w_in[0]
    c_q, c_kv, c_win, c_gate = 0, Q_COLS, Q_COLS + KV_COLS, Q_COLS + KV_COLS + WIN_COLS
    c_rest = c_gate + GATE_COLS
    w_main = jnp.concatenate([w[:, c_q:c_win].astype(BF16), w[:, c_rest:].astype(BF16),
                              w[:, c_win:c_gate].astype(BF16)], axis=1)
    wg = w[:, c_gate:c_rest].reshape(d, N_BRANCH, N_KV_HEADS, HEADS_PER_GROUP).transpose(0, 2, 1, 3)
    wg = wg.reshape(d, N_KV_HEADS, N_BRANCH * HEADS_PER_GROUP)
    w_gate = jnp.pad(wg, ((0, 0), (0, 0), (0, LANES - N_BRANCH * HEADS_PER_GROUP))).reshape(d, 2 * LANES).astype(BF16)
    w_out_b = w_out[0].astype(BF16)
    w1 = cmp_w1[0]
    w1cat = jnp.concatenate([w1[:, :CMP_STRIDE].reshape(2, CMP_STRIDE * HEAD_DIM, HEAD_DIM),
                             w1[:, CMP_STRIDE:].reshape(2, CMP_STRIDE * HEAD_DIM, HEAD_DIM)], axis=2).astype(BF16)
    w2b = cmp_w2[0].astype(BF16)
    pe_rows = jnp.pad(cmp_pe[0].reshape(2, 2, CMP_STRIDE * HEAD_DIM), ((0, 0), (0, SUBLANES - 2), (0, 0)))
    b1 = cmp_b1[0].reshape(2, 1, HEAD_DIM)
    b2 = cmp_b2[0].reshape(2, 1, HEAD_DIM)
    conv_w_pad = jnp.pad(conv_dw[0], ((0, CONV_HALO - CONV_WIDTH), (0, 0)))
    db = conv_db[0].reshape(1, CONV_DIM)
    lg = conv_ln_g[0].reshape(1, CONV_DIM)
    lb = conv_ln_b[0].reshape(1, CONV_DIM)
    g_pre = norm_pre[0].reshape(1, d)
    g_post = norm_post[0].reshape(1, d)

    n_c = bp + bs
    c_all = jnp.pad(jnp.concatenate([c_prompt, c_sample], axis=0), ((0, _round_up(n_c, SUBLANES) - n_c), (0, 0)))
    mod = _ada(c_all, w_ada[0], b_ada[0])
    shift, scale, gate = mod[:, :d], mod[:, d:2 * d], mod[:, 2 * d:]
    cconst = _cmpconst(pe_rows, w1cat, b1)

    tm_p = 1024
    xp = x_prompt.reshape(bp * tp, d)
    q_p, kv_p, kvb_p, rest_p, winb_p, gate_p = _inproj(
        xp, g_pre, scale[:bp].reshape(bp, 1, d), shift[:bp].reshape(bp, 1, d), w_main, w_gate, tm_p, tp)
    n_cmp_p = (tp - CMP_BLOCK) // CMP_STRIDE + 1
    n_sel_p = -(-tp // SEL_BLOCK)
    ncp_p = _round_up(tp // CMP_STRIDE, LANES)
    kcv_p = _compress(kv_p.reshape(bp, tp * _KV_STREAMS, HEAD_DIM), w1cat, cconst, w2b, b2, ncp_p)
    tk = 512
    e_p = _expand_matrix(tp).reshape(LANES, tp // tk, tk).transpose(1, 0, 2)
    o_nsa_p = _attn_p(q_p, kcv_p, kvb_p.reshape(bp, tp, KV_COLS), winb_p.reshape(bp, tp, WIN_COLS), gate_p, rest_p,
                      _overlap_matrix_t(n_cmp_p, n_sel_p, ncp_p), e_p, bp, tp, n_cmp_p, n_sel_p, tk=tk)
    cv_p, st_p = _conv(rest_p, jnp.zeros((bp, CONV_HALO, CONV_DIM), F32), conv_w_pad, db, lg, lb, bp, tp, 256, 256)
    y_p = _outproj(o_nsa_p, cv_p, w_out_b, xp, gate[:bp].reshape(bp, 1, d), g_post, 512, tp)

    r = SAMPLE_ROWS
    xs = jnp.pad(x_sample, ((0, 0), (0, r - ts), (0, 0))).reshape(bs * r, d)
    per_row = lambda v: jnp.repeat(v[bp:bp + bs], r, axis=0)
    tm_s = min(512, bs * r)
    n_mod = (bs * r) // tm_s
    q_s, kv_s, kvb_s, rest_s, _, gate_s = _inproj(
        xs, g_pre, per_row(scale).reshape(n_mod, tm_s, d), per_row(shift).reshape(n_mod, tm_s, d),
        w_main, w_gate, tm_s, tm_s)
    total = past + ts
    n_cmp_s = (total - CMP_BLOCK) // CMP_STRIDE + 1
    n_sel_s = -(-total // SEL_BLOCK)
    assert n_cmp_s <= past // CMP_STRIDE - 1 + 1 and (n_cmp_s - 1) * CMP_STRIDE + CMP_BLOCK <= past
    ncp_s = _round_up(past // CMP_STRIDE, LANES)
    assert ncp_s == past // CMP_STRIDE
    n_ws = N_WIN_SLOTS * N_KV_HEADS
    o_nsa_s, win_state_s = _attn_s(
        page_table, cache_kv.reshape(n_phys, page * _KV_STREAMS, HEAD_DIM),
        state_win_kv.reshape(bs, WINDOW * n_ws, HEAD_DIM),
        q_s.reshape(bs, r, Q_COLS), kvb_s.reshape(bs, r, KV_COLS), rest_s.reshape(bs, r, _REST_COLS),
        gate_s.reshape(bs, r, 2 * LANES), w1cat, cconst, w2b, b2,
        _overlap_matrix(n_cmp_s, n_sel_s, ncp_s), _expand_matrix(past + LANES),
        n_cmp_s, n_sel_s, past, ts)
    u_init_s = jnp.pad(state_conv[0], ((0, 0), (CONV_HALO - (CONV_WIDTH - 1), 0), (0, 0)))
    cv_s, st_s = _conv_short(rest_s, u_init_s, conv_w_pad, db, lg, lb, bs, r, ts, math.gcd(bs, SUBLANES))
    y_s = _outproj(o_nsa_s.reshape(bs * r, NSA_WIDTH), cv_s, w_out_b, xs, per_row(gate).reshape(n_mod, tm_s, d),
                   g_post, tm_s, tm_s)

    keep = CONV_WIDTH - 1
    y_prompt = y_p.reshape(bp, tp, d)
    y_sample = y_s.reshape(bs, r, d)[:, :ts]
    kv_rows_prompt = kv_p.reshape(1, bp, tp, N_CACHE_SLOTS, N_KV_HEADS, HEAD_DIM)
    kv_rows_sample = kv_s.reshape(1, bs, r, N_CACHE_SLOTS, N_KV_HEADS, HEAD_DIM)[:, :, :ts]
    win_p = rest_p.reshape(bp, tp, _REST_COLS)[:, tp - min(WINDOW, tp):, _REST_WIN * WIN_COLS:]
    win_prompt = win_p.reshape(1, bp, min(WINDOW, tp), N_WIN_SLOTS, N_KV_HEADS, HEAD_DIM)
    win_sample = win_state_s.reshape(1, bs, WINDOW, N_WIN_SLOTS, N_KV_HEADS, HEAD_DIM)
    conv_prompt = st_p[:, CONV_HALO - keep:].reshape(1, bp, keep, CONV_DIM)
    conv_sample = st_s[:, CONV_HALO - keep:].reshape(1, bs, keep, CONV_DIM)
    return (y_prompt, y_sample, kv_rows_prompt, kv_rows_sample, win_prompt, win_sample, conv_prompt, conv_sample)
```

```python
import functools
import math

import numpy as np
import jax
import jax.numpy as jnp
from jax import lax
from jax.experimental import pallas as pl
from jax.experimental.pallas import tpu as pltpu

D_MODEL = 2048
HEAD_DIM = 128
N_HEADS = 8
N_KV_HEADS = 2
HEADS_PER_GROUP = N_HEADS // N_KV_HEADS
NSA_WIDTH = N_HEADS * HEAD_DIM
CONV_DIM = D_MODEL - NSA_WIDTH
CMP_BLOCK = 32
CMP_STRIDE = 16
SEL_BLOCK = 64
N_SELECT = 16
WINDOW = 512
CONV_WIDTH = 31
N_CACHE_SLOTS = 4
N_WIN_SLOTS = 2
N_BRANCH = 3
Q_COLS = N_HEADS * HEAD_DIM
KV_COLS = N_CACHE_SLOTS * N_KV_HEADS * HEAD_DIM
WIN_COLS = N_WIN_SLOTS * N_KV_HEADS * HEAD_DIM
GATE_COLS = N_BRANCH * N_HEADS
_KV_STREAMS = N_CACHE_SLOTS * N_KV_HEADS
NORM_EPS = 1e-6
NEG_INF = -1e30
FORCE_BONUS = 1e6
SM_SCALE_LOG2 = HEAD_DIM ** -0.5 * math.log2(math.e)

LANES = 128
SUBLANES = 8
CONV_HALO = 32
SAMPLE_ROWS = 8
VMEM_LIMIT = 56 * 1024 * 1024

F32 = jnp.float32
BF16 = jnp.bfloat16


def _sigmoid(x):
    return 1.0 / (1.0 + jnp.exp(-x))


def _silu(x):
    return x * _sigmoid(x)


def _dot(a, b):
    return jnp.dot(a, b, preferred_element_type=F32)


def _dot_nt(a, b):
    return lax.dot_general(a, b, (((1,), (1,)), ((), ())), preferred_element_type=F32)


def _masked_softmax(s, mask):
    s = jnp.where(mask, s, NEG_INF)
    e = jnp.where(mask, jnp.exp2(s - jnp.max(s, axis=-1, keepdims=True)), 0.0)
    return e / jnp.maximum(jnp.sum(e, axis=-1, keepdims=True), 1e-30)


def _params(*sem):
    return pltpu.CompilerParams(dimension_semantics=sem, vmem_limit_bytes=VMEM_LIMIT)


def _ada_kernel(c_ref, w_ref, b_ref, o_ref):
    o_ref[...] = _dot(_silu(c_ref[...]).astype(BF16), w_ref[...].astype(BF16)) + b_ref[...]


def _ada(c_all, w_ada, b_ada, tn=512):
    rows, d = c_all.shape
    n = w_ada.shape[1]
    return pl.pallas_call(
        _ada_kernel,
        grid=(n // tn,),
        in_specs=[pl.BlockSpec((rows, d), lambda j: (0, 0)),
                  pl.BlockSpec((d, tn), lambda j: (0, j)),
                  pl.BlockSpec((1, tn), lambda j: (0, j))],
        out_specs=pl.BlockSpec((rows, tn), lambda j: (0, j)),
        out_shape=jax.ShapeDtypeStruct((rows, n), F32),
        compiler_params=_params("parallel"),
        name="ada",
    )(c_all, w_ada, b_ada.reshape(1, n))


def _cmpconst_kernel(pe_ref, w1_ref, b1_ref, o_ref):
    y = _dot(pe_ref[...].astype(BF16), w1_ref[...])
    o_ref[...] = y[0:1, :HEAD_DIM] + y[1:2, HEAD_DIM:] + b1_ref[...]


def _cmpconst(pe_rows, w1cat, b1):
    return pl.pallas_call(
        _cmpconst_kernel,
        grid=(2,),
        in_specs=[pl.BlockSpec((None, SUBLANES, 16 * HEAD_DIM), lambda s: (s, 0, 0)),
                  pl.BlockSpec((None, 16 * HEAD_DIM, 2 * HEAD_DIM), lambda s: (s, 0, 0)),
                  pl.BlockSpec((None, 1, HEAD_DIM), lambda s: (s, 0, 0))],
        out_specs=pl.BlockSpec((None, 1, HEAD_DIM), lambda s: (s, 0, 0)),
        out_shape=jax.ShapeDtypeStruct((2, 1, HEAD_DIM), F32),
        compiler_params=_params("parallel"),
        name="cmpconst",
    )(pe_rows, w1cat, b1)


_TN = 512
_J_KV = Q_COLS // _TN
_J_REST = (Q_COLS + KV_COLS) // _TN
_REST_COLS = NSA_WIDTH + 3 * CONV_DIM + WIN_COLS
_N_MAIN = Q_COLS + KV_COLS + _REST_COLS
_J_WIN = (_N_MAIN - WIN_COLS) // _TN
_NJ = _N_MAIN // _TN
_REST_Z = 0
_REST_A = 1
_REST_GL = 2
_REST_ZC = 3
_REST_WIN = (NSA_WIDTH + 3 * CONV_DIM) // WIN_COLS


_C_GATE = Q_COLS + KV_COLS + WIN_COLS
_C_REST = _C_GATE + GATE_COLS
_J_SHIFT0 = _J_REST


def _wprep_kernel(a_ref, b_ref, o_ref):
    j = pl.program_id(0)
    shifted = (j >= _J_SHIFT0) & (j < _J_WIN)
    lane0 = _C_REST % _TN

    @pl.when(jnp.logical_not(shifted))
    def _():
        o_ref[...] = a_ref[...].astype(BF16)

    @pl.when(shifted)
    def _():
        both = jnp.concatenate([a_ref[...], b_ref[...]], axis=1)
        o_ref[...] = both[:, lane0:lane0 + _TN].astype(BF16)


def _wprep(w):
    d = w.shape[0]
    assert _C_REST % _TN < LANES and (_C_GATE - WIN_COLS) % _TN == 0
    src0 = _C_REST // _TN
    win_tile = (_C_GATE - WIN_COLS) // _TN

    def a_map(j):
        return (0, jnp.where(j < _J_SHIFT0, j, jnp.where(j < _J_WIN, j - _J_SHIFT0 + src0, win_tile)))

    def b_map(j):
        shifted = (j >= _J_SHIFT0) & (j < _J_WIN)
        return (0, jnp.where(shifted, (j - _J_SHIFT0 + src0 + 1) * (_TN // LANES), 0))

    return pl.pallas_call(
        _wprep_kernel,
        grid=(_NJ,),
        in_specs=[pl.BlockSpec((d, _TN), a_map), pl.BlockSpec((d, LANES), b_map)],
        out_specs=pl.BlockSpec((d, _TN), lambda j: (0, j)),
        out_shape=jax.ShapeDtypeStruct((d, _N_MAIN), BF16),
        compiler_params=_params("parallel"),
        name="wprep",
    )(w, w)


def _inproj_kernel(x_ref, g_ref, sc_ref, sh_ref, w_ref, wg_ref,
                   q_ref, kv_ref, kvb_ref, rest_ref, winb_ref, gate_ref, h_ref):
    j = pl.program_id(1)

    @pl.when(j == 0)
    def _():
        x = x_ref[...]
        y = x * lax.rsqrt(jnp.mean(x * x, axis=-1, keepdims=True) + NORM_EPS) * g_ref[...]
        h = (y * (1.0 + sc_ref[...]) + sh_ref[...]).astype(BF16)
        h_ref[...] = h
        gate_ref[...] = _dot(h, wg_ref[...])

    tile = lambda: _dot(h_ref[...], w_ref[...])

    @pl.when(j < _J_KV)
    def _():
        q_ref[...] = tile().astype(BF16)

    tm = h_ref.shape[0]
    streams_per_tile = _TN // HEAD_DIM
    for jj in range(_J_KV, _J_REST):
        @pl.when(j == jj)
        def _(jj=jj):
            acc = tile()
            kvb_ref[...] = acc.astype(BF16)
            for cc in range(streams_per_tile):
                c = (jj - _J_KV) * streams_per_tile + cc
                kv_ref[pl.ds(c, tm, stride=_KV_STREAMS), :] = acc[:, cc * HEAD_DIM:(cc + 1) * HEAD_DIM]

    @pl.when((j >= _J_REST) & (j != _J_WIN))
    def _():
        rest_ref[...] = tile()

    @pl.when(j == _J_WIN)
    def _():
        acc = tile()
        rest_ref[...] = acc
        winb_ref[...] = acc.astype(BF16)


def _inproj(x, g, scale3, shift3, w_main, w_gate, tm, rows_per_mod):
    m, d = x.shape
    mod_rows = scale3.shape[1]
    tiles_per_mod = rows_per_mod // tm

    def mod_map(i, j):
        return (i // tiles_per_mod, 0, 0)

    clampj = lambda j, lo, n: jnp.clip(j - lo, 0, n - 1)
    return pl.pallas_call(
        _inproj_kernel,
        grid=(m // tm, _NJ),
        in_specs=[pl.BlockSpec((tm, d), lambda i, j: (i, 0)),
                  pl.BlockSpec((1, d), lambda i, j: (0, 0)),
                  pl.BlockSpec((None, mod_rows, d), mod_map),
                  pl.BlockSpec((None, mod_rows, d), mod_map),
                  pl.BlockSpec((d, _TN), lambda i, j: (0, j)),
                  pl.BlockSpec((d, 2 * LANES), lambda i, j: (0, 0))],
        out_specs=[pl.BlockSpec((tm, _TN), lambda i, j: (i, clampj(j, 0, _J_KV))),
                   pl.BlockSpec((tm * _KV_STREAMS, HEAD_DIM), lambda i, j: (i, 0)),
                   pl.BlockSpec((tm, _TN), lambda i, j: (i, clampj(j, _J_KV, _J_REST - _J_KV))),
                   pl.BlockSpec((tm, _TN), lambda i, j: (i, clampj(j, _J_REST, _NJ - _J_REST))),
                   pl.BlockSpec((tm, _TN), lambda i, j: (i, 0)),
                   pl.BlockSpec((tm, 2 * LANES), lambda i, j: (i, 0))],
        out_shape=[jax.ShapeDtypeStruct((m, Q_COLS), BF16),
                   jax.ShapeDtypeStruct((m * _KV_STREAMS, HEAD_DIM), F32),
                   jax.ShapeDtypeStruct((m, KV_COLS), BF16),
                   jax.ShapeDtypeStruct((m, _REST_COLS), F32),
                   jax.ShapeDtypeStruct((m, WIN_COLS), BF16),
                   jax.ShapeDtypeStruct((m, 2 * LANES), F32)],
        scratch_shapes=[pltpu.VMEM((tm, d), BF16)],
        compiler_params=_params("parallel", "arbitrary"),
        name="inproj",
    )(x, g, scale3, shift3, w_main, w_gate)


def _compress_rows(load_rows, nch, w1, cconst, w2, b2):
    lhs = jnp.concatenate([load_rows(l) for l in range(CMP_STRIDE)], axis=1).astype(BF16)
    y = _dot(lhs, w1)
    second = pltpu.roll(y[:, HEAD_DIM:], shift=nch - 1, axis=0)
    h = _silu(y[:, :HEAD_DIM] + second + cconst)
    out = _dot(h.astype(BF16), w2) + b2
    row = lax.broadcasted_iota(jnp.int32, out.shape, 0)
    return jnp.where(row < nch - 1, out, 0.0)


def _compress_kernel(x_ref, w1_ref, cc_ref, w2_ref, b2_ref, o_ref, *, nch, ncp):
    for c in range(2 * N_KV_HEADS):
        s = c // N_KV_HEADS

        def load_rows(l, c=c):
            return x_ref[pl.ds(l * _KV_STREAMS + c, nch, stride=CMP_STRIDE * _KV_STREAMS), :]

        out = _compress_rows(load_rows, nch, w1_ref[s], cc_ref[s], w2_ref[s], b2_ref[s])
        if ncp > nch:
            out = jnp.concatenate([out, jnp.zeros((ncp - nch, HEAD_DIM), F32)], axis=0)
        o_ref[c] = out.astype(BF16)


def _compress(kv_streams, w1cat, cconst, w2b, b2, ncp):
    b, rows, _ = kv_streams.shape
    t = rows // _KV_STREAMS
    nch = t // CMP_STRIDE
    return pl.pallas_call(
        functools.partial(_compress_kernel, nch=nch, ncp=ncp),
        grid=(b,),
        in_specs=[pl.BlockSpec((None, rows, HEAD_DIM), lambda i: (i, 0, 0)),
                  pl.BlockSpec((2, 16 * HEAD_DIM, 2 * HEAD_DIM), lambda i: (0, 0, 0)),
                  pl.BlockSpec((2, 1, HEAD_DIM), lambda i: (0, 0, 0)),
                  pl.BlockSpec((2, HEAD_DIM, HEAD_DIM), lambda i: (0, 0, 0)),
                  pl.BlockSpec((2, 1, HEAD_DIM), lambda i: (0, 0, 0))],
        out_specs=pl.BlockSpec((None, 4, ncp, HEAD_DIM), lambda i: (i, 0, 0, 0)),
        out_shape=jax.ShapeDtypeStruct((b, 4, ncp, HEAD_DIM), BF16),
        compiler_params=_params("parallel"),
        name="compress",
    )(kv_streams, w1cat, cconst, w2b, b2)


def _select_mask(psum, ov, qpos, n_sel):
    hi = psum.astype(BF16)
    lo = (psum - hi.astype(F32)).astype(BF16)
    imp = _dot(hi, ov) + _dot(lo, ov)
    rows = imp.shape[0]
    jio = lax.broadcasted_iota(jnp.int32, (rows, LANES), 1)
    allowed = (jio * SEL_BLOCK <= qpos) & (jio < n_sel)
    cur = qpos // SEL_BLOCK
    forced = (jio == 0) | (jio == cur) | (jio == cur - 1)
    score = jnp.where(allowed, imp + jnp.where(forced, FORCE_BONUS, 0.0), NEG_INF)
    rank = jnp.zeros((rows, LANES), F32)
    for i in range(n_sel):
        si = jnp.broadcast_to(score[:, i:i + 1], (rows, LANES))
        tie = jnp.where(jio > i, 1.0, 0.0)
        rank = rank + jnp.where(si > score, 1.0, jnp.where(si == score, tie, 0.0))
    return (rank < float(min(N_SELECT, n_sel))) & allowed


def _select_mask_t(psum, ovt, qpos_row, n_sel):
    hi = psum.astype(BF16)
    lo = (psum - hi.astype(F32)).astype(BF16)
    imp = _dot_nt(ovt, hi) + _dot_nt(ovt, lo)
    sr, rows = imp.shape
    jio = lax.broadcasted_iota(jnp.int32, (sr, rows), 0)
    allowed = (jio * SEL_BLOCK <= qpos_row) & (jio < n_sel)
    cur = qpos_row // SEL_BLOCK
    forced = (jio == 0) | (jio == cur) | (jio == cur - 1)
    score = jnp.where(allowed, imp + jnp.where(forced, FORCE_BONUS, 0.0), NEG_INF)
    rank = jnp.zeros((sr, rows), F32)
    for i in range(n_sel):
        si = jnp.broadcast_to(score[i:i + 1, :], (sr, rows))
        tie = jnp.where(jio > i, 1.0, 0.0)
        rank = rank + jnp.where(si > score, 1.0, jnp.where(si == score, tie, 0.0))
    sel_t = jnp.where((rank < float(min(N_SELECT, n_sel))) & allowed, 1.0, 0.0)
    sel_t = jnp.concatenate([sel_t, jnp.zeros((LANES - sr, rows), F32)], axis=0)
    return sel_t.T


def _biased_softmax(s, bias):
    s = s + bias
    e = jnp.exp2(s - jnp.max(s, axis=-1, keepdims=True))
    return e / jnp.maximum(jnp.sum(e, axis=-1, keepdims=True), 1e-30)


def _attn_p_kernel(q_ref, kc_ref, vc_ref, ks_ref, vs_ref, kw_ref, vw_ref, gate_ref, z_ref, ovt_ref, e_ref,
                   o_ref, m_s, l_s, acc_s, pre_s, *, tq, tk, n_cmp, n_sel):
    qi = pl.program_id(2)
    st = LANES
    subs = range(tq // st)
    chains = [(u, r) for u in subs for r in range(HEADS_PER_GROUP)]
    rows = lambda u: slice(u * st, (u + 1) * st)
    cid = lambda u, r: u * HEADS_PER_GROUP + r
    q0s = [qi * tq + u * st for u in subs]
    qpos = [q0s[u] + lax.broadcasted_iota(jnp.int32, (st, 1), 0) for u in subs]
    q_of = lambda u, r: q_ref[rows(u), r * HEAD_DIM:(r + 1) * HEAD_DIM]
    gates = [_sigmoid(gate_ref[rows(u), :]) for u in subs]
    g_of = lambda br, u, r: jnp.broadcast_to(
        gates[u][:, br * HEADS_PER_GROUP + r:br * HEADS_PER_GROUP + r + 1], (st, HEAD_DIM))

    kc = kc_ref[...]
    vc = vc_ref[...]
    w0 = [pl.multiple_of(jnp.maximum(q0s[u] - WINDOW, 0), LANES) for u in subs]
    k_w = [kw_ref[pl.ds(w0[u], WINDOW + st), :] for u in subs]
    v_w = [vw_ref[pl.ds(w0[u], WINDOW + st), :] for u in subs]
    wbias = []
    for u in subs:
        dist = qpos[u] - (w0[u] + lax.broadcasted_iota(jnp.int32, (st, WINDOW + st), 1))
        wbias.append(jnp.where((dist >= 0) & (dist <= WINDOW), 0.0, NEG_INF))
    nio = lax.broadcasted_iota(jnp.int32, (st, kc.shape[0]), 1)
    cmask = [(nio < n_cmp) & (nio * CMP_STRIDE + (CMP_BLOCK - 1) <= qpos[u]) for u in subs]

    s_cmp = {c: _dot_nt(q_of(*c), kc) * SM_SCALE_LOG2 for c in chains}
    s_win = {c: _dot_nt(q_of(*c), k_w[c[0]]) * SM_SCALE_LOG2 for c in chains}
    p_cmp = {c: _masked_softmax(s_cmp[c], cmask[c[0]]) for c in chains}
    psum = [(p_cmp[(u, 0)] + p_cmp[(u, 1)]) + (p_cmp[(u, 2)] + p_cmp[(u, 3)]) for u in subs]
    p_win = {c: _biased_softmax(s_win[c], wbias[c[0]]) for c in chains}
    selb = []
    for u in subs:
        qpos_row = q0s[u] + lax.broadcasted_iota(jnp.int32, (1, st), 1)
        selb.append(_select_mask_t(psum[u], ovt_ref[...], qpos_row, n_sel).astype(BF16))
    for (u, r) in chains:
        pre_s[cid(u, r)] = (g_of(0, u, r) * _dot(p_cmp[(u, r)].astype(BF16), vc)
                            + g_of(2, u, r) * _dot(p_win[(u, r)].astype(BF16), v_w[u]))

    m_s[...] = jnp.full(m_s.shape, NEG_INF, F32)
    l_s[...] = jnp.zeros(l_s.shape, F32)
    acc_s[...] = jnp.zeros(acc_s.shape, F32)
    n_kt = (qi * tq + tq - 1) // tk + 1

    def sweep(kt, carry):
        k0 = pl.multiple_of(kt * tk, tk)
        k_t = ks_ref[pl.ds(k0, tk), :]
        v_t = vs_ref[pl.ds(k0, tk), :]
        kpos = k0 + lax.broadcasted_iota(jnp.int32, (st, tk), 1)
        bias = [jnp.where((_dot(selb[u], e_ref[kt]) > 0.5) & (kpos <= qpos[u]), 0.0, NEG_INF) for u in subs]
        s = {c: _dot_nt(q_of(*c), k_t) * SM_SCALE_LOG2 + bias[c[0]] for c in chains}
        m_prev = {c: m_s[cid(*c)] for c in chains}
        m_new = {c: jnp.maximum(m_prev[c], jnp.max(s[c], axis=-1, keepdims=True)) for c in chains}
        p = {c: jnp.exp2(s[c] - jnp.concatenate([m_new[c]] * (tk // LANES), axis=1)) for c in chains}
        alpha = {c: jnp.exp2(m_prev[c] - m_new[c]) for c in chains}
        for c in chains:
            k = cid(*c)
            l_s[k] = alpha[c] * l_s[k] + jnp.sum(p[c], axis=-1, keepdims=True)
            acc_s[k] = alpha[c] * acc_s[k] + _dot(p[c].astype(BF16), v_t)
            m_s[k] = m_new[c]
        return carry

    lax.fori_loop(0, n_kt, sweep, 0)

    for (u, r) in chains:
        k = cid(u, r)
        o_slc = acc_s[k] / jnp.maximum(l_s[k], 1e-30)
        o = pre_s[k] + g_of(1, u, r) * o_slc
        o_ref[rows(u), r * HEAD_DIM:(r + 1) * HEAD_DIM] = (
            o * _silu(z_ref[rows(u), r * HEAD_DIM:(r + 1) * HEAD_DIM])).astype(BF16)


def _attn_p(q_bf, kcv, kv_bf3, win_bf3, gate, rest, ovt, e_p, b, t, n_cmp, n_sel, tq=256, tk=256):
    assert t >= WINDOW + tq and t % tk == 0 and tq % LANES == 0 and t % tq == 0
    nq = t // tq
    n_chain = HEADS_PER_GROUP * (tq // LANES)
    ncp = kcv.shape[2]
    sr = ovt.shape[0]
    gw = HEADS_PER_GROUP * HEAD_DIM
    row = lambda bi, g, qi: bi * nq + qi
    return pl.pallas_call(
        functools.partial(_attn_p_kernel, tq=tq, tk=tk, n_cmp=n_cmp, n_sel=n_sel),
        grid=(b, N_KV_HEADS, nq),
        in_specs=[pl.BlockSpec((tq, gw), lambda bi, g, qi: (row(bi, g, qi), g)),
                  pl.BlockSpec((None, None, ncp, HEAD_DIM), lambda bi, g, qi: (bi, g, 0, 0)),
                  pl.BlockSpec((None, None, ncp, HEAD_DIM), lambda bi, g, qi: (bi, N_KV_HEADS + g, 0, 0)),
                  pl.BlockSpec((None, t, HEAD_DIM), lambda bi, g, qi: (bi, 0, 2 * N_KV_HEADS + g)),
                  pl.BlockSpec((None, t, HEAD_DIM), lambda bi, g, qi: (bi, 0, 3 * N_KV_HEADS + g)),
                  pl.BlockSpec((None, t, HEAD_DIM), lambda bi, g, qi: (bi, 0, g)),
                  pl.BlockSpec((None, t, HEAD_DIM), lambda bi, g, qi: (bi, 0, N_KV_HEADS + g)),
                  pl.BlockSpec((tq, LANES), lambda bi, g, qi: (row(bi, g, qi), g)),
                  pl.BlockSpec((tq, gw), lambda bi, g, qi: (row(bi, g, qi), g)),
                  pl.BlockSpec((sr, ncp), lambda bi, g, qi: (0, 0)),
                  pl.BlockSpec((t // tk, LANES, tk), lambda bi, g, qi: (0, 0, 0))],
        out_specs=pl.BlockSpec((tq, gw), lambda bi, g, qi: (row(bi, g, qi), g)),
        out_shape=jax.ShapeDtypeStruct((b * t, NSA_WIDTH), BF16),
        scratch_shapes=[pltpu.VMEM((n_chain, LANES, LANES), F32),
                        pltpu.VMEM((n_chain, LANES, LANES), F32),
                        pltpu.VMEM((n_chain, LANES, HEAD_DIM), F32),
                        pltpu.VMEM((n_chain, LANES, HEAD_DIM), F32)],
        compiler_params=_params("parallel", "parallel", "arbitrary"),
        name="attn_p",
    )(q_bf, kcv, kcv, kv_bf3, kv_bf3, win_bf3, win_bf3, gate, rest, ovt, e_p)


def _attn_s_kernel(pt_ref, cache_ref, wst_ref, q_ref, kvn_ref, winn_ref, gate_ref, z_ref, winc_ref,
                   w1_ref, cc_ref, w2_ref, b2_ref, ov_ref, e_ref, perm_ref, o_ref, wout_ref,
                   kvbuf, wbuf, wnew, kcs, ksl, kwn, sem, wsem, osem,
                   *, n_pages, page, n_cmp, n_sel, pos0, t_new):
    i = pl.program_id(0)
    nb = pl.num_programs(0) - 1
    past = n_pages * page
    nch = past // CMP_STRIDE
    rows = HEADS_PER_GROUP * SAMPLE_ROWS
    n_ws = N_WIN_SLOTS * N_KV_HEADS
    state_rows = WINDOW * n_ws
    new_rows = t_new * n_ws
    prow = page * _KV_STREAMS
    c_seq = jnp.minimum(i, nb - 1)
    a_seq = jnp.maximum(i - 1, 0)
    cs = c_seq % 2
    asl = a_seq % 2

    def page_copy(seq, p, slot):
        return pltpu.make_async_copy(cache_ref.at[pt_ref[seq, p]], kvbuf.at[slot, pl.ds(p * prow, prow)],
                                     sem.at[slot])

    def win_copy(seq, slot):
        return pltpu.make_async_copy(wst_ref.at[seq], wbuf.at[slot], wsem.at[slot])

    def fetch(seq, slot):
        for p in range(n_pages):
            page_copy(seq, p, slot).start()
        win_copy(seq, slot).start()

    def state_copies(seq, slot):
        kept = pltpu.make_async_copy(wbuf.at[slot, pl.ds(new_rows, state_rows - new_rows)],
                                     wout_ref.at[seq, pl.ds(0, state_rows - new_rows)], osem.at[slot])
        fresh = pltpu.make_async_copy(wnew.at[slot], wout_ref.at[seq, pl.ds(state_rows - new_rows, new_rows)],
                                      osem.at[slot])
        return kept, fresh

    @pl.when(i == 0)
    def _():
        fetch(0, 0)
        kcs[0] = jnp.zeros(kcs.shape[1:], BF16)
        ksl[0] = jnp.zeros(ksl.shape[1:], BF16)
        kwn[0] = jnp.zeros(kwn.shape[1:], BF16)

    @pl.when(i >= 1)
    def _():
        for cp in state_copies(i - 1, (i - 1) % 2):
            cp.wait()

    @pl.when(i + 1 < nb)
    def _():
        fetch(i + 1, (i + 1) % 2)

    @pl.when(i < nb)
    def _():
        for p in range(n_pages):
            page_copy(i, p, i % 2).wait()
        win_copy(i, i % 2).wait()
        for t in range(t_new):
            for c in range(n_ws):
                wnew[i % 2, pl.ds(t * n_ws + c, 1), :] = winc_ref[pl.ds(t, 1), c * HEAD_DIM:(c + 1) * HEAD_DIM]
        for cp in state_copies(i, i % 2):
            cp.start()

    qpos = pos0 + lax.broadcasted_iota(jnp.int32, (rows, 1), 0) % SAMPLE_ROWS
    kvn = kvn_ref[...].astype(F32)
    qf = q_ref[...].astype(F32)
    gates = _sigmoid(gate_ref[...])
    zero_tail = jnp.zeros((LANES - SUBLANES, HEAD_DIM), F32)

    def tail_block(new_f32):
        return jnp.concatenate([new_f32, zero_tail], axis=0).astype(BF16)

    def scores(q, parked, tail):
        return jnp.concatenate([_dot_nt(q, parked), _dot_nt(q, tail)], axis=1) * SM_SCALE_LOG2

    def weighted(p, parked, tail):
        n_old = parked.shape[0]
        return _dot(p[:, :n_old].astype(BF16), parked) + _dot(p[:, n_old:].astype(BF16), tail)

    def stack_heads(x8):
        return jnp.concatenate([x8] * HEADS_PER_GROUP, axis=0)

    groups = range(N_KV_HEADS)
    nk = past + LANES
    kpos = lax.broadcasted_iota(jnp.int32, (rows, nk), 1)
    wpos = (pos0 - WINDOW) + lax.broadcasted_iota(jnp.int32, (rows, WINDOW + LANES), 1)
    dist = qpos - wpos
    wmask = (dist >= 0) & (dist <= WINDOW) & (wpos >= 0)
    nio = lax.broadcasted_iota(jnp.int32, (rows, nch), 1)
    cmask = (nio < n_cmp) & (nio * CMP_STRIDE + (CMP_BLOCK - 1) <= qpos)

    stream = lambda c: kvbuf[cs, pl.ds(c, past, stride=_KV_STREAMS), :].astype(BF16)
    xs0 = jnp.concatenate([stream(g) for g in groups], axis=1)

    qgs, s_cmp, s_slc, s_win, v_tails, vw_tails = [], [], [], [], [], []
    for g in groups:
        heads = [qf[:, (g * HEADS_PER_GROUP + r) * HEAD_DIM:(g * HEADS_PER_GROUP + r + 1) * HEAD_DIM]
                 for r in range(HEADS_PER_GROUP)]
        qgs.append(jnp.concatenate(heads, axis=0).astype(BF16))
    for g in groups:
        s_cmp.append(_dot_nt(qgs[g], kcs[asl, g]) * SM_SCALE_LOG2)
    for g in groups:
        c_k = (2 * N_KV_HEADS + g) * HEAD_DIM
        c_v = (3 * N_KV_HEADS + g) * HEAD_DIM
        v_tails.append(tail_block(kvn[:, c_v:c_v + HEAD_DIM]))
        s_slc.append(scores(qgs[g], ksl[asl, g], tail_block(kvn[:, c_k:c_k + HEAD_DIM])))
    for g in groups:
        wk = g * HEAD_DIM
        wv = (N_KV_HEADS + g) * HEAD_DIM
        vw_tails.append(tail_block(winn_ref[:, wv:wv + HEAD_DIM]))
        s_win.append(scores(qgs[g], kwn[asl, g], tail_block(winn_ref[:, wk:wk + HEAD_DIM])))

    vcs = [kcs[asl, N_KV_HEADS + g] for g in groups]

    perm = perm_ref[...]
    blk = LANES
    cpb = blk // CMP_STRIDE

    def slot_streams(s):
        return jnp.concatenate([stream(s * N_KV_HEADS + g) for g in groups], axis=1)

    def slot_chunks(xs):
        out = [[] for _ in groups]
        for k in range(past // blk):
            y = _dot(perm, xs[k * blk:(k + 1) * blk])
            for g in groups:
                out[g].append(jnp.concatenate(
                    [y[l * cpb:(l + 1) * cpb, g * HEAD_DIM:(g + 1) * HEAD_DIM] for l in range(CMP_STRIDE)], axis=1))
        return jnp.concatenate([jnp.concatenate(o, axis=0) for o in out], axis=0).astype(BF16)

    def compress_slot(s, lhs):
        y = _dot(lhs, w1_ref[s])
        second = pltpu.roll(y[:, HEAD_DIM:], shift=lhs.shape[0] - 1, axis=0)
        h = _silu(y[:, :HEAD_DIM] + second + cc_ref[s])
        out = _dot(h.astype(BF16), w2_ref[s]) + b2_ref[s]
        row = lax.broadcasted_iota(jnp.int32, out.shape, 0) % nch
        out = jnp.where(row < nch - 1, out, 0.0).astype(BF16)
        for g in groups:
            kcs[cs, s * N_KV_HEADS + g] = out[g * nch:(g + 1) * nch]

    xs1 = slot_streams(1)
    lhs0 = slot_chunks(xs0)

    p_cmps = [_masked_softmax(s_cmp[g], cmask) for g in groups]
    psums = [(p[0:8] + p[8:16]) + (p[16:24] + p[24:32]) for p in p_cmps]
    sels = [_select_mask(psums[g], ov_ref[...], qpos[0:SAMPLE_ROWS], n_sel) for g in groups]
    selbs = [jnp.where(s, 1.0, 0.0).astype(BF16) for s in sels]

    for k in range(N_KV_HEADS):
        ksl[cs, k] = stream(2 * N_KV_HEADS + k)
    compress_slot(0, lhs0)
    lhs1 = slot_chunks(xs1)

    smasks = [(stack_heads(_dot(selbs[g], e_ref[...])) > 0.5) & (kpos <= qpos) for g in groups]
    o_cmps = [_dot(p_cmps[g].astype(BF16), vcs[g]) for g in groups]
    p_wins = [_masked_softmax(s_win[g], wmask) for g in groups]
    o_wins = [weighted(p_wins[g], kwn[asl, N_KV_HEADS + g], vw_tails[g]) for g in groups]
    p_slcs = [_masked_softmax(s_slc[g], smasks[g]) for g in groups]
    o_slcs = [weighted(p_slcs[g], ksl[asl, N_KV_HEADS + g], v_tails[g]) for g in groups]

    for k in range(N_KV_HEADS, 2 * N_KV_HEADS):
        ksl[cs, k] = stream(2 * N_KV_HEADS + k)
    compress_slot(1, lhs1)
    for k in range(n_ws):
        kwn[cs, k] = wbuf[cs, pl.ds(k, WINDOW, stride=n_ws), :].astype(BF16)

    for g in groups:
        o_cmp, o_slc, o_win = o_cmps[g], o_slcs[g], o_wins[g]
        for r in range(HEADS_PER_GROUP):
            h = g * HEADS_PER_GROUP + r
            rs = slice(r * SAMPLE_ROWS, (r + 1) * SAMPLE_ROWS)
            g_of = lambda br: jnp.broadcast_to(
                gates[:, g * LANES + br * HEADS_PER_GROUP + r:g * LANES + br * HEADS_PER_GROUP + r + 1],
                (SAMPLE_ROWS, HEAD_DIM))
            o = g_of(0) * o_cmp[rs] + g_of(1) * o_slc[rs] + g_of(2) * o_win[rs]
            o_ref[:, h * HEAD_DIM:(h + 1) * HEAD_DIM] = (
                o * _silu(z_ref[:, h * HEAD_DIM:(h + 1) * HEAD_DIM])).astype(BF16)


def _attn_s(page_table, cache3, wst3, q3, kvn3, rest3, gate3, w1cat, cconst, w2b, b2, ov, e_s,
            n_cmp, n_sel, pos0, t_new):
    nb, n_pages = page_table.shape
    page = cache3.shape[1] // _KV_STREAMS
    past = n_pages * page
    nch = past // CMP_STRIDE
    n_ws = N_WIN_SLOTS * N_KV_HEADS
    assert wst3.shape[1:] == (WINDOW * n_ws, HEAD_DIM) and past % LANES == 0
    ncp = ov.shape[0]
    prev = lambda i: jnp.maximum(i - 1, 0)
    grid_spec = pltpu.PrefetchScalarGridSpec(
        num_scalar_prefetch=1,
        grid=(nb + 1,),
        in_specs=[pl.BlockSpec(memory_space=pl.ANY),
                  pl.BlockSpec(memory_space=pl.ANY),
                  pl.BlockSpec((None, SAMPLE_ROWS, Q_COLS), lambda i, pt: (prev(i), 0, 0)),
                  pl.BlockSpec((None, SAMPLE_ROWS, KV_COLS), lambda i, pt: (prev(i), 0, 0)),
                  pl.BlockSpec((None, SAMPLE_ROWS, WIN_COLS), lambda i, pt: (prev(i), 0, _REST_WIN)),
                  pl.BlockSpec((None, SAMPLE_ROWS, 2 * LANES), lambda i, pt: (prev(i), 0, 0)),
                  pl.BlockSpec((None, SAMPLE_ROWS, NSA_WIDTH), lambda i, pt: (prev(i), 0, _REST_Z)),
                  pl.BlockSpec((None, SAMPLE_ROWS, WIN_COLS), lambda i, pt: (jnp.minimum(i, nb - 1), 0, _REST_WIN)),
                  pl.BlockSpec((2, 16 * HEAD_DIM, 2 * HEAD_DIM), lambda i, pt: (0, 0, 0)),
                  pl.BlockSpec((2, 1, HEAD_DIM), lambda i, pt: (0, 0, 0)),
                  pl.BlockSpec((2, HEAD_DIM, HEAD_DIM), lambda i, pt: (0, 0, 0)),
                  pl.BlockSpec((2, 1, HEAD_DIM), lambda i, pt: (0, 0, 0)),
                  pl.BlockSpec((ncp, LANES), lambda i, pt: (0, 0)),
                  pl.BlockSpec((LANES, past + LANES), lambda i, pt: (0, 0)),
                  pl.BlockSpec((LANES, LANES), lambda i, pt: (0, 0))],
        out_specs=[pl.BlockSpec((None, SAMPLE_ROWS, NSA_WIDTH), lambda i, pt: (prev(i), 0, 0)),
                   pl.BlockSpec(memory_space=pl.ANY)],
        scratch_shapes=[pltpu.VMEM((2, past * _KV_STREAMS, HEAD_DIM), F32),
                        pltpu.VMEM((2, WINDOW * n_ws, HEAD_DIM), F32),
                        pltpu.VMEM((2, t_new * n_ws, HEAD_DIM), F32),
                        pltpu.VMEM((2, 2 * N_KV_HEADS, nch, HEAD_DIM), BF16),
                        pltpu.VMEM((2, 2 * N_KV_HEADS, past, HEAD_DIM), BF16),
                        pltpu.VMEM((2, n_ws, WINDOW, HEAD_DIM), BF16),
                        pltpu.SemaphoreType.DMA((2,)),
                        pltpu.SemaphoreType.DMA((2,)),
                        pltpu.SemaphoreType.DMA((2,))],
    )
    return pl.pallas_call(
        functools.partial(_attn_s_kernel, n_pages=n_pages, page=page, n_cmp=n_cmp, n_sel=n_sel, pos0=pos0,
                          t_new=t_new),
        grid_spec=grid_spec,
        out_shape=[jax.ShapeDtypeStruct((nb, SAMPLE_ROWS, NSA_WIDTH), BF16),
                   jax.ShapeDtypeStruct((nb, WINDOW * n_ws, HEAD_DIM), F32)],
        compiler_params=_params("arbitrary"),
        name="attn_s",
    )(page_table, cache3, wst3, q3, kvn3, rest3, gate3, rest3, rest3, w1cat, cconst, w2b, b2, ov, e_s,
      _chunk_permutation())


_CONV_RC = 64
_CONV_LC = 256


def _depthwise_conv(uext, w_ref, db_ref, ybuf, tt, row0=0):
    off = CONV_HALO - (CONV_WIDTH - 1)
    uext[CONV_HALO + tt:CONV_HALO + tt + SUBLANES] = jnp.zeros((SUBLANES, CONV_DIM), F32)
    rc = min(_CONV_RC, tt)
    for r0 in range(0, tt, rc):
        for c0 in range(0, CONV_DIM, _CONV_LC):
            lanes = slice(c0, c0 + _CONV_LC)
            acc = jnp.broadcast_to(db_ref[:, lanes], (rc, _CONV_LC))
            for r in range(SUBLANES):
                z = None
                for a in range((CONV_WIDTH + off) // SUBLANES + 1):
                    k = SUBLANES * a + r - off
                    if 0 <= k < CONV_WIDTH:
                        term = w_ref[k:k + 1, lanes] * uext[r0 + SUBLANES * a:r0 + SUBLANES * a + rc + SUBLANES, lanes]
                        z = term if z is None else z + term
                acc = acc + z[r:r + rc]
            ybuf[row0 + r0:row0 + r0 + rc, lanes] = acc


def _conv_kernel(a_ref, gl_ref, ah_ref, glh_ref, init_ref, zc_ref, w_ref, db_ref, lg_ref, lb_ref,
                 cv_ref, st_ref, uext, ybuf, *, tt, t_valid):
    ti = pl.program_id(1)
    nt = pl.num_programs(1)
    halo = ah_ref[...] * _sigmoid(glh_ref[...])
    uext[0:CONV_HALO] = jnp.where(ti == 0, init_ref[...], halo)
    uext[CONV_HALO:CONV_HALO + tt] = a_ref[...] * _sigmoid(gl_ref[...])
    _depthwise_conv(uext, w_ref, db_ref, ybuf, tt)

    y = ybuf[...]
    mu = jnp.mean(y, axis=-1, keepdims=True)
    var = jnp.mean(jnp.square(y - mu), axis=-1, keepdims=True)
    ln = (y - mu) * lax.rsqrt(var + NORM_EPS) * lg_ref[...] + lb_ref[...]
    cv_ref[...] = (_silu(ln) * _silu(zc_ref[...])).astype(BF16)

    @pl.when(ti == nt - 1)
    def _():
        st_ref[...] = uext[pl.ds(t_valid, CONV_HALO), :]


def _conv(rest, u_init, w_pad, db, lg, lb, b, t, tt, t_valid):
    assert tt % CONV_HALO == 0 and t % tt == 0
    nt = t // tt
    hb = tt // CONV_HALO
    row = lambda bi, ti: bi * nt + ti
    halo_map = lambda col: (lambda bi, ti: (jnp.maximum(row(bi, ti) * hb - 1, 0), col))
    halo_rows = CONV_HALO
    vec = lambda: pl.BlockSpec((1, CONV_DIM), lambda bi, ti: (0, 0))
    return pl.pallas_call(
        functools.partial(_conv_kernel, tt=tt, t_valid=t_valid),
        grid=(b, nt),
        in_specs=[pl.BlockSpec((tt, CONV_DIM), lambda bi, ti: (row(bi, ti), _REST_A)),
                  pl.BlockSpec((tt, CONV_DIM), lambda bi, ti: (row(bi, ti), _REST_GL)),
                  pl.BlockSpec((halo_rows, CONV_DIM), halo_map(_REST_A)),
                  pl.BlockSpec((halo_rows, CONV_DIM), halo_map(_REST_GL)),
                  pl.BlockSpec((None, CONV_HALO, CONV_DIM), lambda bi, ti: (bi, 0, 0)),
                  pl.BlockSpec((tt, CONV_DIM), lambda bi, ti: (row(bi, ti), _REST_ZC)),
                  pl.BlockSpec((CONV_HALO, CONV_DIM), lambda bi, ti: (0, 0)),
                  vec(), vec(), vec()],
        out_specs=[pl.BlockSpec((tt, CONV_DIM), lambda bi, ti: (row(bi, ti), 0)),
                   pl.BlockSpec((None, CONV_HALO, CONV_DIM), lambda bi, ti: (bi, 0, 0))],
        out_shape=[jax.ShapeDtypeStruct((b * t, CONV_DIM), BF16),
                   jax.ShapeDtypeStruct((b, CONV_HALO, CONV_DIM), F32)],
        scratch_shapes=[pltpu.VMEM((CONV_HALO + tt + SUBLANES, CONV_DIM), F32),
                        pltpu.VMEM((tt, CONV_DIM), F32)],
        compiler_params=_params("parallel", "arbitrary"),
        name="conv",
    )(rest, rest, rest, rest, u_init, rest, w_pad, db, lg, lb)


def _conv_short_kernel(a_ref, gl_ref, init_ref, zc_ref, w_ref, db_ref, lg_ref, lb_ref,
                       cv_ref, st_ref, uext, ybuf, *, tt, t_valid, ns):
    u = a_ref[...] * _sigmoid(gl_ref[...])
    for s in range(ns):
        ue = uext.at[s]
        ue[0:CONV_HALO] = init_ref[s]
        ue[CONV_HALO:CONV_HALO + tt] = u[s * tt:(s + 1) * tt]
        _depthwise_conv(ue, w_ref, db_ref, ybuf, tt, row0=s * tt)
        st_ref[s] = ue[pl.ds(t_valid, CONV_HALO), :]
    y = ybuf[...]
    mu = jnp.mean(y, axis=-1, keepdims=True)
    var = jnp.mean(jnp.square(y - mu), axis=-1, keepdims=True)
    ln = (y - mu) * lax.rsqrt(var + NORM_EPS) * lg_ref[...] + lb_ref[...]
    cv_ref[...] = (_silu(ln) * _silu(zc_ref[...])).astype(BF16)


def _conv_short(rest, u_init, w_pad, db, lg, lb, b, tt, t_valid, ns):
    assert b % ns == 0 and tt % SUBLANES == 0
    rows = ns * tt
    vec = lambda: pl.BlockSpec((1, CONV_DIM), lambda i: (0, 0))
    return pl.pallas_call(
        functools.partial(_conv_short_kernel, tt=tt, t_valid=t_valid, ns=ns),
        grid=(b // ns,),
        in_specs=[pl.BlockSpec((rows, CONV_DIM), lambda i: (i, _REST_A)),
                  pl.BlockSpec((rows, CONV_DIM), lambda i: (i, _REST_GL)),
                  pl.BlockSpec((ns, CONV_HALO, CONV_DIM), lambda i: (i, 0, 0)),
                  pl.BlockSpec((rows, CONV_DIM), lambda i: (i, _REST_ZC)),
                  pl.BlockSpec((CONV_HALO, CONV_DIM), lambda i: (0, 0)),
                  vec(), vec(), vec()],
        out_specs=[pl.BlockSpec((rows, CONV_DIM), lambda i: (i, 0)),
                   pl.BlockSpec((ns, CONV_HALO, CONV_DIM), lambda i: (i, 0, 0))],
        out_shape=[jax.ShapeDtypeStruct((b * tt, CONV_DIM), BF16),
                   jax.ShapeDtypeStruct((b, CONV_HALO, CONV_DIM), F32)],
        scratch_shapes=[pltpu.VMEM((ns, CONV_HALO + tt + SUBLANES, CONV_DIM), F32),
                        pltpu.VMEM((rows, CONV_DIM), F32)],
        compiler_params=_params("parallel"),
        name="conv_short",
    )(rest, rest, u_init, rest, w_pad, db, lg, lb)


def _outproj_kernel(on_ref, cv_ref, w_ref, x_ref, gate_ref, g_ref, y_ref):
    mix = _dot(on_ref[...], w_ref[0:NSA_WIDTH, :]) + _dot(cv_ref[...], w_ref[NSA_WIDTH:, :])
    nrm = mix * lax.rsqrt(jnp.mean(mix * mix, axis=-1, keepdims=True) + NORM_EPS) * g_ref[...]
    y_ref[...] = x_ref[...] + gate_ref[...] * nrm


def _outproj(o_nsa, cv, w_out_b, x, gate3, g_post, tm, rows_per_mod):
    m, d = x.shape
    mod_rows = gate3.shape[1]
    tiles_per_mod = rows_per_mod // tm
    return pl.pallas_call(
        _outproj_kernel,
        grid=(m // tm,),
        in_specs=[pl.BlockSpec((tm, NSA_WIDTH), lambda i: (i, 0)),
                  pl.BlockSpec((tm, CONV_DIM), lambda i: (i, 0)),
                  pl.BlockSpec((d, d), lambda i: (0, 0)),
                  pl.BlockSpec((tm, d), lambda i: (i, 0)),
                  pl.BlockSpec((None, mod_rows, d), lambda i: (i // tiles_per_mod, 0, 0)),
                  pl.BlockSpec((1, d), lambda i: (0, 0))],
        out_specs=pl.BlockSpec((tm, d), lambda i: (i, 0)),
        out_shape=jax.ShapeDtypeStruct((m, d), F32),
        compiler_params=_params("parallel"),
        name="outproj",
    )(o_nsa, cv, w_out_b, x, gate3, g_post)


def _overlap_matrix(n_cmp, n_sel, ncp):
    ci = np.arange(ncp)[:, None] * CMP_STRIDE
    sj = np.arange(LANES)[None, :] * SEL_BLOCK
    ov = (ci < sj + SEL_BLOCK) & (ci + CMP_BLOCK > sj)
    ov &= (np.arange(ncp)[:, None] < n_cmp) & (np.arange(LANES)[None, :] < n_sel)
    return jnp.asarray(ov.astype(np.float32), dtype=BF16)


def _overlap_matrix_t(n_cmp, n_sel, ncp):
    sr = _round_up(n_sel, 2 * SUBLANES)
    return _overlap_matrix(n_cmp, n_sel, ncp).T[:sr]


def _chunk_permutation():
    p = np.zeros((LANES, LANES), np.float32)
    for j in range(LANES // CMP_STRIDE):
        for l in range(CMP_STRIDE):
            p[l * (LANES // CMP_STRIDE) + j, CMP_STRIDE * j + l] = 1.0
    return jnp.asarray(p, dtype=BF16)


def _expand_matrix(n_keys):
    e = (np.arange(n_keys)[None, :] // SEL_BLOCK) == np.arange(LANES)[:, None]
    return jnp.asarray(e.astype(np.float32), dtype=BF16)


def _round_up(x, m):
    return -(-x // m) * m


def kernel(x_prompt, x_sample, c_prompt, c_sample, cache_kv, page_table, state_win_kv, state_conv, w_ada, b_ada, norm_pre, norm_post, w_in, cmp_pe, cmp_w1, cmp_b1, cmp_w2, cmp_b2, conv_dw, conv_db, conv_ln_g, conv_ln_b, w_out):
    assert w_ada.shape[0] == 1, "single-layer trunk"
    bp, tp, d = x_prompt.shape
    bs, ts, _ = x_sample.shape
    n_phys, page = cache_kv.shape[1], cache_kv.shape[2]
    n_pages = page_table.shape[1]
    past = n_pages * page
    assert ts <= SAMPLE_ROWS and d == D_MODEL

    w = w_in[0]
    c_q, c_kv, c_win, c_gate = 0, Q_COLS, Q_COLS + KV_COLS, Q_COLS + KV_COLS + WIN_COLS
    c_rest = c_gate + GATE_COLS
    w_main = _wprep(w)
    wg = w[:, c_gate:c_rest].reshape(d, N_BRANCH, N_KV_HEADS, HEADS_PER_GROUP).transpose(0, 2, 1, 3)
    wg = wg.reshape(d, N_KV_HEADS, N_BRANCH * HEADS_PER_GROUP)
    w_gate = jnp.pad(wg, ((0, 0), (0, 0), (0, LANES - N_BRANCH * HEADS_PER_GROUP))).reshape(d, 2 * LANES).astype(BF16)
    w_out_b = w_out[0].astype(BF16)
    w1 = cmp_w1[0]
    w1cat = jnp.concatenate([w1[:, :CMP_STRIDE].reshape(2, CMP_STRIDE * HEAD_DIM, HEAD_DIM),
                             w1[:, CMP_STRIDE:].reshape(2, CMP_STRIDE * HEAD_DIM, HEAD_DIM)], axis=2).astype(BF16)
    w2b = cmp_w2[0].astype(BF16)
    pe_rows = jnp.pad(cmp_pe[0].reshape(2, 2, CMP_STRIDE * HEAD_DIM), ((0, 0), (0, SUBLANES - 2), (0, 0)))
    b1 = cmp_b1[0].reshape(2, 1, HEAD_DIM)
    b2 = cmp_b2[0].reshape(2, 1, HEAD_DIM)
    conv_w_pad = jnp.pad(conv_dw[0], ((0, CONV_HALO - CONV_WIDTH), (0, 0)))
    db = conv_db[0].reshape(1, CONV_DIM)
    lg = conv_ln_g[0].reshape(1, CONV_DIM)
    lb = conv_ln_b[0].reshape(1, CONV_DIM)
    g_pre = norm_pre[0].reshape(1, d)
    g_post = norm_post[0].reshape(1, d)

    n_c = bp + bs
    c_all = jnp.pad(jnp.concatenate([c_prompt, c_sample], axis=0), ((0, _round_up(n_c, SUBLANES) - n_c), (0, 0)))
    mod = _ada(c_all, w_ada[0], b_ada[0])
    shift, scale, gate = mod[:, :d], mod[:, d:2 * d], mod[:, 2 * d:]
    cconst = _cmpconst(pe_rows, w1cat, b1)

    tm_p = 1024
    xp = x_prompt.reshape(bp * tp, d)
    q_p, kv_p, kvb_p, rest_p, winb_p, gate_p = _inproj(
        xp, g_pre, scale[:bp].reshape(bp, 1, d), shift[:bp].reshape(bp, 1, d), w_main, w_gate, tm_p, tp)
    n_cmp_p = (tp - CMP_BLOCK) // CMP_STRIDE + 1
    n_sel_p = -(-tp // SEL_BLOCK)
    ncp_p = _round_up(tp // CMP_STRIDE, LANES)
    kcv_p = _compress(kv_p.reshape(bp, tp * _KV_STREAMS, HEAD_DIM), w1cat, cconst, w2b, b2, ncp_p)
    tk = 512
    e_p = _expand_matrix(tp).reshape(LANES, tp // tk, tk).transpose(1, 0, 2)
    o_nsa_p = _attn_p(q_p, kcv_p, kvb_p.reshape(bp, tp, KV_COLS), winb_p.reshape(bp, tp, WIN_COLS), gate_p, rest_p,
                      _overlap_matrix_t(n_cmp_p, n_sel_p, ncp_p), e_p, bp, tp, n_cmp_p, n_sel_p, tk=tk)
    cv_p, st_p = _conv(rest_p, jnp.zeros((bp, CONV_HALO, CONV_DIM), F32), conv_w_pad, db, lg, lb, bp, tp, 256, 256)
    y_p = _outproj(o_nsa_p, cv_p, w_out_b, xp, gate[:bp].reshape(bp, 1, d), g_post, 512, tp)

    r = SAMPLE_ROWS
    xs = jnp.pad(x_sample, ((0, 0), (0, r - ts), (0, 0))).reshape(bs * r, d)
    per_row = lambda v: jnp.repeat(v[bp:bp + bs], r, axis=0)
    tm_s = min(512, bs * r)
    n_mod = (bs * r) // tm_s
    q_s, kv_s, kvb_s, rest_s, _, gate_s = _inproj(
        xs, g_pre, per_row(scale).reshape(n_mod, tm_s, d), per_row(shift).reshape(n_mod, tm_s, d),
        w_main, w_gate, tm_s, tm_s)
    total = past + ts
    n_cmp_s = (total - CMP_BLOCK) // CMP_STRIDE + 1
    n_sel_s = -(-total // SEL_BLOCK)
    assert n_cmp_s <= past // CMP_STRIDE - 1 + 1 and (n_cmp_s - 1) * CMP_STRIDE + CMP_BLOCK <= past
    ncp_s = _round_up(past // CMP_STRIDE, LANES)
    assert ncp_s == past // CMP_STRIDE
    n_ws = N_WIN_SLOTS * N_KV_HEADS
    o_nsa_s, win_state_s = _attn_s(
        page_table, cache_kv.reshape(n_phys, page * _KV_STREAMS, HEAD_DIM),
        state_win_kv.reshape(bs, WINDOW * n_ws, HEAD_DIM),
        q_s.reshape(bs, r, Q_COLS), kvb_s.reshape(bs, r, KV_COLS), rest_s.reshape(bs, r, _REST_COLS),
        gate_s.reshape(bs, r, 2 * LANES), w1cat, cconst, w2b, b2,
        _overlap_matrix(n_cmp_s, n_sel_s, ncp_s), _expand_matrix(past + LANES),
        n_cmp_s, n_sel_s, past, ts)
    u_init_s = jnp.pad(state_conv[0], ((0, 0), (CONV_HALO - (CONV_WIDTH - 1), 0), (0, 0)))
    cv_s, st_s = _conv_short(rest_s, u_init_s, conv_w_pad, db, lg, lb, bs, r, ts, math.gcd(bs, SUBLANES))
    y_s = _outproj(o_nsa_s.reshape(bs * r, NSA_WIDTH), cv_s, w_out_b, xs, per_row(gate).reshape(n_mod, tm_s, d),
                   g_post, tm_s, tm_s)

    keep = CONV_WIDTH - 1
    y_prompt = y_p.reshape(bp, tp, d)
    y_sample = y_s.reshape(bs, r, d)[:, :ts]
    kv_rows_prompt = kv_p.reshape(1, bp, tp, N_CACHE_SLOTS, N_KV_HEADS, HEAD_DIM)
    kv_rows_sample = kv_s.reshape(1, bs, r, N_CACHE_SLOTS, N_KV_HEADS, HEAD_DIM)[:, :, :ts]
    win_p = rest_p.reshape(bp, tp, _REST_COLS)[:, tp - min(WINDOW, tp):, _REST_WIN * WIN_COLS:]
    win_prompt = win_p.reshape(1, bp, min(WINDOW, tp), N_WIN_SLOTS, N_KV_HEADS, HEAD_DIM)
    win_sample = win_state_s.reshape(1, bs, WINDOW, N_WIN_SLOTS, N_KV_HEADS, HEAD_DIM)
    conv_prompt = st_p[:, CONV_HALO - keep:].reshape(1, bp, keep, CONV_DIM)
    conv_sample = st_s[:, CONV_HALO - keep:].reshape(1, bs, keep, CONV_DIM)
    return (y_prompt, y_sample, kv_rows_prompt, kv_rows_sample, win_prompt, win_sample, conv_prompt, conv_sample)
```

```python
import functools
import math

import numpy as np
import jax
import jax.numpy as jnp
from jax import lax
from jax.experimental import pallas as pl
from jax.experimental.pallas import tpu as pltpu

D_MODEL = 2048
HEAD_DIM = 128
N_HEADS = 8
N_KV_HEADS = 2
HEADS_PER_GROUP = N_HEADS // N_KV_HEADS
NSA_WIDTH = N_HEADS * HEAD_DIM
CONV_DIM = D_MODEL - NSA_WIDTH
CMP_BLOCK = 32
CMP_STRIDE = 16
SEL_BLOCK = 64
N_SELECT = 16
WINDOW = 512
CONV_WIDTH = 31
N_CACHE_SLOTS = 4
N_WIN_SLOTS = 2
N_BRANCH = 3
Q_COLS = N_HEADS * HEAD_DIM
KV_COLS = N_CACHE_SLOTS * N_KV_HEADS * HEAD_DIM
WIN_COLS = N_WIN_SLOTS * N_KV_HEADS * HEAD_DIM
GATE_COLS = N_BRANCH * N_HEADS
_KV_STREAMS = N_CACHE_SLOTS * N_KV_HEADS
NORM_EPS = 1e-6
NEG_INF = -1e30
FORCE_BONUS = 1e6
SM_SCALE_LOG2 = HEAD_DIM ** -0.5 * math.log2(math.e)

LANES = 128
SUBLANES = 8
CONV_HALO = 32
SAMPLE_ROWS = 8
VMEM_LIMIT = 56 * 1024 * 1024

F32 = jnp.float32
BF16 = jnp.bfloat16


def _sigmoid(x):
    return 1.0 / (1.0 + jnp.exp(-x))


def _silu(x):
    return x * _sigmoid(x)


def _dot(a, b):
    return jnp.dot(a, b, preferred_element_type=F32)


def _dot_nt(a, b):
    return lax.dot_general(a, b, (((1,), (1,)), ((), ())), preferred_element_type=F32)


def _masked_softmax(s, mask):
    s = jnp.where(mask, s, NEG_INF)
    e = jnp.where(mask, jnp.exp2(s - jnp.max(s, axis=-1, keepdims=True)), 0.0)
    return e / jnp.maximum(jnp.sum(e, axis=-1, keepdims=True), 1e-30)


def _params(*sem):
    return pltpu.CompilerParams(dimension_semantics=sem, vmem_limit_bytes=VMEM_LIMIT)


def _ada_kernel(c_ref, w_ref, b_ref, o_ref):
    o_ref[...] = _dot(_silu(c_ref[...]).astype(BF16), w_ref[...].astype(BF16)) + b_ref[...]


def _ada(c_all, w_ada, b_ada, tn=512):
    rows, d = c_all.shape
    n = w_ada.shape[1]
    return pl.pallas_call(
        _ada_kernel,
        grid=(n // tn,),
        in_specs=[pl.BlockSpec((rows, d), lambda j: (0, 0)),
                  pl.BlockSpec((d, tn), lambda j: (0, j)),
                  pl.BlockSpec((1, tn), lambda j: (0, j))],
        out_specs=pl.BlockSpec((rows, tn), lambda j: (0, j)),
        out_shape=jax.ShapeDtypeStruct((rows, n), F32),
        compiler_params=_params("parallel"),
        name="ada",
    )(c_all, w_ada, b_ada.reshape(1, n))


def _cmpconst_kernel(pe_ref, w1_ref, b1_ref, o_ref):
    y = _dot(pe_ref[...].astype(BF16), w1_ref[...])
    o_ref[...] = y[0:1, :HEAD_DIM] + y[1:2, HEAD_DIM:] + b1_ref[...]


def _cmpconst(pe_rows, w1cat, b1):
    return pl.pallas_call(
        _cmpconst_kernel,
        grid=(2,),
        in_specs=[pl.BlockSpec((None, SUBLANES, 16 * HEAD_DIM), lambda s: (s, 0, 0)),
                  pl.BlockSpec((None, 16 * HEAD_DIM, 2 * HEAD_DIM), lambda s: (s, 0, 0)),
                  pl.BlockSpec((None, 1, HEAD_DIM), lambda s: (s, 0, 0))],
        out_specs=pl.BlockSpec((None, 1, HEAD_DIM), lambda s: (s, 0, 0)),
        out_shape=jax.ShapeDtypeStruct((2, 1, HEAD_DIM), F32),
        compiler_params=_params("parallel"),
        name="cmpconst",
    )(pe_rows, w1cat, b1)


_TN = 512
_J_KV = Q_COLS // _TN
_J_REST = (Q_COLS + KV_COLS) // _TN
_REST_COLS = NSA_WIDTH + 3 * CONV_DIM + WIN_COLS
_N_MAIN = Q_COLS + KV_COLS + _REST_COLS
_J_WIN = (_N_MAIN - WIN_COLS) // _TN
_NJ = _N_MAIN // _TN
_REST_Z = 0
_REST_A = 1
_REST_GL = 2
_REST_ZC = 3
_REST_WIN = (NSA_WIDTH + 3 * CONV_DIM) // WIN_COLS


_C_GATE = Q_COLS + KV_COLS + WIN_COLS
_C_REST = _C_GATE + GATE_COLS


def _wprep_kernel(wt_ref, o_ref, buf, sem):
    j = pl.program_id(0)
    r0 = jnp.where(j < _J_REST, j * _TN,
                   jnp.where(j < _J_WIN, _C_REST + (j - _J_REST) * _TN, _C_GATE - WIN_COLS))
    cp = pltpu.make_async_copy(wt_ref.at[pl.ds(pl.multiple_of(r0, SUBLANES), _TN), :], buf, sem.at[0])
    cp.start()
    cp.wait()
    o_ref[...] = buf[...].astype(BF16)


def _wprep(w_t):
    n, d = w_t.shape
    assert _C_REST % SUBLANES == 0 and _C_REST + (_J_WIN - _J_REST) * _TN == n
    return pl.pallas_call(
        _wprep_kernel,
        grid=(_NJ,),
        in_specs=[pl.BlockSpec(memory_space=pl.ANY)],
        out_specs=pl.BlockSpec((_TN, d), lambda j: (j, 0)),
        out_shape=jax.ShapeDtypeStruct((_N_MAIN, d), BF16),
        scratch_shapes=[pltpu.VMEM((_TN, d), F32), pltpu.SemaphoreType.DMA((1,))],
        compiler_params=_params("arbitrary"),
        name="wprep",
    )(w_t)


def _inproj_kernel(x_ref, g_ref, sc_ref, sh_ref, w_ref, wg_ref,
                   q_ref, kv_ref, kvb_ref, rest_ref, winb_ref, gate_ref, h_ref):
    j = pl.program_id(1)

    @pl.when(j == 0)
    def _():
        x = x_ref[...]
        y = x * lax.rsqrt(jnp.mean(x * x, axis=-1, keepdims=True) + NORM_EPS) * g_ref[...]
        h = (y * (1.0 + sc_ref[...]) + sh_ref[...]).astype(BF16)
        h_ref[...] = h
        gate_ref[...] = _dot(h, wg_ref[...])

    tile = lambda: _dot_nt(h_ref[...], w_ref[...])

    @pl.when(j < _J_KV)
    def _():
        q_ref[...] = tile().astype(BF16)

    tm = h_ref.shape[0]
    streams_per_tile = _TN // HEAD_DIM
    for jj in range(_J_KV, _J_REST):
        @pl.when(j == jj)
        def _(jj=jj):
            acc = tile()
            kvb_ref[...] = acc.astype(BF16)
            for cc in range(streams_per_tile):
                c = (jj - _J_KV) * streams_per_tile + cc
                kv_ref[pl.ds(c, tm, stride=_KV_STREAMS), :] = acc[:, cc * HEAD_DIM:(cc + 1) * HEAD_DIM]

    @pl.when((j >= _J_REST) & (j != _J_WIN))
    def _():
        rest_ref[...] = tile()

    @pl.when(j == _J_WIN)
    def _():
        acc = tile()
        rest_ref[...] = acc
        winb_ref[...] = acc.astype(BF16)


def _inproj(x, g, scale3, shift3, w_main, w_gate, tm, rows_per_mod):
    m, d = x.shape
    mod_rows = scale3.shape[1]
    tiles_per_mod = rows_per_mod // tm

    def mod_map(i, j):
        return (i // tiles_per_mod, 0, 0)

    clampj = lambda j, lo, n: jnp.clip(j - lo, 0, n - 1)
    return pl.pallas_call(
        _inproj_kernel,
        grid=(m // tm, _NJ),
        in_specs=[pl.BlockSpec((tm, d), lambda i, j: (i, 0)),
                  pl.BlockSpec((1, d), lambda i, j: (0, 0)),
                  pl.BlockSpec((None, mod_rows, d), mod_map),
                  pl.BlockSpec((None, mod_rows, d), mod_map),
                  pl.BlockSpec((_TN, d), lambda i, j: (j, 0)),
                  pl.BlockSpec((d, 2 * LANES), lambda i, j: (0, 0))],
        out_specs=[pl.BlockSpec((tm, _TN), lambda i, j: (i, clampj(j, 0, _J_KV))),
                   pl.BlockSpec((tm * _KV_STREAMS, HEAD_DIM), lambda i, j: (i, 0)),
                   pl.BlockSpec((tm, _TN), lambda i, j: (i, clampj(j, _J_KV, _J_REST - _J_KV))),
                   pl.BlockSpec((tm, _TN), lambda i, j: (i, clampj(j, _J_REST, _NJ - _J_REST))),
                   pl.BlockSpec((tm, _TN), lambda i, j: (i, 0)),
                   pl.BlockSpec((tm, 2 * LANES), lambda i, j: (i, 0))],
        out_shape=[jax.ShapeDtypeStruct((m, Q_COLS), BF16),
                   jax.ShapeDtypeStruct((m * _KV_STREAMS, HEAD_DIM), F32),
                   jax.ShapeDtypeStruct((m, KV_COLS), BF16),
                   jax.ShapeDtypeStruct((m, _REST_COLS), F32),
                   jax.ShapeDtypeStruct((m, WIN_COLS), BF16),
                   jax.ShapeDtypeStruct((m, 2 * LANES), F32)],
        scratch_shapes=[pltpu.VMEM((tm, d), BF16)],
        compiler_params=_params("parallel", "arbitrary"),
        name="inproj",
    )(x, g, scale3, shift3, w_main, w_gate)


def _compress_rows(load_rows, nch, w1, cconst, w2, b2):
    lhs = jnp.concatenate([load_rows(l) for l in range(CMP_STRIDE)], axis=1).astype(BF16)
    y = _dot(lhs, w1)
    second = pltpu.roll(y[:, HEAD_DIM:], shift=nch - 1, axis=0)
    h = _silu(y[:, :HEAD_DIM] + second + cconst)
    out = _dot(h.astype(BF16), w2) + b2
    row = lax.broadcasted_iota(jnp.int32, out.shape, 0)
    return jnp.where(row < nch - 1, out, 0.0)


def _compress_kernel(x_ref, w1_ref, cc_ref, w2_ref, b2_ref, o_ref, *, nch, ncp):
    for c in range(2 * N_KV_HEADS):
        s = c // N_KV_HEADS

        def load_rows(l, c=c):
            return x_ref[pl.ds(l * _KV_STREAMS + c, nch, stride=CMP_STRIDE * _KV_STREAMS), :]

        out = _compress_rows(load_rows, nch, w1_ref[s], cc_ref[s], w2_ref[s], b2_ref[s])
        if ncp > nch:
            out = jnp.concatenate([out, jnp.zeros((ncp - nch, HEAD_DIM), F32)], axis=0)
        o_ref[c] = out.astype(BF16)


def _compress(kv_streams, w1cat, cconst, w2b, b2, ncp):
    b, rows, _ = kv_streams.shape
    t = rows // _KV_STREAMS
    nch = t // CMP_STRIDE
    return pl.pallas_call(
        functools.partial(_compress_kernel, nch=nch, ncp=ncp),
        grid=(b,),
        in_specs=[pl.BlockSpec((None, rows, HEAD_DIM), lambda i: (i, 0, 0)),
                  pl.BlockSpec((2, 16 * HEAD_DIM, 2 * HEAD_DIM), lambda i: (0, 0, 0)),
                  pl.BlockSpec((2, 1, HEAD_DIM), lambda i: (0, 0, 0)),
                  pl.BlockSpec((2, HEAD_DIM, HEAD_DIM), lambda i: (0, 0, 0)),
                  pl.BlockSpec((2, 1, HEAD_DIM), lambda i: (0, 0, 0))],
        out_specs=pl.BlockSpec((None, 4, ncp, HEAD_DIM), lambda i: (i, 0, 0, 0)),
        out_shape=jax.ShapeDtypeStruct((b, 4, ncp, HEAD_DIM), BF16),
        compiler_params=_params("parallel"),
        name="compress",
    )(kv_streams, w1cat, cconst, w2b, b2)


def _select_mask(psum, ov, qpos, n_sel):
    hi = psum.astype(BF16)
    lo = (psum - hi.astype(F32)).astype(BF16)
    imp = _dot(hi, ov) + _dot(lo, ov)
    rows = imp.shape[0]
    jio = lax.broadcasted_iota(jnp.int32, (rows, LANES), 1)
    allowed = (jio * SEL_BLOCK <= qpos) & (jio < n_sel)
    cur = qpos // SEL_BLOCK
    forced = (jio == 0) | (jio == cur) | (jio == cur - 1)
    score = jnp.where(allowed, imp + jnp.where(forced, FORCE_BONUS, 0.0), NEG_INF)
    rank = jnp.zeros((rows, LANES), F32)
    for i in range(n_sel):
        si = jnp.broadcast_to(score[:, i:i + 1], (rows, LANES))
        tie = jnp.where(jio > i, 1.0, 0.0)
        rank = rank + jnp.where(si > score, 1.0, jnp.where(si == score, tie, 0.0))
    return (rank < float(min(N_SELECT, n_sel))) & allowed


def _select_mask_t(psum, ovt, qpos_row, n_sel):
    hi = psum.astype(BF16)
    lo = (psum - hi.astype(F32)).astype(BF16)
    imp = _dot_nt(ovt, hi) + _dot_nt(ovt, lo)
    sr, rows = imp.shape
    jio = lax.broadcasted_iota(jnp.int32, (sr, rows), 0)
    allowed = (jio * SEL_BLOCK <= qpos_row) & (jio < n_sel)
    cur = qpos_row // SEL_BLOCK
    forced = (jio == 0) | (jio == cur) | (jio == cur - 1)
    score = jnp.where(allowed, imp + jnp.where(forced, FORCE_BONUS, 0.0), NEG_INF)
    rank = jnp.zeros((sr, rows), F32)
    for i in range(n_sel):
        si = jnp.broadcast_to(score[i:i + 1, :], (sr, rows))
        tie = jnp.where(jio > i, 1.0, 0.0)
        rank = rank + jnp.where(si > score, 1.0, jnp.where(si == score, tie, 0.0))
    sel_t = jnp.where((rank < float(min(N_SELECT, n_sel))) & allowed, 1.0, 0.0)
    sel_t = jnp.concatenate([sel_t, jnp.zeros((LANES - sr, rows), F32)], axis=0)
    return sel_t.T


def _biased_softmax(s, bias):
    s = s + bias
    e = jnp.exp2(s - jnp.max(s, axis=-1, keepdims=True))
    return e / jnp.maximum(jnp.sum(e, axis=-1, keepdims=True), 1e-30)


def _attn_p_kernel(q_ref, kc_ref, vc_ref, ks_ref, vs_ref, kw_ref, vw_ref, gate_ref, z_ref, ovt_ref, e_ref,
                   o_ref, m_s, l_s, acc_s, pre_s, *, tq, tk, n_cmp, n_sel):
    qi = pl.program_id(2)
    st = LANES
    subs = range(tq // st)
    chains = [(u, r) for u in subs for r in range(HEADS_PER_GROUP)]
    rows = lambda u: slice(u * st, (u + 1) * st)
    cid = lambda u, r: u * HEADS_PER_GROUP + r
    q0s = [qi * tq + u * st for u in subs]
    qpos = [q0s[u] + lax.broadcasted_iota(jnp.int32, (st, 1), 0) for u in subs]
    q_of = lambda u, r: q_ref[rows(u), r * HEAD_DIM:(r + 1) * HEAD_DIM]
    gates = [_sigmoid(gate_ref[rows(u), :]) for u in subs]
    g_of = lambda br, u, r: jnp.broadcast_to(
        gates[u][:, br * HEADS_PER_GROUP + r:br * HEADS_PER_GROUP + r + 1], (st, HEAD_DIM))

    kc = kc_ref[...]
    vc = vc_ref[...]
    w0 = [pl.multiple_of(jnp.maximum(q0s[u] - WINDOW, 0), LANES) for u in subs]
    k_w = [kw_ref[pl.ds(w0[u], WINDOW + st), :] for u in subs]
    v_w = [vw_ref[pl.ds(w0[u], WINDOW + st), :] for u in subs]
    wbias = []
    for u in subs:
        dist = qpos[u] - (w0[u] + lax.broadcasted_iota(jnp.int32, (st, WINDOW + st), 1))
        wbias.append(jnp.where((dist >= 0) & (dist <= WINDOW), 0.0, NEG_INF))
    nio = lax.broadcasted_iota(jnp.int32, (st, kc.shape[0]), 1)
    cmask = [(nio < n_cmp) & (nio * CMP_STRIDE + (CMP_BLOCK - 1) <= qpos[u]) for u in subs]

    s_cmp = {c: _dot_nt(q_of(*c), kc) * SM_SCALE_LOG2 for c in chains}
    s_win = {c: _dot_nt(q_of(*c), k_w[c[0]]) * SM_SCALE_LOG2 for c in chains}
    p_cmp = {c: _masked_softmax(s_cmp[c], cmask[c[0]]) for c in chains}
    psum = [(p_cmp[(u, 0)] + p_cmp[(u, 1)]) + (p_cmp[(u, 2)] + p_cmp[(u, 3)]) for u in subs]
    p_win = {c: _biased_softmax(s_win[c], wbias[c[0]]) for c in chains}
    selb = []
    for u in subs:
        qpos_row = q0s[u] + lax.broadcasted_iota(jnp.int32, (1, st), 1)
        selb.append(_select_mask_t(psum[u], ovt_ref[...], qpos_row, n_sel).astype(BF16))
    for (u, r) in chains:
        pre_s[cid(u, r)] = (g_of(0, u, r) * _dot(p_cmp[(u, r)].astype(BF16), vc)
                            + g_of(2, u, r) * _dot(p_win[(u, r)].astype(BF16), v_w[u]))

    m_s[...] = jnp.full(m_s.shape, NEG_INF, F32)
    l_s[...] = jnp.zeros(l_s.shape, F32)
    acc_s[...] = jnp.zeros(acc_s.shape, F32)
    n_kt = (qi * tq + tq - 1) // tk + 1

    def sweep(kt, carry):
        k0 = pl.multiple_of(kt * tk, tk)
        k_t = ks_ref[pl.ds(k0, tk), :]
        v_t = vs_ref[pl.ds(k0, tk), :]
        kpos = k0 + lax.broadcasted_iota(jnp.int32, (st, tk), 1)
        bias = [jnp.where((_dot(selb[u], e_ref[kt]) > 0.5) & (kpos <= qpos[u]), 0.0, NEG_INF) for u in subs]
        s = {c: _dot_nt(q_of(*c), k_t) * SM_SCALE_LOG2 + bias[c[0]] for c in chains}
        m_prev = {c: m_s[cid(*c)] for c in chains}
        m_new = {c: jnp.maximum(m_prev[c], jnp.max(s[c], axis=-1, keepdims=True)) for c in chains}
        p = {c: jnp.exp2(s[c] - jnp.concatenate([m_new[c]] * (tk // LANES), axis=1)) for c in chains}
        alpha = {c: jnp.exp2(m_prev[c] - m_new[c]) for c in chains}
        for c in chains:
            k = cid(*c)
            l_s[k] = alpha[c] * l_s[k] + jnp.sum(p[c], axis=-1, keepdims=True)
            acc_s[k] = alpha[c] * acc_s[k] + _dot(p[c].astype(BF16), v_t)
            m_s[k] = m_new[c]
        return carry

    lax.fori_loop(0, n_kt, sweep, 0)

    for (u, r) in chains:
        k = cid(u, r)
        o_slc = acc_s[k] / jnp.maximum(l_s[k], 1e-30)
        o = pre_s[k] + g_of(1, u, r) * o_slc
        o_ref[rows(u), r * HEAD_DIM:(r + 1) * HEAD_DIM] = (
            o * _silu(z_ref[rows(u), r * HEAD_DIM:(r + 1) * HEAD_DIM])).astype(BF16)


def _attn_p(q_bf, kcv, kv_bf3, win_bf3, gate, rest, ovt, e_p, b, t, n_cmp, n_sel, tq=256, tk=256):
    assert t >= WINDOW + tq and t % tk == 0 and tq % LANES == 0 and t % tq == 0
    nq = t // tq
    n_chain = HEADS_PER_GROUP * (tq // LANES)
    ncp = kcv.shape[2]
    sr = ovt.shape[0]
    gw = HEADS_PER_GROUP * HEAD_DIM
    row = lambda bi, g, qi: bi * nq + qi
    return pl.pallas_call(
        functools.partial(_attn_p_kernel, tq=tq, tk=tk, n_cmp=n_cmp, n_sel=n_sel),
        grid=(b, N_KV_HEADS, nq),
        in_specs=[pl.BlockSpec((tq, gw), lambda bi, g, qi: (row(bi, g, qi), g)),
                  pl.BlockSpec((None, None, ncp, HEAD_DIM), lambda bi, g, qi: (bi, g, 0, 0)),
                  pl.BlockSpec((None, None, ncp, HEAD_DIM), lambda bi, g, qi: (bi, N_KV_HEADS + g, 0, 0)),
                  pl.BlockSpec((None, t, HEAD_DIM), lambda bi, g, qi: (bi, 0, 2 * N_KV_HEADS + g)),
                  pl.BlockSpec((None, t, HEAD_DIM), lambda bi, g, qi: (bi, 0, 3 * N_KV_HEADS + g)),
                  pl.BlockSpec((None, t, HEAD_DIM), lambda bi, g, qi: (bi, 0, g)),
                  pl.BlockSpec((None, t, HEAD_DIM), lambda bi, g, qi: (bi, 0, N_KV_HEADS + g)),
                  pl.BlockSpec((tq, LANES), lambda bi, g, qi: (row(bi, g, qi), g)),
                  pl.BlockSpec((tq, gw), lambda bi, g, qi: (row(bi, g, qi), g)),
                  pl.BlockSpec((sr, ncp), lambda bi, g, qi: (0, 0)),
                  pl.BlockSpec((t // tk, LANES, tk), lambda bi, g, qi: (0, 0, 0))],
        out_specs=pl.BlockSpec((tq, gw), lambda bi, g, qi: (row(bi, g, qi), g)),
        out_shape=jax.ShapeDtypeStruct((b * t, NSA_WIDTH), BF16),
        scratch_shapes=[pltpu.VMEM((n_chain, LANES, LANES), F32),
                        pltpu.VMEM((n_chain, LANES, LANES), F32),
                        pltpu.VMEM((n_chain, LANES, HEAD_DIM), F32),
                        pltpu.VMEM((n_chain, LANES, HEAD_DIM), F32)],
        compiler_params=_params("parallel", "parallel", "arbitrary"),
        name="attn_p",
    )(q_bf, kcv, kcv, kv_bf3, kv_bf3, win_bf3, win_bf3, gate, rest, ovt, e_p)


def _attn_s_kernel(pt_ref, cache_ref, wst_ref, q_ref, kvn_ref, winn_ref, gate_ref, z_ref, winc_ref,
                   w1_ref, cc_ref, w2_ref, b2_ref, ov_ref, e_ref, perm_ref, o_ref, wout_ref,
                   kvbuf, wbuf, wnew, kcs, ksl, kwn, sem, wsem, osem,
                   *, n_pages, page, n_cmp, n_sel, pos0, t_new):
    i = pl.program_id(0)
    nb = pl.num_programs(0) - 1
    past = n_pages * page
    nch = past // CMP_STRIDE
    rows = HEADS_PER_GROUP * SAMPLE_ROWS
    n_ws = N_WIN_SLOTS * N_KV_HEADS
    state_rows = WINDOW * n_ws
    new_rows = t_new * n_ws
    prow = page * _KV_STREAMS
    c_seq = jnp.minimum(i, nb - 1)
    a_seq = jnp.maximum(i - 1, 0)
    cs = c_seq % 2
    asl = a_seq % 2

    def page_copy(seq, p, slot):
        return pltpu.make_async_copy(cache_ref.at[pt_ref[seq, p]], kvbuf.at[slot, pl.ds(p * prow, prow)],
                                     sem.at[slot])

    def win_copy(seq, slot):
        return pltpu.make_async_copy(wst_ref.at[seq], wbuf.at[slot], wsem.at[slot])

    def fetch(seq, slot):
        for p in range(n_pages):
            page_copy(seq, p, slot).start()
        win_copy(seq, slot).start()

    def state_copies(seq, slot):
        kept = pltpu.make_async_copy(wbuf.at[slot, pl.ds(new_rows, state_rows - new_rows)],
                                     wout_ref.at[seq, pl.ds(0, state_rows - new_rows)], osem.at[slot])
        fresh = pltpu.make_async_copy(wnew.at[slot], wout_ref.at[seq, pl.ds(state_rows - new_rows, new_rows)],
                                      osem.at[slot])
        return kept, fresh

    @pl.when(i == 0)
    def _():
        fetch(0, 0)
        kcs[0] = jnp.zeros(kcs.shape[1:], BF16)
        ksl[0] = jnp.zeros(ksl.shape[1:], BF16)
        kwn[0] = jnp.zeros(kwn.shape[1:], BF16)

    @pl.when(i >= 1)
    def _():
        for cp in state_copies(i - 1, (i - 1) % 2):
            cp.wait()

    @pl.when(i + 1 < nb)
    def _():
        fetch(i + 1, (i + 1) % 2)

    @pl.when(i < nb)
    def _():
        for p in range(n_pages):
            page_copy(i, p, i % 2).wait()
        win_copy(i, i % 2).wait()
        for t in range(t_new):
            for c in range(n_ws):
                wnew[i % 2, pl.ds(t * n_ws + c, 1), :] = winc_ref[pl.ds(t, 1), c * HEAD_DIM:(c + 1) * HEAD_DIM]
        for cp in state_copies(i, i % 2):
            cp.start()

    qpos = pos0 + lax.broadcasted_iota(jnp.int32, (rows, 1), 0) % SAMPLE_ROWS
    kvn = kvn_ref[...].astype(F32)
    qf = q_ref[...].astype(F32)
    gates = _sigmoid(gate_ref[...])
    zero_tail = jnp.zeros((LANES - SUBLANES, HEAD_DIM), F32)

    def tail_block(new_f32):
        return jnp.concatenate([new_f32, zero_tail], axis=0).astype(BF16)

    def scores(q, parked, tail):
        return jnp.concatenate([_dot_nt(q, parked), _dot_nt(q, tail)], axis=1) * SM_SCALE_LOG2

    def weighted(p, parked, tail):
        n_old = parked.shape[0]
        return _dot(p[:, :n_old].astype(BF16), parked) + _dot(p[:, n_old:].astype(BF16), tail)

    def stack_heads(x8):
        return jnp.concatenate([x8] * HEADS_PER_GROUP, axis=0)

    groups = range(N_KV_HEADS)
    nk = past + LANES
    kpos = lax.broadcasted_iota(jnp.int32, (rows, nk), 1)
    wpos = (pos0 - WINDOW) + lax.broadcasted_iota(jnp.int32, (rows, WINDOW + LANES), 1)
    dist = qpos - wpos
    wmask = (dist >= 0) & (dist <= WINDOW) & (wpos >= 0)
    nio = lax.broadcasted_iota(jnp.int32, (rows, nch), 1)
    cmask = (nio < n_cmp) & (nio * CMP_STRIDE + (CMP_BLOCK - 1) <= qpos)

    stream = lambda c: kvbuf[cs, pl.ds(c, past, stride=_KV_STREAMS), :].astype(BF16)
    xs0 = jnp.concatenate([stream(g) for g in groups], axis=1)

    qgs, s_cmp, s_slc, s_win, v_tails, vw_tails = [], [], [], [], [], []
    for g in groups:
        heads = [qf[:, (g * HEADS_PER_GROUP + r) * HEAD_DIM:(g * HEADS_PER_GROUP + r + 1) * HEAD_DIM]
                 for r in range(HEADS_PER_GROUP)]
        qgs.append(jnp.concatenate(heads, axis=0).astype(BF16))
    for g in groups:
        s_cmp.append(_dot_nt(qgs[g], kcs[asl, g]) * SM_SCALE_LOG2)
    for g in groups:
        c_k = (2 * N_KV_HEADS + g) * HEAD_DIM
        c_v = (3 * N_KV_HEADS + g) * HEAD_DIM
        v_tails.append(tail_block(kvn[:, c_v:c_v + HEAD_DIM]))
        s_slc.append(scores(qgs[g], ksl[asl, g], tail_block(kvn[:, c_k:c_k + HEAD_DIM])))
    for g in groups:
        wk = g * HEAD_DIM
        wv = (N_KV_HEADS + g) * HEAD_DIM
        vw_tails.append(tail_block(winn_ref[:, wv:wv + HEAD_DIM]))
        s_win.append(scores(qgs[g], kwn[asl, g], tail_block(winn_ref[:, wk:wk + HEAD_DIM])))

    vcs = [kcs[asl, N_KV_HEADS + g] for g in groups]

    perm = perm_ref[...]
    blk = LANES
    cpb = blk // CMP_STRIDE

    def slot_streams(s):
        return jnp.concatenate([stream(s * N_KV_HEADS + g) for g in groups], axis=1)

    def slot_chunks(xs):
        out = [[] for _ in groups]
        for k in range(past // blk):
            y = _dot(perm, xs[k * blk:(k + 1) * blk])
            for g in groups:
                out[g].append(jnp.concatenate(
                    [y[l * cpb:(l + 1) * cpb, g * HEAD_DIM:(g + 1) * HEAD_DIM] for l in range(CMP_STRIDE)], axis=1))
        return jnp.concatenate([jnp.concatenate(o, axis=0) for o in out], axis=0).astype(BF16)

    def compress_slot(s, lhs):
        y = _dot(lhs, w1_ref[s])
        second = pltpu.roll(y[:, HEAD_DIM:], shift=lhs.shape[0] - 1, axis=0)
        h = _silu(y[:, :HEAD_DIM] + second + cc_ref[s])
        out = _dot(h.astype(BF16), w2_ref[s]) + b2_ref[s]
        row = lax.broadcasted_iota(jnp.int32, out.shape, 0) % nch
        out = jnp.where(row < nch - 1, out, 0.0).astype(BF16)
        for g in groups:
            kcs[cs, s * N_KV_HEADS + g] = out[g * nch:(g + 1) * nch]

    xs1 = slot_streams(1)
    lhs0 = slot_chunks(xs0)

    p_cmps = [_masked_softmax(s_cmp[g], cmask) for g in groups]
    psums = [(p[0:8] + p[8:16]) + (p[16:24] + p[24:32]) for p in p_cmps]
    sels = [_select_mask(psums[g], ov_ref[...], qpos[0:SAMPLE_ROWS], n_sel) for g in groups]
    selbs = [jnp.where(s, 1.0, 0.0).astype(BF16) for s in sels]

    for k in range(N_KV_HEADS):
        ksl[cs, k] = stream(2 * N_KV_HEADS + k)
    compress_slot(0, lhs0)
    lhs1 = slot_chunks(xs1)

    smasks = [(stack_heads(_dot(selbs[g], e_ref[...])) > 0.5) & (kpos <= qpos) for g in groups]
    o_cmps = [_dot(p_cmps[g].astype(BF16), vcs[g]) for g in groups]
    p_wins = [_masked_softmax(s_win[g], wmask) for g in groups]
    o_wins = [weighted(p_wins[g], kwn[asl, N_KV_HEADS + g], vw_tails[g]) for g in groups]
    p_slcs = [_masked_softmax(s_slc[g], smasks[g]) for g in groups]
    o_slcs = [weighted(p_slcs[g], ksl[asl, N_KV_HEADS + g], v_tails[g]) for g in groups]

    for k in range(N_KV_HEADS, 2 * N_KV_HEADS):
        ksl[cs, k] = stream(2 * N_KV_HEADS + k)
    compress_slot(1, lhs1)
    for k in range(n_ws):
        kwn[cs, k] = wbuf[cs, pl.ds(k, WINDOW, stride=n_ws), :].astype(BF16)

    for g in groups:
        o_cmp, o_slc, o_win = o_cmps[g], o_slcs[g], o_wins[g]
        for r in range(HEADS_PER_GROUP):
            h = g * HEADS_PER_GROUP + r
            rs = slice(r * SAMPLE_ROWS, (r + 1) * SAMPLE_ROWS)
            g_of = lambda br: jnp.broadcast_to(
                gates[:, g * LANES + br * HEADS_PER_GROUP + r:g * LANES + br * HEADS_PER_GROUP + r + 1],
                (SAMPLE_ROWS, HEAD_DIM))
            o = g_of(0) * o_cmp[rs] + g_of(1) * o_slc[rs] + g_of(2) * o_win[rs]
            o_ref[:, h * HEAD_DIM:(h + 1) * HEAD_DIM] = (
                o * _silu(z_ref[:, h * HEAD_DIM:(h + 1) * HEAD_DIM])).astype(BF16)


def _attn_s(page_table, cache3, wst3, q3, kvn3, rest3, gate3, w1cat, cconst, w2b, b2, ov, e_s,
            n_cmp, n_sel, pos0, t_new):
    nb, n_pages = page_table.shape
    page = cache3.shape[1] // _KV_STREAMS
    past = n_pages * page
    nch = past // CMP_STRIDE
    n_ws = N_WIN_SLOTS * N_KV_HEADS
    assert wst3.shape[1:] == (WINDOW * n_ws, HEAD_DIM) and past % LANES == 0
    ncp = ov.shape[0]
    prev = lambda i: jnp.maximum(i - 1, 0)
    grid_spec = pltpu.PrefetchScalarGridSpec(
        num_scalar_prefetch=1,
        grid=(nb + 1,),
        in_specs=[pl.BlockSpec(memory_space=pl.ANY),
                  pl.BlockSpec(memory_space=pl.ANY),
                  pl.BlockSpec((None, SAMPLE_ROWS, Q_COLS), lambda i, pt: (prev(i), 0, 0)),
                  pl.BlockSpec((None, SAMPLE_ROWS, KV_COLS), lambda i, pt: (prev(i), 0, 0)),
                  pl.BlockSpec((None, SAMPLE_ROWS, WIN_COLS), lambda i, pt: (prev(i), 0, _REST_WIN)),
                  pl.BlockSpec((None, SAMPLE_ROWS, 2 * LANES), lambda i, pt: (prev(i), 0, 0)),
                  pl.BlockSpec((None, SAMPLE_ROWS, NSA_WIDTH), lambda i, pt: (prev(i), 0, _REST_Z)),
                  pl.BlockSpec((None, SAMPLE_ROWS, WIN_COLS), lambda i, pt: (jnp.minimum(i, nb - 1), 0, _REST_WIN)),
                  pl.BlockSpec((2, 16 * HEAD_DIM, 2 * HEAD_DIM), lambda i, pt: (0, 0, 0)),
                  pl.BlockSpec((2, 1, HEAD_DIM), lambda i, pt: (0, 0, 0)),
                  pl.BlockSpec((2, HEAD_DIM, HEAD_DIM), lambda i, pt: (0, 0, 0)),
                  pl.BlockSpec((2, 1, HEAD_DIM), lambda i, pt: (0, 0, 0)),
                  pl.BlockSpec((ncp, LANES), lambda i, pt: (0, 0)),
                  pl.BlockSpec((LANES, past + LANES), lambda i, pt: (0, 0)),
                  pl.BlockSpec((LANES, LANES), lambda i, pt: (0, 0))],
        out_specs=[pl.BlockSpec((None, SAMPLE_ROWS, NSA_WIDTH), lambda i, pt: (prev(i), 0, 0)),
                   pl.BlockSpec(memory_space=pl.ANY)],
        scratch_shapes=[pltpu.VMEM((2, past * _KV_STREAMS, HEAD_DIM), F32),
                        pltpu.VMEM((2, WINDOW * n_ws, HEAD_DIM), F32),
                        pltpu.VMEM((2, t_new * n_ws, HEAD_DIM), F32),
                        pltpu.VMEM((2, 2 * N_KV_HEADS, nch, HEAD_DIM), BF16),
                        pltpu.VMEM((2, 2 * N_KV_HEADS, past, HEAD_DIM), BF16),
                        pltpu.VMEM((2, n_ws, WINDOW, HEAD_DIM), BF16),
                        pltpu.SemaphoreType.DMA((2,)),
                        pltpu.SemaphoreType.DMA((2,)),
                        pltpu.SemaphoreType.DMA((2,))],
    )
    return pl.pallas_call(
        functools.partial(_attn_s_kernel, n_pages=n_pages, page=page, n_cmp=n_cmp, n_sel=n_sel, pos0=pos0,
                          t_new=t_new),
        grid_spec=grid_spec,
        out_shape=[jax.ShapeDtypeStruct((nb, SAMPLE_ROWS, NSA_WIDTH), BF16),
                   jax.ShapeDtypeStruct((nb, WINDOW * n_ws, HEAD_DIM), F32)],
        compiler_params=_params("arbitrary"),
        name="attn_s",
    )(page_table, cache3, wst3, q3, kvn3, rest3, gate3, rest3, rest3, w1cat, cconst, w2b, b2, ov, e_s,
      _chunk_permutation())


_CONV_RC = 64
_CONV_LC = 256


def _depthwise_conv(uext, w_ref, db_ref, ybuf, tt, row0=0):
    off = CONV_HALO - (CONV_WIDTH - 1)
    uext[CONV_HALO + tt:CONV_HALO + tt + SUBLANES] = jnp.zeros((SUBLANES, CONV_DIM), F32)
    rc = min(_CONV_RC, tt)
    for r0 in range(0, tt, rc):
        for c0 in range(0, CONV_DIM, _CONV_LC):
            lanes = slice(c0, c0 + _CONV_LC)
            acc = jnp.broadcast_to(db_ref[:, lanes], (rc, _CONV_LC))
            for r in range(SUBLANES):
                z = None
                for a in range((CONV_WIDTH + off) // SUBLANES + 1):
                    k = SUBLANES * a + r - off
                    if 0 <= k < CONV_WIDTH:
                        term = w_ref[k:k + 1, lanes] * uext[r0 + SUBLANES * a:r0 + SUBLANES * a + rc + SUBLANES, lanes]
                        z = term if z is None else z + term
                acc = acc + z[r:r + rc]
            ybuf[row0 + r0:row0 + r0 + rc, lanes] = acc


def _conv_kernel(a_ref, gl_ref, ah_ref, glh_ref, init_ref, zc_ref, w_ref, db_ref, lg_ref, lb_ref,
                 cv_ref, st_ref, uext, ybuf, *, tt, t_valid):
    ti = pl.program_id(1)
    nt = pl.num_programs(1)
    halo = ah_ref[...] * _sigmoid(glh_ref[...])
    uext[0:CONV_HALO] = jnp.where(ti == 0, init_ref[...], halo)
    uext[CONV_HALO:CONV_HALO + tt] = a_ref[...] * _sigmoid(gl_ref[...])
    _depthwise_conv(uext, w_ref, db_ref, ybuf, tt)

    y = ybuf[...]
    mu = jnp.mean(y, axis=-1, keepdims=True)
    var = jnp.mean(jnp.square(y - mu), axis=-1, keepdims=True)
    ln = (y - mu) * lax.rsqrt(var + NORM_EPS) * lg_ref[...] + lb_ref[...]
    cv_ref[...] = (_silu(ln) * _silu(zc_ref[...])).astype(BF16)

    @pl.when(ti == nt - 1)
    def _():
        st_ref[...] = uext[pl.ds(t_valid, CONV_HALO), :]


def _conv(rest, u_init, w_pad, db, lg, lb, b, t, tt, t_valid):
    assert tt % CONV_HALO == 0 and t % tt == 0
    nt = t // tt
    hb = tt // CONV_HALO
    row = lambda bi, ti: bi * nt + ti
    halo_map = lambda col: (lambda bi, ti: (jnp.maximum(row(bi, ti) * hb - 1, 0), col))
    halo_rows = CONV_HALO
    vec = lambda: pl.BlockSpec((1, CONV_DIM), lambda bi, ti: (0, 0))
    return pl.pallas_call(
        functools.partial(_conv_kernel, tt=tt, t_valid=t_valid),
        grid=(b, nt),
        in_specs=[pl.BlockSpec((tt, CONV_DIM), lambda bi, ti: (row(bi, ti), _REST_A)),
                  pl.BlockSpec((tt, CONV_DIM), lambda bi, ti: (row(bi, ti), _REST_GL)),
                  pl.BlockSpec((halo_rows, CONV_DIM), halo_map(_REST_A)),
                  pl.BlockSpec((halo_rows, CONV_DIM), halo_map(_REST_GL)),
                  pl.BlockSpec((None, CONV_HALO, CONV_DIM), lambda bi, ti: (bi, 0, 0)),
                  pl.BlockSpec((tt, CONV_DIM), lambda bi, ti: (row(bi, ti), _REST_ZC)),
                  pl.BlockSpec((CONV_HALO, CONV_DIM), lambda bi, ti: (0, 0)),
                  vec(), vec(), vec()],
        out_specs=[pl.BlockSpec((tt, CONV_DIM), lambda bi, ti: (row(bi, ti), 0)),
                   pl.BlockSpec((None, CONV_HALO, CONV_DIM), lambda bi, ti: (bi, 0, 0))],
        out_shape=[jax.ShapeDtypeStruct((b * t, CONV_DIM), BF16),
                   jax.ShapeDtypeStruct((b, CONV_HALO, CONV_DIM), F32)],
        scratch_shapes=[pltpu.VMEM((CONV_HALO + tt + SUBLANES, CONV_DIM), F32),
                        pltpu.VMEM((tt, CONV_DIM), F32)],
        compiler_params=_params("parallel", "arbitrary"),
        name="conv",
    )(rest, rest, rest, rest, u_init, rest, w_pad, db, lg, lb)


def _conv_short_kernel(a_ref, gl_ref, init_ref, zc_ref, w_ref, db_ref, lg_ref, lb_ref,
                       cv_ref, st_ref, uext, ybuf, *, tt, t_valid, ns):
    u = a_ref[...] * _sigmoid(gl_ref[...])
    for s in range(ns):
        ue = uext.at[s]
        ue[0:CONV_HALO] = init_ref[s]
        ue[CONV_HALO:CONV_HALO + tt] = u[s * tt:(s + 1) * tt]
        _depthwise_conv(ue, w_ref, db_ref, ybuf, tt, row0=s * tt)
        st_ref[s] = ue[pl.ds(t_valid, CONV_HALO), :]
    y = ybuf[...]
    mu = jnp.mean(y, axis=-1, keepdims=True)
    var = jnp.mean(jnp.square(y - mu), axis=-1, keepdims=True)
    ln = (y - mu) * lax.rsqrt(var + NORM_EPS) * lg_ref[...] + lb_ref[...]
    cv_ref[...] = (_silu(ln) * _silu(zc_ref[...])).astype(BF16)


def _conv_short(rest, u_init, w_pad, db, lg, lb, b, tt, t_valid, ns):
    assert b % ns == 0 and tt % SUBLANES == 0
    rows = ns * tt
    vec = lambda: pl.BlockSpec((1, CONV_DIM), lambda i: (0, 0))
    return pl.pallas_call(
        functools.partial(_conv_short_kernel, tt=tt, t_valid=t_valid, ns=ns),
        grid=(b // ns,),
        in_specs=[pl.BlockSpec((rows, CONV_DIM), lambda i: (i, _REST_A)),
                  pl.BlockSpec((rows, CONV_DIM), lambda i: (i, _REST_GL)),
                  pl.BlockSpec((ns, CONV_HALO, CONV_DIM), lambda i: (i, 0, 0)),
                  pl.BlockSpec((rows, CONV_DIM), lambda i: (i, _REST_ZC)),
                  pl.BlockSpec((CONV_HALO, CONV_DIM), lambda i: (0, 0)),
                  vec(), vec(), vec()],
        out_specs=[pl.BlockSpec((rows, CONV_DIM), lambda i: (i, 0)),
                   pl.BlockSpec((ns, CONV_HALO, CONV_DIM), lambda i: (i, 0, 0))],
        out_shape=[jax.ShapeDtypeStruct((b * tt, CONV_DIM), BF16),
                   jax.ShapeDtypeStruct((b, CONV_HALO, CONV_DIM), F32)],
        scratch_shapes=[pltpu.VMEM((ns, CONV_HALO + tt + SUBLANES, CONV_DIM), F32),
                        pltpu.VMEM((rows, CONV_DIM), F32)],
        compiler_params=_params("parallel"),
        name="conv_short",
    )(rest, rest, u_init, rest, w_pad, db, lg, lb)


def _outproj_kernel(on_ref, cv_ref, w_ref, x_ref, gate_ref, g_ref, y_ref):
    mix = _dot(on_ref[...], w_ref[0:NSA_WIDTH, :]) + _dot(cv_ref[...], w_ref[NSA_WIDTH:, :])
    nrm = mix * lax.rsqrt(jnp.mean(mix * mix, axis=-1, keepdims=True) + NORM_EPS) * g_ref[...]
    y_ref[...] = x_ref[...] + gate_ref[...] * nrm


def _outproj(o_nsa, cv, w_out_b, x, gate3, g_post, tm, rows_per_mod):
    m, d = x.shape
    mod_rows = gate3.shape[1]
    tiles_per_mod = rows_per_mod // tm
    return pl.pallas_call(
        _outproj_kernel,
        grid=(m // tm,),
        in_specs=[pl.BlockSpec((tm, NSA_WIDTH), lambda i: (i, 0)),
                  pl.BlockSpec((tm, CONV_DIM), lambda i: (i, 0)),
                  pl.BlockSpec((d, d), lambda i: (0, 0)),
                  pl.BlockSpec((tm, d), lambda i: (i, 0)),
                  pl.BlockSpec((None, mod_rows, d), lambda i: (i // tiles_per_mod, 0, 0)),
                  pl.BlockSpec((1, d), lambda i: (0, 0))],
        out_specs=pl.BlockSpec((tm, d), lambda i: (i, 0)),
        out_shape=jax.ShapeDtypeStruct((m, d), F32),
        compiler_params=_params("parallel"),
        name="outproj",
    )(o_nsa, cv, w_out_b, x, gate3, g_post)


def _overlap_matrix(n_cmp, n_sel, ncp):
    ci = np.arange(ncp)[:, None] * CMP_STRIDE
    sj = np.arange(LANES)[None, :] * SEL_BLOCK
    ov = (ci < sj + SEL_BLOCK) & (ci + CMP_BLOCK > sj)
    ov &= (np.arange(ncp)[:, None] < n_cmp) & (np.arange(LANES)[None, :] < n_sel)
    return jnp.asarray(ov.astype(np.float32), dtype=BF16)


def _overlap_matrix_t(n_cmp, n_sel, ncp):
    sr = _round_up(n_sel, 2 * SUBLANES)
    return _overlap_matrix(n_cmp, n_sel, ncp).T[:sr]


def _chunk_permutation():
    p = np.zeros((LANES, LANES), np.float32)
    for j in range(LANES // CMP_STRIDE):
        for l in range(CMP_STRIDE):
            p[l * (LANES // CMP_STRIDE) + j, CMP_STRIDE * j + l] = 1.0
    return jnp.asarray(p, dtype=BF16)


def _expand_matrix(n_keys):
    e = (np.arange(n_keys)[None, :] // SEL_BLOCK) == np.arange(LANES)[:, None]
    return jnp.asarray(e.astype(np.float32), dtype=BF16)


def _round_up(x, m):
    return -(-x // m) * m


def kernel(x_prompt, x_sample, c_prompt, c_sample, cache_kv, page_table, state_win_kv, state_conv, w_ada, b_ada, norm_pre, norm_post, w_in, cmp_pe, cmp_w1, cmp_b1, cmp_w2, cmp_b2, conv_dw, conv_db, conv_ln_g, conv_ln_b, w_out):
    assert w_ada.shape[0] == 1, "single-layer trunk"
    bp, tp, d = x_prompt.shape
    bs, ts, _ = x_sample.shape
    n_phys, page = cache_kv.shape[1], cache_kv.shape[2]
    n_pages = page_table.shape[1]
    past = n_pages * page
    assert ts <= SAMPLE_ROWS and d == D_MODEL

    w = w_in[0]
    c_q, c_kv, c_win, c_gate = 0, Q_COLS, Q_COLS + KV_COLS, Q_COLS + KV_COLS + WIN_COLS
    c_rest = c_gate + GATE_COLS
    w_main = _wprep(w.T)
    wg = w[:, c_gate:c_rest].reshape(d, N_BRANCH, N_KV_HEADS, HEADS_PER_GROUP).transpose(0, 2, 1, 3)
    wg = wg.reshape(d, N_KV_HEADS, N_BRANCH * HEADS_PER_GROUP)
    w_gate = jnp.pad(wg, ((0, 0), (0, 0), (0, LANES - N_BRANCH * HEADS_PER_GROUP))).reshape(d, 2 * LANES).astype(BF16)
    w_out_b = w_out[0].astype(BF16)
    w1 = cmp_w1[0]
    w1cat = jnp.concatenate([w1[:, :CMP_STRIDE].reshape(2, CMP_STRIDE * HEAD_DIM, HEAD_DIM),
                             w1[:, CMP_STRIDE:].reshape(2, CMP_STRIDE * HEAD_DIM, HEAD_DIM)], axis=2).astype(BF16)
    w2b = cmp_w2[0].astype(BF16)
    pe_rows = jnp.pad(cmp_pe[0].reshape(2, 2, CMP_STRIDE * HEAD_DIM), ((0, 0), (0, SUBLANES - 2), (0, 0)))
    b1 = cmp_b1[0].reshape(2, 1, HEAD_DIM)
    b2 = cmp_b2[0].reshape(2, 1, HEAD_DIM)
    conv_w_pad = jnp.pad(conv_dw[0], ((0, CONV_HALO - CONV_WIDTH), (0, 0)))
    db = conv_db[0].reshape(1, CONV_DIM)
    lg = conv_ln_g[0].reshape(1, CONV_DIM)
    lb = conv_ln_b[0].reshape(1, CONV_DIM)
    g_pre = norm_pre[0].reshape(1, d)
    g_post = norm_post[0].reshape(1, d)

    n_c = bp + bs
    c_all = jnp.pad(jnp.concatenate([c_prompt, c_sample], axis=0), ((0, _round_up(n_c, SUBLANES) - n_c), (0, 0)))
    mod = _ada(c_all, w_ada[0], b_ada[0])
    shift, scale, gate = mod[:, :d], mod[:, d:2 * d], mod[:, 2 * d:]
    cconst = _cmpconst(pe_rows, w1cat, b1)

    tm_p = 1024
    xp = x_prompt.reshape(bp * tp, d)
    q_p, kv_p, kvb_p, rest_p, winb_p, gate_p = _inproj(
        xp, g_pre, scale[:bp].reshape(bp, 1, d), shift[:bp].reshape(bp, 1, d), w_main, w_gate, tm_p, tp)
    n_cmp_p = (tp - CMP_BLOCK) // CMP_STRIDE + 1
    n_sel_p = -(-tp // SEL_BLOCK)
    ncp_p = _round_up(tp // CMP_STRIDE, LANES)
    kcv_p = _compress(kv_p.reshape(bp, tp * _KV_STREAMS, HEAD_DIM), w1cat, cconst, w2b, b2, ncp_p)
    tk = 512
    e_p = _expand_matrix(tp).reshape(LANES, tp // tk, tk).transpose(1, 0, 2)
    o_nsa_p = _attn_p(q_p, kcv_p, kvb_p.reshape(bp, tp, KV_COLS), winb_p.reshape(bp, tp, WIN_COLS), gate_p, rest_p,
                      _overlap_matrix_t(n_cmp_p, n_sel_p, ncp_p), e_p, bp, tp, n_cmp_p, n_sel_p, tk=tk)
    cv_p, st_p = _conv(rest_p, jnp.zeros((bp, CONV_HALO, CONV_DIM), F32), conv_w_pad, db, lg, lb, bp, tp, 256, 256)
    y_p = _outproj(o_nsa_p, cv_p, w_out_b, xp, gate[:bp].reshape(bp, 1, d), g_post, 512, tp)

    r = SAMPLE_ROWS
    xs = jnp.pad(x_sample, ((0, 0), (0, r - ts), (0, 0))).reshape(bs * r, d)
    per_row = lambda v: jnp.repeat(v[bp:bp + bs], r, axis=0)
    tm_s = min(512, bs * r)
    n_mod = (bs * r) // tm_s
    q_s, kv_s, kvb_s, rest_s, _, gate_s = _inproj(
        xs, g_pre, per_row(scale).reshape(n_mod, tm_s, d), per_row(shift).reshape(n_mod, tm_s, d),
        w_main, w_gate, tm_s, tm_s)
    total = past + ts
    n_cmp_s = (total - CMP_BLOCK) // CMP_STRIDE + 1
    n_sel_s = -(-total // SEL_BLOCK)
    assert n_cmp_s <= past // CMP_STRIDE - 1 + 1 and (n_cmp_s - 1) * CMP_STRIDE + CMP_BLOCK <= past
    ncp_s = _round_up(past // CMP_STRIDE, LANES)
    assert ncp_s == past // CMP_STRIDE
    n_ws = N_WIN_SLOTS * N_KV_HEADS
    o_nsa_s, win_state_s = _attn_s(
        page_table, cache_kv.reshape(n_phys, page * _KV_STREAMS, HEAD_DIM),
        state_win_kv.reshape(bs, WINDOW * n_ws, HEAD_DIM),
        q_s.reshape(bs, r, Q_COLS), kvb_s.reshape(bs, r, KV_COLS), rest_s.reshape(bs, r, _REST_COLS),
        gate_s.reshape(bs, r, 2 * LANES), w1cat, cconst, w2b, b2,
        _overlap_matrix(n_cmp_s, n_sel_s, ncp_s), _expand_matrix(past + LANES),
        n_cmp_s, n_sel_s, past, ts)
    u_init_s = jnp.pad(state_conv[0], ((0, 0), (CONV_HALO - (CONV_WIDTH - 1), 0), (0, 0)))
    cv_s, st_s = _conv_short(rest_s, u_init_s, conv_w_pad, db, lg, lb, bs, r, ts, math.gcd(bs, SUBLANES))
    y_s = _outproj(o_nsa_s.reshape(bs * r, NSA_WIDTH), cv_s, w_out_b, xs, per_row(gate).reshape(n_mod, tm_s, d),
                   g_post, tm_s, tm_s)

    keep = CONV_WIDTH - 1
    y_prompt = y_p.reshape(bp, tp, d)
    y_sample = y_s.reshape(bs, r, d)[:, :ts]
    kv_rows_prompt = kv_p.reshape(1, bp, tp, N_CACHE_SLOTS, N_KV_HEADS, HEAD_DIM)
    kv_rows_sample = kv_s.reshape(1, bs, r, N_CACHE_SLOTS, N_KV_HEADS, HEAD_DIM)[:, :, :ts]
    win_p = rest_p.reshape(bp, tp, _REST_COLS)[:, tp - min(WINDOW, tp):, _REST_WIN * WIN_COLS:]
    win_prompt = win_p.reshape(1, bp, min(WINDOW, tp), N_WIN_SLOTS, N_KV_HEADS, HEAD_DIM)
    win_sample = win_state_s.reshape(1, bs, WINDOW, N_WIN_SLOTS, N_KV_HEADS, HEAD_DIM)
    conv_prompt = st_p[:, CONV_HALO - keep:].reshape(1, bp, keep, CONV_DIM)
    conv_sample = st_s[:, CONV_HALO - keep:].reshape(1, bs, keep, CONV_DIM)
    return (y_prompt, y_sample, kv_rows_prompt, kv_rows_sample, win_prompt, win_sample, conv_prompt, conv_sample)
```
